```python
import math
import jax, jax.numpy as jnp
from jax import lax
import numpy as np

D_MODEL = 1024
BATCH = 16
SEQ = 2048
DEPTH = 1

ATTN_HEADS = 4
HEAD_DIM = 64
V_HEAD_DIM = 2 * HEAD_DIM
ATTN_WIDTH = ATTN_HEADS * V_HEAD_DIM
QK_WIDTH = ATTN_HEADS * 2 * HEAD_DIM
ROT_DIM = HEAD_DIM // 4
ROPE_THETA = 500000.0
Q_BLOCK = 128
LAMBDA_INIT_SCALE = 0.1

SSM_WIDTH = D_MODEL // 2
SSM_GROUP = 16
SSM_GROUPS = SSM_WIDTH // SSM_GROUP
SSM_STATE = 64
DT_MIN = 1e-3
DT_MAX = 1e-1

EPS = 1e-6

IN_SIZES = (QK_WIDTH, QK_WIDTH, ATTN_WIDTH, ATTN_WIDTH, SSM_WIDTH, SSM_WIDTH, D_MODEL, D_MODEL)
IN_WIDTH = sum(IN_SIZES)
IN_OFFSETS = tuple(int(o) for o in np.cumsum(IN_SIZES)[:-1])

kernel_name = "hybrid_diffattn_s5_encoder_block"


def rmsnorm(x, g):
    xf = x.astype(jnp.float32)
    inv = lax.rsqrt(jnp.mean(xf * xf, axis=-1, keepdims=True) + EPS)
    return (xf * inv).astype(x.dtype) * g


def partial_rope(t, positions):
    half = ROT_DIM // 2
    inv_freq = ROPE_THETA ** (-jnp.arange(half, dtype=jnp.float32) * 2.0 / ROT_DIM)
    ang = positions.astype(jnp.float32)[:, :, None] * inv_freq
    cos = jnp.cos(ang)[:, :, None, None, :].astype(t.dtype)
    sin = jnp.sin(ang)[:, :, None, None, :].astype(t.dtype)
    t1 = t[..., :half]
    t2 = t[..., half:ROT_DIM]
    rest = t[..., ROT_DIM:]
    return jnp.concatenate([t1 * cos - t2 * sin, t2 * cos + t1 * sin, rest], axis=-1)


def diff_attention(q, k, v, lam):
    b, s = q.shape[0], q.shape[1]
    nblk = s // Q_BLOCK
    scale = HEAD_DIM ** -0.5
    qb = q.reshape(b, nblk, Q_BLOCK, ATTN_HEADS, 2, HEAD_DIM).transpose(1, 0, 2, 3, 4, 5)

    def block(qblk):
        sc = jnp.einsum('bqhcd,bkhcd->bhcqk', qblk, k).astype(jnp.float32) * scale
        p = jax.nn.softmax(sc, axis=-1)
        w = p[:, :, 0] - lam * p[:, :, 1]
        return jnp.einsum('bhqk,bkhe->bqhe', w.astype(v.dtype), v)

    o = lax.map(block, qb)
    return o.transpose(1, 0, 2, 3, 4).reshape(b, s, ATTN_HEADS, V_HEAD_DIM)


def s5_direction(u, lam_re, lam_im, log_dt, b_re, b_im, c_re, c_im, reverse):
    f32 = jnp.float32
    dt = jnp.exp(log_dt.astype(f32))[:, None]
    lr = jnp.minimum(lam_re.astype(f32), -1e-4)
    li = lam_im.astype(f32)
    mag = jnp.exp(lr * dt)
    ab_re = mag * jnp.cos(li * dt)
    ab_im = mag * jnp.sin(li * dt)
    den = lr * lr + li * li
    nr = ab_re - 1.0
    ni = ab_im
    coef_re = (nr * lr + ni * li) / den
    coef_im = (ni * lr - nr * li) / den
    bf_re = b_re.astype(f32)
    bf_im = b_im.astype(f32)
    bb_re = coef_re[..., None] * bf_re - coef_im[..., None] * bf_im
    bb_im = coef_re[..., None] * bf_im + coef_im[..., None] * bf_re
    uf = u.astype(f32)
    bu_re = jnp.einsum('bsgh,gph->sbgp', uf, bb_re)
    bu_im = jnp.einsum('bsgh,gph->sbgp', uf, bb_im)
    s = u.shape[1]
    a_re = jnp.broadcast_to(ab_re[None, None], (s, 1) + ab_re.shape)
    a_im = jnp.broadcast_to(ab_im[None, None], (s, 1) + ab_im.shape)

    def combine(e_i, e_j):
        ar_i, ai_i, br_i, bi_i = e_i
        ar_j, ai_j, br_j, bi_j = e_j
        return (ar_j * ar_i - ai_j * ai_i,
                ar_j * ai_i + ai_j * ar_i,
                ar_j * br_i - ai_j * bi_i + br_j,
                ar_j * bi_i + ai_j * br_i + bi_j)

    _, _, x_re, x_im = lax.associative_scan(combine, (a_re, a_im, bu_re, bu_im),
                                            reverse=reverse, axis=0)
    y = (jnp.einsum('sbgp,ghp->bsgh', x_re, c_re.astype(f32))
         - jnp.einsum('sbgp,ghp->bsgh', x_im, c_im.astype(f32)))
    return y.astype(u.dtype)


def hybrid_layer(x, c, positions, layer_idx, w_ada, b_ada, g_pre, w_in, lam_qk, g_subln,
                 lam_re, lam_im, log_dt, b_re, b_im, c_re, c_im, d_skip,
                 w_glu, b_glu, w_up_attn, w_up_ssm, w_out):
    b, s, _ = x.shape
    mod = jax.nn.silu(c) @ w_ada + b_ada
    shift, scale, gate = jnp.split(mod, 3, axis=-1)
    h = rmsnorm(x, g_pre) * (1.0 + scale[:, None, :]) + shift[:, None, :]

    proj = h @ w_in
    q, k, v, z_a, u, z_s, g_a, g_s = jnp.split(proj, IN_OFFSETS, axis=-1)

    q = partial_rope(q.reshape(b, s, ATTN_HEADS, 2, HEAD_DIM), positions)
    k = partial_rope(k.reshape(b, s, ATTN_HEADS, 2, HEAD_DIM), positions)
    v = v.reshape(b, s, ATTN_HEADS, V_HEAD_DIM)
    lam_init = 0.8 - 0.6 * math.exp(-0.3 * layer_idx)
    lf = lam_qk.astype(jnp.float32)
    lam = jnp.exp(jnp.sum(lf[0] * lf[1])) - jnp.exp(jnp.sum(lf[2] * lf[3])) + lam_init
    o = diff_attention(q, k, v, lam)
    o = rmsnorm(o, g_subln) * (1.0 - lam_init)
    attn_branch = (o.reshape(b, s, ATTN_WIDTH) * jax.nn.silu(z_a)) @ w_up_attn

    ug = u.reshape(b, s, SSM_GROUPS, SSM_GROUP)
    y = (s5_direction(ug, lam_re[0], lam_im[0], log_dt[0], b_re[0], b_im[0], c_re[0], c_im[0], False)
         + s5_direction(ug, lam_re[1], lam_im[1], log_dt[1], b_re[1], b_im[1], c_re[1], c_im[1], True))
    y = y.reshape(b, s, SSM_WIDTH) + d_skip * u
    y = jax.nn.gelu(y)
    y = y * jax.nn.sigmoid(y @ w_glu + b_glu)
    ssm_branch = (y * jax.nn.silu(z_s)) @ w_up_ssm

    merged = jax.nn.sigmoid(g_a) * attn_branch + jax.nn.sigmoid(g_s) * ssm_branch
    return x + gate[:, None, :] * (merged @ w_out)


def setup_inputs(seed: int = 0) -> dict:
    key = jax.random.key(seed)
    ks = jax.random.split(key, 24)
    L, D = DEPTH, D_MODEL
    G, P, Hg = SSM_GROUPS, SSM_STATE, SSM_GROUP
    nrm = jax.random.normal
    f32 = jnp.float32
    x = nrm(ks[0], (BATCH, SEQ, D), f32)
    c = nrm(ks[1], (BATCH, D), f32)
    positions = jnp.broadcast_to(jnp.arange(SEQ, dtype=jnp.int32)[None, :], (BATCH, SEQ))
    w_ada = nrm(ks[2], (L, D, 3 * D), f32) * D ** -0.5
    b_ada = nrm(ks[3], (L, 3 * D), f32) * 0.01
    g_pre = 1.0 + 0.02 * nrm(ks[4], (L, D), f32)
    w_in = nrm(ks[5], (L, D, IN_WIDTH), f32) * D ** -0.5
    lam_qk = nrm(ks[6], (L, 4, HEAD_DIM), f32) * LAMBDA_INIT_SCALE
    g_subln = 1.0 + 0.02 * nrm(ks[7], (L, V_HEAD_DIM), f32)
    n_idx = jnp.arange(P, dtype=f32)
    ssm_lam_re = -0.5 + 0.01 * nrm(ks[8], (L, 2, G, P), f32)
    ssm_lam_im = math.pi * n_idx + 0.01 * nrm(ks[9], (L, 2, G, P), f32)
    ssm_log_dt = jax.random.uniform(ks[10], (L, 2, G), f32, math.log(DT_MIN), math.log(DT_MAX))
    ssm_b_re = nrm(ks[11], (L, 2, G, P, Hg), f32) * (2 * Hg) ** -0.5
    ssm_b_im = nrm(ks[12], (L, 2, G, P, Hg), f32) * (2 * Hg) ** -0.5
    ssm_c_re = nrm(ks[13], (L, 2, G, Hg, P), f32) * (2 * P) ** -0.5
    ssm_c_im = nrm(ks[14], (L, 2, G, Hg, P), f32) * (2 * P) ** -0.5
    ssm_d = nrm(ks[15], (L, SSM_WIDTH), f32)
    w_glu = nrm(ks[16], (L, SSM_WIDTH, SSM_WIDTH), f32) * SSM_WIDTH ** -0.5
    b_glu = nrm(ks[17], (L, SSM_WIDTH), f32) * 0.01
    w_up_attn = nrm(ks[18], (L, ATTN_WIDTH, D), f32) * ATTN_WIDTH ** -0.5
    w_up_ssm = nrm(ks[19], (L, SSM_WIDTH, D), f32) * SSM_WIDTH ** -0.5
    w_out = nrm(ks[20], (L, D, D), f32) * D ** -0.5
    g_final = 1.0 + 0.02 * nrm(ks[21], (D,), f32)
    return {"x": x, "c": c, "positions": positions, "w_ada": w_ada, "b_ada": b_ada,
            "g_pre": g_pre, "w_in": w_in, "lam_qk": lam_qk, "g_subln": g_subln,
            "ssm_lam_re": ssm_lam_re, "ssm_lam_im": ssm_lam_im, "ssm_log_dt": ssm_log_dt,
            "ssm_b_re": ssm_b_re, "ssm_b_im": ssm_b_im, "ssm_c_re": ssm_c_re, "ssm_c_im": ssm_c_im,
            "ssm_d": ssm_d, "w_glu": w_glu, "b_glu": b_glu, "w_up_attn": w_up_attn,
            "w_up_ssm": w_up_ssm, "w_out": w_out, "g_final": g_final}


def reference(x, c, positions, w_ada, b_ada, g_pre, w_in, lam_qk, g_subln,
              ssm_lam_re, ssm_lam_im, ssm_log_dt, ssm_b_re, ssm_b_im, ssm_c_re, ssm_c_im,
              ssm_d, w_glu, b_glu, w_up_attn, w_up_ssm, w_out, g_final):
    for l in range(DEPTH):
        x = hybrid_layer(x, c, positions, l, w_ada[l], b_ada[l], g_pre[l], w_in[l], lam_qk[l],
                         g_subln[l], ssm_lam_re[l], ssm_lam_im[l], ssm_log_dt[l],
                         ssm_b_re[l], ssm_b_im[l], ssm_c_re[l], ssm_c_im[l], ssm_d[l],
                         w_glu[l], b_glu[l], w_up_attn[l], w_up_ssm[l], w_out[l])
    return rmsnorm(x, g_final)
```

```python
import functools
import math

import jax
import jax.numpy as jnp
from jax import lax
from jax.experimental import pallas as pl
from jax.experimental.pallas import tpu as pltpu

ATTN_HEADS = 4
HEAD_DIM = 64
V_HEAD_DIM = 2 * HEAD_DIM
ROT_DIM = HEAD_DIM // 4
ROPE_THETA = 500000.0
SSM_GROUP = 16
SSM_STATE = 64
SSM_CHUNK = 16
EPS = 1e-6
LANES = 128
VMEM_LIMIT_BYTES = 56 * 1024 * 1024
ROW_BLOCK = 512
Q_ROWS = 256

_HI = lax.Precision.HIGHEST
_BF = jnp.bfloat16
_F32 = jnp.float32


def _nt_dot(a, b):
    return lax.dot_general(a, b, (((1,), (1,)), ((), ())), preferred_element_type=_F32)


def _dot(a, b):
    return jnp.dot(a, b, preferred_element_type=_F32)


def _mod_kernel(c_ref, w_ref, b_ref, o_ref):
    c = c_ref[...]
    o_ref[...] = jnp.dot(jax.nn.silu(c), w_ref[...], precision=_HI,
                         preferred_element_type=_F32) + b_ref[...]


def _modulation(c, w_ada, b_ada):
    bsz, d = c.shape
    n = w_ada.shape[1]
    tn = d
    return pl.pallas_call(
        _mod_kernel,
        grid=(n // tn,),
        in_specs=[pl.BlockSpec((bsz, d), lambda j: (0, 0)),
                  pl.BlockSpec((d, tn), lambda j: (0, j)),
                  pl.BlockSpec((1, tn), lambda j: (0, j))],
        out_specs=pl.BlockSpec((bsz, tn), lambda j: (0, j)),
        out_shape=jax.ShapeDtypeStruct((bsz, n), _F32),
        compiler_params=pltpu.CompilerParams(vmem_limit_bytes=VMEM_LIMIT_BYTES),
        name="adaln_mod",
    )(c, w_ada, b_ada.reshape(1, n))


def _inproj_kernel(x_ref, mod_ref, gpre_ref, w_ref, rc_ref, ra_ref, rb_ref,
                   q_ref, k_ref, v_ref, gates_ref, u_ref, *, d, qk_w, v_w, u_off, u_w):
    shift = mod_ref[0, :, 0:d]
    scale = mod_ref[0, :, d:2 * d]
    x = x_ref[0]
    inv = lax.rsqrt(jnp.mean(x * x, axis=-1, keepdims=True) + EPS)
    h = ((x * inv) * (gpre_ref[...] * (1.0 + scale)) + shift).astype(_BF)

    def rope(t):
        reps = qk_w // LANES
        rc = jnp.concatenate([rc_ref[0]] * reps, axis=1)
        ra = jnp.concatenate([ra_ref[0]] * reps, axis=1)
        rb = jnp.concatenate([rb_ref[0]] * reps, axis=1)
        half = ROT_DIM // 2
        up = pltpu.roll(t, qk_w - half, 1)
        dn = pltpu.roll(t, half, 1)
        return t * rc + up * ra + dn * rb

    q_ref[0] = rope(_dot(h, w_ref[:, 0:qk_w])).astype(_BF)
    k_ref[0] = rope(_dot(h, w_ref[:, qk_w:2 * qk_w])).astype(_BF)
    v_off = 2 * qk_w
    v_ref[0] = _dot(h, w_ref[:, v_off:v_off + v_w]).astype(_BF)
    u_ref[0] = _dot(h, w_ref[:, u_off:u_off + u_w])
    za_off = v_off + v_w
    gates_ref[0, :, 0:v_w] = _dot(h, w_ref[:, za_off:za_off + v_w])
    zs_off = u_off + u_w
    rest = w_ref.shape[1] - zs_off
    piece = 512
    for p in range(rest // piece):
        gates_ref[0, :, v_w + p * piece:v_w + (p + 1) * piece] = _dot(
            h, w_ref[:, zs_off + p * piece:zs_off + (p + 1) * piece])


def _inproj(x, mod3, g_pre, w_in_bf, rope_c, rope_a, rope_b, *, rows, qk_w, v_w, u_off, u_w):
    bsz, s, d = x.shape
    n_in = w_in_bf.shape[1]
    gates_w = n_in - 2 * qk_w - v_w - u_w
    kern = functools.partial(_inproj_kernel, d=d, qk_w=qk_w, v_w=v_w, u_off=u_off, u_w=u_w)
    seq_spec = lambda w: pl.BlockSpec((1, rows, w), lambda b, t: (b, t, 0))
    return pl.pallas_call(
        kern,
        grid=(bsz, s // rows),
        in_specs=[seq_spec(d),
                  pl.BlockSpec((1, 1, 3 * d), lambda b, t: (b, 0, 0)),
                  pl.BlockSpec((1, d), lambda b, t: (0, 0)),
                  pl.BlockSpec((d, n_in), lambda b, t: (0, 0)),
                  seq_spec(LANES), seq_spec(LANES), seq_spec(LANES)],
        out_specs=[seq_spec(qk_w), seq_spec(qk_w), seq_spec(v_w), seq_spec(gates_w), seq_spec(u_w)],
        out_shape=[jax.ShapeDtypeStruct((bsz, s, qk_w), _BF),
                   jax.ShapeDtypeStruct((bsz, s, qk_w), _BF),
                   jax.ShapeDtypeStruct((bsz, s, v_w), _BF),
                   jax.ShapeDtypeStruct((bsz, s, gates_w), _F32),
                   jax.ShapeDtypeStruct((bsz, s, u_w), _F32)],
        compiler_params=pltpu.CompilerParams(
            dimension_semantics=("arbitrary", "arbitrary"), vmem_limit_bytes=VMEM_LIMIT_BYTES),
        name="norm_inproj",
    )(x, mod3, g_pre.reshape(1, d), w_in_bf, rope_c, rope_a, rope_b)


def _attn_kernel(lamqk_ref, gsub_ref, q_ref, k_ref, v_ref, o_ref, *, lam_init):
    lf = lamqk_ref[...]
    lam = (jnp.exp(jnp.sum(lf[0:1, :] * lf[1:2, :], axis=-1, keepdims=True))
           - jnp.exp(jnp.sum(lf[2:3, :] * lf[3:4, :], axis=-1, keepdims=True)) + lam_init)
    q = q_ref[0]
    k = k_ref[0]
    v = v_ref[0]
    first = lax.broadcasted_iota(jnp.int32, (1, 2 * HEAD_DIM), 1) < HEAD_DIM
    zero = jnp.zeros_like(q)
    s1 = _nt_dot(jnp.where(first, q, zero), k)
    s2 = _nt_dot(jnp.where(first, zero, q), k)
    e1 = jnp.exp(s1 - jnp.max(s1, axis=-1, keepdims=True))
    e2 = jnp.exp(s2 - jnp.max(s2, axis=-1, keepdims=True))
    r1 = 1.0 / jnp.sum(e1, axis=-1, keepdims=True)
    r2 = lam / jnp.sum(e2, axis=-1, keepdims=True)
    w = e1 * r1 - e2 * r2
    o = _dot(w.astype(_BF), v)
    inv = lax.rsqrt(jnp.mean(o * o, axis=-1, keepdims=True) + EPS)
    o_ref[0] = ((o * inv) * gsub_ref[...]) * (1.0 - lam_init)


def _attention(lam_qk, g_subln, q, k, v, *, tq, lam_init):
    bsz, s, _ = q.shape
    hw = 2 * HEAD_DIM
    kern = functools.partial(_attn_kernel, lam_init=lam_init)
    return pl.pallas_call(
        kern,
        grid=(bsz, ATTN_HEADS, s // tq),
        in_specs=[pl.BlockSpec(lam_qk.shape, lambda b, h, i: (0, 0)),
                  pl.BlockSpec((1, V_HEAD_DIM), lambda b, h, i: (0, 0)),
                  pl.BlockSpec((1, tq, hw), lambda b, h, i: (b, i, h)),
                  pl.BlockSpec((1, s, hw), lambda b, h, i: (b, 0, h)),
                  pl.BlockSpec((1, s, V_HEAD_DIM), lambda b, h, i: (b, 0, h))],
        out_specs=pl.BlockSpec((1, tq, V_HEAD_DIM), lambda b, h, i: (b, i, h)),
        out_shape=jax.ShapeDtypeStruct((bsz, s, ATTN_HEADS * V_HEAD_DIM), _F32),
        compiler_params=pltpu.CompilerParams(
            dimension_semantics=("arbitrary", "arbitrary", "arbitrary"),
            vmem_limit_bytes=VMEM_LIMIT_BYTES),
        name="diff_attention",
    )(lam_qk, g_subln.reshape(1, V_HEAD_DIM), q, k, v)


def _ssm_kernel(ut_ref, tm_ref, pz_ref, py_ref, a_ref, yt_ref, at_ref, z_ref, xp_ref, *, bsz, n_c):
    cw = SSM_CHUNK * SSM_GROUP
    sw = 2 * SSM_STATE
    for gl in range(2):
        for j in range(bsz):
            a = ut_ref[j, :, gl * SSM_GROUP:(gl + 1) * SSM_GROUP, :].reshape(cw, n_c)
            at_ref[gl, j * n_c:(j + 1) * n_c, :] = a.T.astype(_BF)
    z = _dot(at_ref[0], pz_ref[0, 0]) + _dot(at_ref[1], pz_ref[0, 1])
    for comp in range(4):
        z_ref[comp] = z[:, comp * sw:(comp + 1) * sw]

    a_fr = a_ref[0, 0:1, :]
    a_fi = a_ref[0, 1:2, :]
    a_br = a_ref[0, 2:3, :]
    a_bi = a_ref[0, 3:4, :]

    def step(c, carry):
        fr, fi, br, bi = carry
        rows_f = pl.ds(c, bsz, stride=n_c)
        rows_b = pl.ds(n_c - 1 - c, bsz, stride=n_c)
        xp_ref[0, rows_f, :] = fr
        xp_ref[1, rows_f, :] = fi
        xp_ref[2, rows_b, :] = br
        xp_ref[3, rows_b, :] = bi
        zfr = z_ref[0, rows_f, :]
        zfi = z_ref[1, rows_f, :]
        zbr = z_ref[2, rows_b, :]
        zbi = z_ref[3, rows_b, :]
        return (a_fr * fr - a_fi * fi + zfr, a_fr * fi + a_fi * fr + zfi,
                a_br * br - a_bi * bi + zbr, a_br * bi + a_bi * br + zbi)

    zero = jnp.zeros((bsz, sw), _F32)
    lax.fori_loop(0, n_c, step, (zero, zero, zero, zero))

    for gl in range(2):
        for j in range(bsz):
            a = ut_ref[j, :, gl * SSM_GROUP:(gl + 1) * SSM_GROUP, :].reshape(cw, n_c)
            yt = _dot(tm_ref[gl], a.astype(_BF))
            xp = jnp.concatenate([xp_ref[comp, j * n_c:(j + 1) * n_c, :] for comp in range(4)],
                                 axis=1)
            yt = yt + _nt_dot(py_ref[0, gl], xp.astype(_BF))
            yt_ref[j, :, gl * SSM_GROUP:(gl + 1) * SSM_GROUP, :] = yt.reshape(
                SSM_CHUNK, SSM_GROUP, n_c)


def _ssm(ut, tm, pz, py, a16):
    bsz, _, u_w, n_c = ut.shape
    n_pairs = u_w // (2 * SSM_GROUP)
    cw = SSM_CHUNK * SSM_GROUP
    st_w = 8 * SSM_STATE
    kern = functools.partial(_ssm_kernel, bsz=bsz, n_c=n_c)
    blk = pl.BlockSpec((bsz, SSM_CHUNK, 2 * SSM_GROUP, n_c), lambda g: (0, 0, g, 0))
    return pl.pallas_call(
        kern,
        grid=(n_pairs,),
        in_specs=[blk,
                  pl.BlockSpec((2, cw, cw), lambda g: (g, 0, 0)),
                  pl.BlockSpec((1, 2, cw, st_w), lambda g: (g, 0, 0, 0)),
                  pl.BlockSpec((1, 2, cw, st_w), lambda g: (g, 0, 0, 0)),
                  pl.BlockSpec((1, 4, 2 * SSM_STATE), lambda g: (g, 0, 0))],
        out_specs=blk,
        out_shape=jax.ShapeDtypeStruct(ut.shape, _F32),
        scratch_shapes=[pltpu.VMEM((2, bsz * n_c, cw), _BF),
                        pltpu.VMEM((4, bsz * n_c, 2 * SSM_STATE), _F32),
                        pltpu.VMEM((4, bsz * n_c, 2 * SSM_STATE), _F32)],
        compiler_params=pltpu.CompilerParams(
            dimension_semantics=("arbitrary",), vmem_limit_bytes=VMEM_LIMIT_BYTES),
        name="s5_chunked",
    )(ut, tm, pz, py, a16)


def _post_kernel(o_ref, gates_ref, u_ref, y_ref, x_ref, mod_ref, dskip_ref, bglu_ref, gfin_ref,
                 wua_ref, wglu_ref, wus_ref, wout_ref, out_ref, *, d, aw, sw):
    y = jax.nn.gelu(y_ref[0] + dskip_ref[...] * u_ref[0])
    y = y * jax.nn.sigmoid(_dot(y.astype(_BF), wglu_ref[...]) + bglu_ref[...])
    s_br = _dot((y * jax.nn.silu(gates_ref[0, :, aw:aw + sw])).astype(_BF), wus_ref[...])
    a_br = _dot((o_ref[0] * jax.nn.silu(gates_ref[0, :, 0:aw])).astype(_BF), wua_ref[...])
    g_off = aw + sw
    merged = (jax.nn.sigmoid(gates_ref[0, :, g_off:g_off + d]) * a_br
              + jax.nn.sigmoid(gates_ref[0, :, g_off + d:g_off + 2 * d]) * s_br)
    r = _dot(merged.astype(_BF), wout_ref[...])
    xo = x_ref[0] + mod_ref[0, :, 2 * d:3 * d] * r
    inv = lax.rsqrt(jnp.mean(xo * xo, axis=-1, keepdims=True) + EPS)
    out_ref[0] = (xo * inv) * gfin_ref[...]


def _post(o, gates, u, y, x, mod3, d_skip, b_glu, g_final, wua, wglu, wus, wout, *, rows):
    bsz, s, d = x.shape
    aw = o.shape[2]
    sw = u.shape[2]
    kern = functools.partial(_post_kernel, d=d, aw=aw, sw=sw)
    full = lambda a: pl.BlockSpec(a.shape, lambda b, t: (0,) * a.ndim)
    seq_spec = lambda w: pl.BlockSpec((1, rows, w), lambda b, t: (b, t, 0))
    dsk = d_skip.reshape(1, sw)
    bgl = b_glu.reshape(1, sw)
    gfi = g_final.reshape(1, d)
    return pl.pallas_call(
        kern,
        grid=(bsz, s // rows),
        in_specs=[seq_spec(aw), seq_spec(gates.shape[2]), seq_spec(sw), seq_spec(sw), seq_spec(d),
                  pl.BlockSpec((1, 1, 3 * d), lambda b, t: (b, 0, 0)),
                  full(dsk), full(bgl), full(gfi), full(wua), full(wglu), full(wus), full(wout)],
        out_specs=seq_spec(d),
        out_shape=jax.ShapeDtypeStruct(x.shape, _F32),
        compiler_params=pltpu.CompilerParams(
            dimension_semantics=("arbitrary", "arbitrary"), vmem_limit_bytes=VMEM_LIMIT_BYTES),
        name="merge_out",
    )(o, gates, u, y, x, mod3, dsk, bgl, gfi, wua, wglu, wus, wout)


def _rope_tables(positions):
    bsz, s = positions.shape
    half = ROT_DIM // 2
    inv_freq = ROPE_THETA ** (-jnp.arange(half, dtype=_F32) * 2.0 / ROT_DIM)
    ang = positions.astype(_F32)[:, :, None] * inv_freq
    cos, sin = jnp.cos(ang), jnp.sin(ang)
    one = jnp.ones((bsz, s, HEAD_DIM - ROT_DIM), _F32)
    zero_r = jnp.zeros((bsz, s, HEAD_DIM - ROT_DIM), _F32)
    zero_h = jnp.zeros_like(sin)
    reps = LANES // HEAD_DIM
    tile = lambda parts: jnp.tile(jnp.concatenate(parts, axis=-1), (1, 1, reps))
    return (tile([cos, cos, one]), tile([-sin, zero_h, zero_r]), tile([zero_h, sin, zero_r]))


def _ssm_operators(lam_re, lam_im, log_dt, b_re, b_im, c_re, c_im):
    n_g = lam_re.shape[1]
    el = SSM_CHUNK
    dt = jnp.exp(log_dt.astype(_F32))[..., None]
    lr = jnp.minimum(lam_re.astype(_F32), -1e-4)
    li = lam_im.astype(_F32)
    kk = jnp.arange(el + 1, dtype=_F32)[:, None, None, None]
    mag = jnp.exp(kk * (lr * dt))
    pr = mag * jnp.cos(kk * (li * dt))
    pi = mag * jnp.sin(kk * (li * dt))
    den = lr * lr + li * li
    nr, ni = pr[1] - 1.0, pi[1]
    coef_re = (nr * lr + ni * li) / den
    coef_im = (ni * lr - nr * li) / den
    bbr = coef_re[..., None] * b_re - coef_im[..., None] * b_im
    bbi = coef_re[..., None] * b_im + coef_im[..., None] * b_re
    abr = pr[..., None] * bbr - pi[..., None] * bbi
    abi = pr[..., None] * bbi + pi[..., None] * bbr
    kern = (jnp.einsum('eghp,kegpi->keghi', c_re, abr[:el], precision=_HI)
            - jnp.einsum('eghp,kegpi->keghi', c_im, abi[:el], precision=_HI))
    lag = jnp.arange(el)[:, None] - jnp.arange(el)[None, :]
    tf = jnp.where((lag >= 0)[:, :, None, None, None], kern[jnp.clip(lag, 0, el - 1), 0], 0.0)
    tb = jnp.where((lag <= 0)[:, :, None, None, None], kern[jnp.clip(-lag, 0, el - 1), 1], 0.0)
    tm = (tf + tb).transpose(2, 0, 3, 1, 4).reshape(n_g, el * SSM_GROUP, el * SSM_GROUP)

    def pair_layout(m):
        rows = m.shape[1]
        m = m.reshape(n_g // 2, 2, rows, 4, 1, SSM_STATE)
        slot = jnp.eye(2, dtype=m.dtype).reshape(1, 2, 1, 1, 2, 1)
        return (m * slot).reshape(n_g // 2, 2, rows, 8 * SSM_STATE)

    rev = el - 1 - jnp.arange(el)
    fwd = jnp.arange(el)
    pz = jnp.stack([abr[rev, 0], abi[rev, 0], abr[fwd, 1], abi[fwd, 1]], axis=0)
    pz = pz.transpose(2, 1, 4, 0, 3).reshape(n_g, el * SSM_GROUP, 4, SSM_STATE)

    def c_pow(e, ks):
        cr, ci = c_re[e][None], c_im[e][None]
        p_r, p_i = pr[ks, e][:, :, None, :], pi[ks, e][:, :, None, :]
        return cr * p_r - ci * p_i, cr * p_i + ci * p_r

    fr_, fi_ = c_pow(0, fwd + 1)
    br_, bi_ = c_pow(1, el - fwd)
    py = jnp.stack([fr_, -fi_, br_, -bi_], axis=0)
    py = py.transpose(2, 1, 3, 0, 4).reshape(n_g, el * SSM_GROUP, 4, SSM_STATE)
    a16 = jnp.stack([pr[el, 0], pi[el, 0], pr[el, 1], pi[el, 1]], axis=1)
    a16 = a16.reshape(n_g // 2, 2, 4, SSM_STATE).transpose(0, 2, 1, 3).reshape(
        n_g // 2, 4, 2 * SSM_STATE)
    return tm.astype(_BF), pair_layout(pz).astype(_BF), pair_layout(py).astype(_BF), a16


def _layer(x, c, tabs, layer_idx, w_ada, b_ada, g_pre, w_in, lam_qk, g_subln,
           lam_re, lam_im, log_dt, b_re, b_im, c_re, c_im, d_skip,
           w_glu, b_glu, w_up_attn, w_up_ssm, w_out, g_final):
    bsz, s, d = x.shape
    n_c = s // SSM_CHUNK
    qk_w = ATTN_HEADS * 2 * HEAD_DIM
    v_w = ATTN_HEADS * V_HEAD_DIM
    u_w = lam_re.shape[1] * SSM_GROUP
    u_off = 2 * qk_w + 2 * v_w
    lam_init = 0.8 - 0.6 * math.exp(-0.3 * layer_idx)
    mod3 = _modulation(c, w_ada, b_ada).reshape(bsz, 1, 3 * d)
    col_scale = jnp.concatenate([jnp.full((qk_w,), HEAD_DIM ** -0.5, _F32),
                                 jnp.ones((w_in.shape[1] - qk_w,), _F32)])
    w_in_bf = (w_in * col_scale).astype(_BF)
    rows = min(ROW_BLOCK, s)
    q, k, v, gates, u = _inproj(x, mod3, g_pre, w_in_bf, *tabs, rows=rows,
                                qk_w=qk_w, v_w=v_w, u_off=u_off, u_w=u_w)
    o = _attention(lam_qk, g_subln, q, k, v, tq=min(Q_ROWS, s), lam_init=lam_init)
    ut = u.reshape(bsz, n_c, SSM_CHUNK, u_w).transpose(0, 2, 3, 1)
    yt = _ssm(ut, *_ssm_operators(lam_re, lam_im, log_dt, b_re, b_im, c_re, c_im))
    y = yt.transpose(0, 3, 1, 2).reshape(bsz, s, u_w)
    return _post(o, gates, u, y, x, mod3, d_skip, b_glu, g_final,
                 w_up_attn.astype(_BF), w_glu.astype(_BF), w_up_ssm.astype(_BF), w_out.astype(_BF),
                 rows=rows)


def kernel(x, c, positions, w_ada, b_ada, g_pre, w_in, lam_qk, g_subln, ssm_lam_re, ssm_lam_im,
           ssm_log_dt, ssm_b_re, ssm_b_im, ssm_c_re, ssm_c_im, ssm_d, w_glu, b_glu, w_up_attn,
           w_up_ssm, w_out, g_final):
    depth = w_ada.shape[0]
    assert depth == 1, "the final RMSNorm is fused into the single layer's epilogue"
    assert x.shape[1] % (SSM_CHUNK * 8) == 0
    return _layer(x, c, _rope_tables(positions), 0, w_ada[0], b_ada[0], g_pre[0], w_in[0],
                  lam_qk[0], g_subln[0], ssm_lam_re[0], ssm_lam_im[0], ssm_log_dt[0], ssm_b_re[0],
                  ssm_b_im[0], ssm_c_re[0], ssm_c_im[0], ssm_d[0], w_glu[0], b_glu[0],
                  w_up_attn[0], w_up_ssm[0], w_out[0], g_final)
```

```python
import functools
import math

import jax
import jax.numpy as jnp
import numpy as np
from jax import lax
from jax.experimental import pallas as pl
from jax.experimental.pallas import tpu as pltpu

ATTN_HEADS = 4
HEAD_DIM = 64
V_HEAD_DIM = 2 * HEAD_DIM
ROT_DIM = HEAD_DIM // 4
ROPE_THETA = 500000.0
SSM_GROUP = 16
SSM_STATE = 64
SSM_CHUNK = 16
EPS = 1e-6
LANES = 128
VMEM_LIMIT_BYTES = 56 * 1024 * 1024
ROW_BLOCK = 512
Q_ROWS = 512
KEY_BLOCK = 256
VT_PAD = 16

_HI = lax.Precision.HIGHEST
_BF = jnp.bfloat16
_F32 = jnp.float32


def _nt_dot(a, b):
    return lax.dot_general(a, b, (((1,), (1,)), ((), ())), preferred_element_type=_F32)


def _dot(a, b):
    return jnp.dot(a, b, preferred_element_type=_F32)


def _mod_kernel(c_ref, w_ref, b_ref, o_ref):
    c = c_ref[...]
    o_ref[...] = jnp.dot(jax.nn.silu(c), w_ref[...], precision=_HI,
                         preferred_element_type=_F32) + b_ref[...]


def _modulation(c, w_ada, b_ada):
    bsz, d = c.shape
    n = w_ada.shape[1]
    tn = d
    return pl.pallas_call(
        _mod_kernel,
        grid=(n // tn,),
        in_specs=[pl.BlockSpec((bsz, d), lambda j: (0, 0)),
                  pl.BlockSpec((d, tn), lambda j: (0, j)),
                  pl.BlockSpec((1, tn), lambda j: (0, j))],
        out_specs=pl.BlockSpec((bsz, tn), lambda j: (0, j)),
        out_shape=jax.ShapeDtypeStruct((bsz, n), _F32),
        compiler_params=pltpu.CompilerParams(vmem_limit_bytes=VMEM_LIMIT_BYTES),
        name="adaln_mod",
    )(c, w_ada, b_ada.reshape(1, n))


def _inproj_kernel(x_ref, mod_ref, gpre_ref, w_ref, wvt_ref, rc_ref, ra_ref, rb_ref,
                   qt_ref, k_ref, vt_ref, gates_ref, u_ref, *, d, qk_w, v_w, u_off, u_w):
    shift = mod_ref[0, :, 0:d]
    scale = mod_ref[0, :, d:2 * d]
    x = x_ref[0]
    inv = lax.rsqrt(jnp.mean(x * x, axis=-1, keepdims=True) + EPS)
    h = ((x * inv) * (gpre_ref[...] * (1.0 + scale)) + shift).astype(_BF)

    def rope(t):
        reps = qk_w // LANES
        rc = jnp.concatenate([rc_ref[0]] * reps, axis=1)
        ra = jnp.concatenate([ra_ref[0]] * reps, axis=1)
        rb = jnp.concatenate([rb_ref[0]] * reps, axis=1)
        half = ROT_DIM // 2
        up = pltpu.roll(t, qk_w - half, 1)
        dn = pltpu.roll(t, half, 1)
        return t * rc + up * ra + dn * rb

    qt = rope(_dot(h, w_ref[:, 0:qk_w])).T.astype(_BF)
    hw = 2 * HEAD_DIM
    for hd in range(qk_w // hw):
        qt_ref[0, hd] = qt[hd * hw:(hd + 1) * hw, :]
    k_ref[0] = rope(_dot(h, w_ref[:, qk_w:2 * qk_w])).astype(_BF)
    v_off = 2 * qk_w
    vt = _nt_dot(wvt_ref[...], h).astype(_BF)
    rows = h.shape[0]
    pad = (lax.broadcasted_iota(jnp.int32, (VT_PAD, KEY_BLOCK), 0) == 0).astype(_BF)
    for hd in range(v_w // V_HEAD_DIM):
        for c in range(rows // KEY_BLOCK):
            vt_ref[0, hd, c, 0:V_HEAD_DIM, :] = vt[hd * V_HEAD_DIM:(hd + 1) * V_HEAD_DIM,
                                                   c * KEY_BLOCK:(c + 1) * KEY_BLOCK]
            vt_ref[0, hd, c, V_HEAD_DIM:V_HEAD_DIM + VT_PAD, :] = pad
    u_ref[0] = _dot(h, w_ref[:, u_off:u_off + u_w])
    za_off = v_off + v_w
    gates_ref[0, :, 0:v_w] = _dot(h, w_ref[:, za_off:za_off + v_w])
    zs_off = u_off + u_w
    rest = w_ref.shape[1] - zs_off
    piece = 512
    for p in range(rest // piece):
        gates_ref[0, :, v_w + p * piece:v_w + (p + 1) * piece] = _dot(
            h, w_ref[:, zs_off + p * piece:zs_off + (p + 1) * piece])


def _inproj(x, mod3, g_pre, w_in_bf, wvt_bf, rope_c, rope_a, rope_b, *, rows, qk_w, v_w, u_off, u_w):
    bsz, s, d = x.shape
    n_in = w_in_bf.shape[1]
    gates_w = n_in - 2 * qk_w - v_w - u_w
    n_heads = v_w // V_HEAD_DIM
    vt_rows = V_HEAD_DIM + VT_PAD
    kern = functools.partial(_inproj_kernel, d=d, qk_w=qk_w, v_w=v_w, u_off=u_off, u_w=u_w)
    seq_spec = lambda w: pl.BlockSpec((1, rows, w), lambda b, t: (b, t, 0))
    return pl.pallas_call(
        kern,
        grid=(bsz, s // rows),
        in_specs=[seq_spec(d),
                  pl.BlockSpec((1, 1, 3 * d), lambda b, t: (b, 0, 0)),
                  pl.BlockSpec((1, d), lambda b, t: (0, 0)),
                  pl.BlockSpec((d, n_in), lambda b, t: (0, 0)),
                  pl.BlockSpec((v_w, d), lambda b, t: (0, 0)),
                  seq_spec(LANES), seq_spec(LANES), seq_spec(LANES)],
        out_specs=[pl.BlockSpec((1, n_heads, 2 * HEAD_DIM, rows), lambda b, t: (b, 0, 0, t)),
                   seq_spec(qk_w),
                   pl.BlockSpec((1, n_heads, rows // KEY_BLOCK, vt_rows, KEY_BLOCK),
                                lambda b, t: (b, 0, t, 0, 0)),
                   seq_spec(gates_w), seq_spec(u_w)],
        out_shape=[jax.ShapeDtypeStruct((bsz, n_heads, 2 * HEAD_DIM, s), _BF),
                   jax.ShapeDtypeStruct((bsz, s, qk_w), _BF),
                   jax.ShapeDtypeStruct((bsz, n_heads, s // KEY_BLOCK, vt_rows, KEY_BLOCK), _BF),
                   jax.ShapeDtypeStruct((bsz, s, gates_w), _F32),
                   jax.ShapeDtypeStruct((bsz, s, u_w), _F32)],
        compiler_params=pltpu.CompilerParams(
            dimension_semantics=("arbitrary", "arbitrary"), vmem_limit_bytes=VMEM_LIMIT_BYTES),
        name="norm_inproj",
    )(x, mod3, g_pre.reshape(1, d), w_in_bf, wvt_bf, rope_c, rope_a, rope_b)


def _attn_kernel(lamqk_ref, gsub_ref, qt_ref, k_ref, vt_ref, o_ref,
                 s1_ref, s2_ref, e1_ref, e2_ref, *, lam_init):
    n_kb, tk, tq = s1_ref.shape
    lf = lamqk_ref[...]
    lam = (jnp.exp(jnp.sum(lf[0:1, :] * lf[1:2, :], axis=-1, keepdims=True))
           - jnp.exp(jnp.sum(lf[2:3, :] * lf[3:4, :], axis=-1, keepdims=True)) + lam_init)
    qt = qt_ref[0, 0]
    first = lax.broadcasted_iota(jnp.int32, (2 * HEAD_DIM, 1), 0) < HEAD_DIM
    zero = jnp.zeros_like(qt)
    q1t = jnp.where(first, qt, zero)
    q2t = jnp.where(first, zero, qt)

    def scores(kb, qmt, s_ref, m):
        st = _dot(k_ref[0, kb * tk:(kb + 1) * tk, :], qmt)
        s_ref[kb] = st
        return jnp.maximum(m, jnp.max(st.reshape(tk // 8, 8, tq), axis=0))

    def probs(kb, s_ref, m_row, e_ref):
        e_ref[kb] = jnp.exp(s_ref[kb] - m_row).astype(_BF)

    def values(kb, e_ref):
        return _dot(vt_ref[0, 0, kb], e_ref[kb])

    m1 = m2 = jnp.full((8, tq), -jnp.inf, _F32)
    for kb in range(n_kb):
        m1 = scores(kb, q1t, s1_ref, m1)
    m1 = jnp.max(m1, axis=0, keepdims=True)
    for kb in range(n_kb):
        m2 = scores(kb, q2t, s2_ref, m2)
        probs(kb, s1_ref, m1, e1_ref)
    m2 = jnp.max(m2, axis=0, keepdims=True)
    ev1 = ev2 = None
    for kb in range(n_kb):
        probs(kb, s2_ref, m2, e2_ref)
        ev1 = values(kb, e1_ref) if ev1 is None else ev1 + values(kb, e1_ref)
    for kb in range(n_kb):
        ev2 = values(kb, e2_ref) if ev2 is None else ev2 + values(kb, e2_ref)

    def normalised(ev):
        return ev[0:V_HEAD_DIM, :] * (1.0 / ev[V_HEAD_DIM:V_HEAD_DIM + 1, :])

    ot = normalised(ev1) - lam * normalised(ev2)
    inv = lax.rsqrt(jnp.mean(ot * ot, axis=0, keepdims=True) + EPS)
    o_ref[0] = (((ot * inv) * gsub_ref[...]) * (1.0 - lam_init)).T


def _attention(lam_qk, g_subln, qt, k, vt, *, tq, lam_init):
    bsz, n_heads, hw, s = qt.shape
    _, _, n_kb, vt_rows, tk = vt.shape
    kern = functools.partial(_attn_kernel, lam_init=lam_init)
    return pl.pallas_call(
        kern,
        grid=(bsz, n_heads, s // tq),
        in_specs=[pl.BlockSpec(lam_qk.shape, lambda b, h, i: (0, 0)),
                  pl.BlockSpec((V_HEAD_DIM, 1), lambda b, h, i: (0, 0)),
                  pl.BlockSpec((1, 1, hw, tq), lambda b, h, i: (b, h, 0, i)),
                  pl.BlockSpec((1, s, hw), lambda b, h, i: (b, 0, h)),
                  pl.BlockSpec((1, 1, n_kb, vt_rows, tk), lambda b, h, i: (b, h, 0, 0, 0))],
        out_specs=pl.BlockSpec((1, tq, V_HEAD_DIM), lambda b, h, i: (b, i, h)),
        out_shape=jax.ShapeDtypeStruct((bsz, s, n_heads * V_HEAD_DIM), _F32),
        scratch_shapes=[pltpu.VMEM((n_kb, tk, tq), _F32), pltpu.VMEM((n_kb, tk, tq), _F32),
                        pltpu.VMEM((n_kb, tk, tq), _BF), pltpu.VMEM((n_kb, tk, tq), _BF)],
        compiler_params=pltpu.CompilerParams(
            dimension_semantics=("arbitrary", "arbitrary", "arbitrary"),
            vmem_limit_bytes=VMEM_LIMIT_BYTES),
        name="diff_attention",
    )(lam_qk, g_subln.reshape(V_HEAD_DIM, 1), qt, k, vt)


def _ssm_kernel(ut_ref, tm_ref, pz_ref, py_ref, a_ref, yt_ref, at_ref, z_ref, xp_ref, *, bsz, n_c):
    cw = SSM_CHUNK * SSM_GROUP
    sw = 2 * SSM_STATE
    for gl in range(2):
        for j in range(bsz):
            a = ut_ref[j, :, gl * SSM_GROUP:(gl + 1) * SSM_GROUP, :].reshape(cw, n_c)
            at_ref[gl, j * n_c:(j + 1) * n_c, :] = a.T.astype(_BF)
    z = _dot(at_ref[0], pz_ref[0, 0]) + _dot(at_ref[1], pz_ref[0, 1])
    for comp in range(4):
        z_ref[comp] = z[:, comp * sw:(comp + 1) * sw]

    a_fr = a_ref[0, 0:1, :]
    a_fi = a_ref[0, 1:2, :]
    a_br = a_ref[0, 2:3, :]
    a_bi = a_ref[0, 3:4, :]

    def step(c, carry):
        fr, fi, br, bi = carry
        rows_f = pl.ds(c, bsz, stride=n_c)
        rows_b = pl.ds(n_c - 1 - c, bsz, stride=n_c)
        xp_ref[0, rows_f, :] = fr
        xp_ref[1, rows_f, :] = fi
        xp_ref[2, rows_b, :] = br
        xp_ref[3, rows_b, :] = bi
        zfr = z_ref[0, rows_f, :]
        zfi = z_ref[1, rows_f, :]
        zbr = z_ref[2, rows_b, :]
        zbi = z_ref[3, rows_b, :]
        return (a_fr * fr - a_fi * fi + zfr, a_fr * fi + a_fi * fr + zfi,
                a_br * br - a_bi * bi + zbr, a_br * bi + a_bi * br + zbi)

    zero = jnp.zeros((bsz, sw), _F32)
    lax.fori_loop(0, n_c, step, (zero, zero, zero, zero))

    for gl in range(2):
        for j in range(bsz):
            a = ut_ref[j, :, gl * SSM_GROUP:(gl + 1) * SSM_GROUP, :].reshape(cw, n_c)
            yt = _dot(tm_ref[gl], a.astype(_BF))
            xp = jnp.concatenate([xp_ref[comp, j * n_c:(j + 1) * n_c, :] for comp in range(4)],
                                 axis=1)
            yt = yt + _nt_dot(py_ref[0, gl], xp.astype(_BF))
            yt_ref[j, :, gl * SSM_GROUP:(gl + 1) * SSM_GROUP, :] = yt.reshape(
                SSM_CHUNK, SSM_GROUP, n_c)


def _ssm(ut, tm, pz, py, a16):
    bsz, _, u_w, n_c = ut.shape
    n_pairs = u_w // (2 * SSM_GROUP)
    cw = SSM_CHUNK * SSM_GROUP
    st_w = 8 * SSM_STATE
    kern = functools.partial(_ssm_kernel, bsz=bsz, n_c=n_c)
    blk = pl.BlockSpec((bsz, SSM_CHUNK, 2 * SSM_GROUP, n_c), lambda g: (0, 0, g, 0))
    return pl.pallas_call(
        kern,
        grid=(n_pairs,),
        in_specs=[blk,
                  pl.BlockSpec((2, cw, cw), lambda g: (g, 0, 0)),
                  pl.BlockSpec((1, 2, cw, st_w), lambda g: (g, 0, 0, 0)),
                  pl.BlockSpec((1, 2, cw, st_w), lambda g: (g, 0, 0, 0)),
                  pl.BlockSpec((1, 4, 2 * SSM_STATE), lambda g: (g, 0, 0))],
        out_specs=blk,
        out_shape=jax.ShapeDtypeStruct(ut.shape, _F32),
        scratch_shapes=[pltpu.VMEM((2, bsz * n_c, cw), _BF),
                        pltpu.VMEM((4, bsz * n_c, 2 * SSM_STATE), _F32),
                        pltpu.VMEM((4, bsz * n_c, 2 * SSM_STATE), _F32)],
        compiler_params=pltpu.CompilerParams(
            dimension_semantics=("arbitrary",), vmem_limit_bytes=VMEM_LIMIT_BYTES),
        name="s5_chunked",
    )(ut, tm, pz, py, a16)


def _post_kernel(o_ref, gates_ref, u_ref, y_ref, x_ref, mod_ref, dskip_ref, bglu_ref, gfin_ref,
                 wua_ref, wglu_ref, wus_ref, wout_ref, out_ref, *, d, aw, sw):
    y = jax.nn.gelu(y_ref[0] + dskip_ref[...] * u_ref[0])
    y = y * jax.nn.sigmoid(_dot(y.astype(_BF), wglu_ref[...]) + bglu_ref[...])
    s_br = _dot((y * jax.nn.silu(gates_ref[0, :, aw:aw + sw])).astype(_BF), wus_ref[...])
    a_br = _dot((o_ref[0] * jax.nn.silu(gates_ref[0, :, 0:aw])).astype(_BF), wua_ref[...])
    g_off = aw + sw
    merged = (jax.nn.sigmoid(gates_ref[0, :, g_off:g_off + d]) * a_br
              + jax.nn.sigmoid(gates_ref[0, :, g_off + d:g_off + 2 * d]) * s_br)
    r = _dot(merged.astype(_BF), wout_ref[...])
    xo = x_ref[0] + mod_ref[0, :, 2 * d:3 * d] * r
    inv = lax.rsqrt(jnp.mean(xo * xo, axis=-1, keepdims=True) + EPS)
    out_ref[0] = (xo * inv) * gfin_ref[...]


def _post(o, gates, u, y, x, mod3, d_skip, b_glu, g_final, wua, wglu, wus, wout, *, rows):
    bsz, s, d = x.shape
    aw = o.shape[2]
    sw = u.shape[2]
    kern = functools.partial(_post_kernel, d=d, aw=aw, sw=sw)
    full = lambda a: pl.BlockSpec(a.shape, lambda b, t: (0,) * a.ndim)
    seq_spec = lambda w: pl.BlockSpec((1, rows, w), lambda b, t: (b, t, 0))
    dsk = d_skip.reshape(1, sw)
    bgl = b_glu.reshape(1, sw)
    gfi = g_final.reshape(1, d)
    return pl.pallas_call(
        kern,
        grid=(bsz, s // rows),
        in_specs=[seq_spec(aw), seq_spec(gates.shape[2]), seq_spec(sw), seq_spec(sw), seq_spec(d),
                  pl.BlockSpec((1, 1, 3 * d), lambda b, t: (b, 0, 0)),
                  full(dsk), full(bgl), full(gfi), full(wua), full(wglu), full(wus), full(wout)],
        out_specs=seq_spec(d),
        out_shape=jax.ShapeDtypeStruct(x.shape, _F32),
        compiler_params=pltpu.CompilerParams(
            dimension_semantics=("arbitrary", "arbitrary"), vmem_limit_bytes=VMEM_LIMIT_BYTES),
        name="merge_out",
    )(o, gates, u, y, x, mod3, dsk, bgl, gfi, wua, wglu, wus, wout)


def _rope_tables(positions):
    half = ROT_DIM // 2
    inv_freq = ROPE_THETA ** (-jnp.arange(half, dtype=_F32) * 2.0 / ROT_DIM)
    in_head = jnp.arange(LANES) % HEAD_DIM
    freq = jnp.where(in_head < ROT_DIM, jnp.tile(inv_freq, LANES // half), 0.0)
    ang = positions.astype(_F32)[:, :, None] * freq
    sin = jnp.sin(ang)
    return (jnp.cos(ang), jnp.where(in_head < half, -sin, 0.0), jnp.where(in_head >= half, sin, 0.0))


def _ssm_operators(lam_re, lam_im, log_dt, b_re, b_im, c_re, c_im):
    n_g = lam_re.shape[1]
    el = SSM_CHUNK
    dt = jnp.exp(log_dt.astype(_F32))[..., None]
    lr = jnp.minimum(lam_re.astype(_F32), -1e-4)
    li = lam_im.astype(_F32)
    kk = jnp.arange(el + 1, dtype=_F32)[:, None, None, None]
    mag = jnp.exp(kk * (lr * dt))
    pr = mag * jnp.cos(kk * (li * dt))
    pi = mag * jnp.sin(kk * (li * dt))
    den = lr * lr + li * li
    nr, ni = pr[1] - 1.0, pi[1]
    coef_re = (nr * lr + ni * li) / den
    coef_im = (ni * lr - nr * li) / den
    bbr = coef_re[..., None] * b_re - coef_im[..., None] * b_im
    bbi = coef_re[..., None] * b_im + coef_im[..., None] * b_re
    abr = pr[..., None] * bbr - pi[..., None] * bbi
    abi = pr[..., None] * bbi + pi[..., None] * bbr
    kern = (jnp.einsum('eghp,kegpi->keghi', c_re, abr[:el], precision=_HI)
            - jnp.einsum('eghp,kegpi->keghi', c_im, abi[:el], precision=_HI))
    lag = np.arange(el)[:, None] - np.arange(el)[None, :]
    sel = np.stack([lag[:, :, None] == np.arange(el), -lag[:, :, None] == np.arange(el)],
                   axis=-1).astype(np.float32)
    tm = jnp.einsum('stke,keghi->gshti', sel, kern, precision=_HI).reshape(
        n_g, el * SSM_GROUP, el * SSM_GROUP)

    def pair_layout(m):
        rows = m.shape[1]
        m = m.reshape(n_g // 2, 2, rows, 4, 1, SSM_STATE)
        slot = jnp.eye(2, dtype=m.dtype).reshape(1, 2, 1, 1, 2, 1)
        return (m * slot).reshape(n_g // 2, 2, rows, 8 * SSM_STATE)

    pz = jnp.stack([jnp.flip(abr[:el, 0], 0), jnp.flip(abi[:el, 0], 0), abr[:el, 1], abi[:el, 1]],
                   axis=0)
    pz = pz.transpose(2, 1, 4, 0, 3).reshape(n_g, el * SSM_GROUP, 4, SSM_STATE)

    def c_pow(e, p_r, p_i):
        cr, ci = c_re[e][None], c_im[e][None]
        p_r, p_i = p_r[:, :, None, :], p_i[:, :, None, :]
        return cr * p_r - ci * p_i, cr * p_i + ci * p_r

    fr_, fi_ = c_pow(0, pr[1:, 0], pi[1:, 0])
    br_, bi_ = c_pow(1, jnp.flip(pr[1:, 1], 0), jnp.flip(pi[1:, 1], 0))
    py = jnp.stack([fr_, -fi_, br_, -bi_], axis=0)
    py = py.transpose(2, 1, 3, 0, 4).reshape(n_g, el * SSM_GROUP, 4, SSM_STATE)
    a16 = jnp.stack([pr[el, 0], pi[el, 0], pr[el, 1], pi[el, 1]], axis=1)
    a16 = a16.reshape(n_g // 2, 2, 4, SSM_STATE).transpose(0, 2, 1, 3).reshape(
        n_g // 2, 4, 2 * SSM_STATE)
    return tm.astype(_BF), pair_layout(pz).astype(_BF), pair_layout(py).astype(_BF), a16


def _layer(x, c, tabs, layer_idx, w_ada, b_ada, g_pre, w_in, lam_qk, g_subln,
           lam_re, lam_im, log_dt, b_re, b_im, c_re, c_im, d_skip,
           w_glu, b_glu, w_up_attn, w_up_ssm, w_out, g_final):
    bsz, s, d = x.shape
    n_c = s // SSM_CHUNK
    qk_w = ATTN_HEADS * 2 * HEAD_DIM
    v_w = ATTN_HEADS * V_HEAD_DIM
    u_w = lam_re.shape[1] * SSM_GROUP
    u_off = 2 * qk_w + 2 * v_w
    lam_init = 0.8 - 0.6 * math.exp(-0.3 * layer_idx)
    mod3 = _modulation(c, w_ada, b_ada).reshape(bsz, 1, 3 * d)
    col_scale = jnp.concatenate([jnp.full((qk_w,), HEAD_DIM ** -0.5, _F32),
                                 jnp.ones((w_in.shape[1] - qk_w,), _F32)])
    w_in_bf = (w_in * col_scale).astype(_BF)
    rows = min(ROW_BLOCK, s)
    wvt_bf = w_in[:, 2 * qk_w:2 * qk_w + v_w].T.astype(_BF)
    qt, k, vt, gates, u = _inproj(x, mod3, g_pre, w_in_bf, wvt_bf, *tabs, rows=rows,
                                  qk_w=qk_w, v_w=v_w, u_off=u_off, u_w=u_w)
    o = _attention(lam_qk, g_subln, qt, k, vt, tq=min(Q_ROWS, s), lam_init=lam_init)
    ut = u.reshape(bsz, n_c, SSM_CHUNK, u_w).transpose(0, 2, 3, 1)
    yt = _ssm(ut, *_ssm_operators(lam_re, lam_im, log_dt, b_re, b_im, c_re, c_im))
    y = yt.transpose(0, 3, 1, 2).reshape(bsz, s, u_w)
    return _post(o, gates, u, y, x, mod3, d_skip, b_glu, g_final,
                 w_up_attn.astype(_BF), w_glu.astype(_BF), w_up_ssm.astype(_BF), w_out.astype(_BF),
                 rows=rows)


def kernel(x, c, positions, w_ada, b_ada, g_pre, w_in, lam_qk, g_subln, ssm_lam_re, ssm_lam_im,
           ssm_log_dt, ssm_b_re, ssm_b_im, ssm_c_re, ssm_c_im, ssm_d, w_glu, b_glu, w_up_attn,
           w_up_ssm, w_out, g_final):
    depth = w_ada.shape[0]
    assert depth == 1, "the final RMSNorm is fused into the single layer's epilogue"
    assert x.shape[1] % (SSM_CHUNK * 8) == 0
    return _layer(x, c, _rope_tables(positions), 0, w_ada[0], b_ada[0], g_pre[0], w_in[0],
                  lam_qk[0], g_subln[0], ssm_lam_re[0], ssm_lam_im[0], ssm_log_dt[0], ssm_b_re[0],
                  ssm_b_im[0], ssm_c_re[0], ssm_c_im[0], ssm_d[0], w_glu[0], b_glu[0],
                  w_up_attn[0], w_up_ssm[0], w_out[0], g_final)
```

```python
import functools
import math

import jax
import jax.numpy as jnp
import numpy as np
from jax import lax
from jax.experimental import pallas as pl
from jax.experimental.pallas import tpu as pltpu

ATTN_HEADS = 4
HEAD_DIM = 64
V_HEAD_DIM = 2 * HEAD_DIM
ROT_DIM = HEAD_DIM // 4
ROPE_THETA = 500000.0
SSM_GROUP = 16
SSM_STATE = 64
SSM_CHUNK = 16
EPS = 1e-6
LANES = 128
VMEM_LIMIT_BYTES = 56 * 1024 * 1024
ROW_BLOCK = 512
Q_ROWS = 512
KEY_BLOCK = 256
VT_PAD = 16

_HI = lax.Precision.HIGHEST
_BF = jnp.bfloat16
_F32 = jnp.float32


def _nt_dot(a, b):
    return lax.dot_general(a, b, (((1,), (1,)), ((), ())), preferred_element_type=_F32)


def _dot(a, b):
    return jnp.dot(a, b, preferred_element_type=_F32)


def _mod_kernel(c_ref, w_ref, b_ref, o_ref):
    c = c_ref[...]
    o_ref[...] = jnp.dot(jax.nn.silu(c), w_ref[...], precision=_HI,
                         preferred_element_type=_F32) + b_ref[...]


def _modulation(c, w_ada, b_ada):
    bsz, d = c.shape
    n = w_ada.shape[1]
    tn = d
    return pl.pallas_call(
        _mod_kernel,
        grid=(n // tn,),
        in_specs=[pl.BlockSpec((bsz, d), lambda j: (0, 0)),
                  pl.BlockSpec((d, tn), lambda j: (0, j)),
                  pl.BlockSpec((1, tn), lambda j: (0, j))],
        out_specs=pl.BlockSpec((bsz, tn), lambda j: (0, j)),
        out_shape=jax.ShapeDtypeStruct((bsz, n), _F32),
        compiler_params=pltpu.CompilerParams(vmem_limit_bytes=VMEM_LIMIT_BYTES),
        name="adaln_mod",
    )(c, w_ada, b_ada.reshape(1, n))


def _inproj_kernel(x_ref, mod_ref, gpre_ref, w_ref, wvt_ref, rc_ref, ra_ref, rb_ref,
                   qt_ref, k_ref, vt_ref, gates_ref, u_ref, *, d, qk_w, v_w, u_off, u_w):
    shift = mod_ref[0, :, 0:d]
    scale = mod_ref[0, :, d:2 * d]
    x = x_ref[0]
    inv = lax.rsqrt(jnp.mean(x * x, axis=-1, keepdims=True) + EPS)
    h = ((x * inv) * (gpre_ref[...] * (1.0 + scale)) + shift).astype(_BF)

    def rope(t):
        reps = qk_w // LANES
        rc = jnp.concatenate([rc_ref[0]] * reps, axis=1)
        ra = jnp.concatenate([ra_ref[0]] * reps, axis=1)
        rb = jnp.concatenate([rb_ref[0]] * reps, axis=1)
        half = ROT_DIM // 2
        up = pltpu.roll(t, qk_w - half, 1)
        dn = pltpu.roll(t, half, 1)
        return t * rc + up * ra + dn * rb

    qt = rope(_dot(h, w_ref[:, 0:qk_w])).T.astype(_BF)
    hw = 2 * HEAD_DIM
    for hd in range(qk_w // hw):
        qt_ref[0, hd] = qt[hd * hw:(hd + 1) * hw, :]
    k_ref[0] = rope(_dot(h, w_ref[:, qk_w:2 * qk_w])).astype(_BF)
    v_off = 2 * qk_w
    vt = _nt_dot(wvt_ref[...], h).astype(_BF)
    rows = h.shape[0]
    pad = (lax.broadcasted_iota(jnp.int32, (VT_PAD, KEY_BLOCK), 0) == 0).astype(_BF)
    for hd in range(v_w // V_HEAD_DIM):
        for c in range(rows // KEY_BLOCK):
            vt_ref[0, hd, c, 0:V_HEAD_DIM, :] = vt[hd * V_HEAD_DIM:(hd + 1) * V_HEAD_DIM,
                                                   c * KEY_BLOCK:(c + 1) * KEY_BLOCK]
            vt_ref[0, hd, c, V_HEAD_DIM:V_HEAD_DIM + VT_PAD, :] = pad
    u_ref[0] = _dot(h, w_ref[:, u_off:u_off + u_w])
    za_off = v_off + v_w
    gates_ref[0, :, 0:v_w] = _dot(h, w_ref[:, za_off:za_off + v_w])
    zs_off = u_off + u_w
    rest = w_ref.shape[1] - zs_off
    piece = 512
    for p in range(rest // piece):
        gates_ref[0, :, v_w + p * piece:v_w + (p + 1) * piece] = _dot(
            h, w_ref[:, zs_off + p * piece:zs_off + (p + 1) * piece])


def _inproj(x, mod3, g_pre, w_in_bf, wvt_bf, rope_c, rope_a, rope_b, *, rows, qk_w, v_w, u_off, u_w):
    bsz, s, d = x.shape
    n_in = w_in_bf.shape[1]
    gates_w = n_in - 2 * qk_w - v_w - u_w
    n_heads = v_w // V_HEAD_DIM
    vt_rows = V_HEAD_DIM + VT_PAD
    kern = functools.partial(_inproj_kernel, d=d, qk_w=qk_w, v_w=v_w, u_off=u_off, u_w=u_w)
    seq_spec = lambda w: pl.BlockSpec((1, rows, w), lambda b, t: (b, t, 0))
    return pl.pallas_call(
        kern,
        grid=(bsz, s // rows),
        in_specs=[seq_spec(d),
                  pl.BlockSpec((1, 1, 3 * d), lambda b, t: (b, 0, 0)),
                  pl.BlockSpec((1, d), lambda b, t: (0, 0)),
                  pl.BlockSpec((d, n_in), lambda b, t: (0, 0)),
                  pl.BlockSpec((v_w, d), lambda b, t: (0, 0)),
                  seq_spec(LANES), seq_spec(LANES), seq_spec(LANES)],
        out_specs=[pl.BlockSpec((1, n_heads, 2 * HEAD_DIM, rows), lambda b, t: (b, 0, 0, t)),
                   seq_spec(qk_w),
                   pl.BlockSpec((1, n_heads, rows // KEY_BLOCK, vt_rows, KEY_BLOCK),
                                lambda b, t: (b, 0, t, 0, 0)),
                   seq_spec(gates_w), seq_spec(u_w)],
        out_shape=[jax.ShapeDtypeStruct((bsz, n_heads, 2 * HEAD_DIM, s), _BF),
                   jax.ShapeDtypeStruct((bsz, s, qk_w), _BF),
                   jax.ShapeDtypeStruct((bsz, n_heads, s // KEY_BLOCK, vt_rows, KEY_BLOCK), _BF),
                   jax.ShapeDtypeStruct((bsz, s, gates_w), _F32),
                   jax.ShapeDtypeStruct((bsz, s, u_w), _F32)],
        compiler_params=pltpu.CompilerParams(
            dimension_semantics=("arbitrary", "arbitrary"), vmem_limit_bytes=VMEM_LIMIT_BYTES),
        name="norm_inproj",
    )(x, mod3, g_pre.reshape(1, d), w_in_bf, wvt_bf, rope_c, rope_a, rope_b)


def _attn_kernel(lamqk_ref, gsub_ref, qt_ref, k_ref, vt_ref, o_ref,
                 s1_ref, s2_ref, e1_ref, e2_ref, *, lam_init):
    n_kb, tk, tq = s1_ref.shape
    lf = lamqk_ref[...]
    lam = (jnp.exp(jnp.sum(lf[0:1, :] * lf[1:2, :], axis=-1, keepdims=True))
           - jnp.exp(jnp.sum(lf[2:3, :] * lf[3:4, :], axis=-1, keepdims=True)) + lam_init)
    qt = qt_ref[0, 0]
    first = lax.broadcasted_iota(jnp.int32, (2 * HEAD_DIM, 1), 0) < HEAD_DIM
    zero = jnp.zeros_like(qt)
    q1t = jnp.where(first, qt, zero)
    q2t = jnp.where(first, zero, qt)

    def scores(kb, qmt, s_ref, m):
        st = _dot(k_ref[0, kb * tk:(kb + 1) * tk, :], qmt)
        s_ref[kb] = st
        return jnp.maximum(m, jnp.max(st.reshape(tk // 8, 8, tq), axis=0))

    def probs(kb, s_ref, m_row, e_ref):
        e_ref[kb] = jnp.exp(s_ref[kb] - m_row).astype(_BF)

    def values(kb, e_ref):
        return _dot(vt_ref[0, 0, kb], e_ref[kb])

    m1 = m2 = jnp.full((8, tq), -jnp.inf, _F32)
    for kb in range(n_kb):
        m1 = scores(kb, q1t, s1_ref, m1)
    m1 = jnp.max(m1, axis=0, keepdims=True)
    for kb in range(n_kb):
        m2 = scores(kb, q2t, s2_ref, m2)
        probs(kb, s1_ref, m1, e1_ref)
    m2 = jnp.max(m2, axis=0, keepdims=True)
    ev1 = ev2 = None
    for kb in range(n_kb):
        probs(kb, s2_ref, m2, e2_ref)
        ev1 = values(kb, e1_ref) if ev1 is None else ev1 + values(kb, e1_ref)
    for kb in range(n_kb):
        ev2 = values(kb, e2_ref) if ev2 is None else ev2 + values(kb, e2_ref)

    def normalised(ev):
        return ev[0:V_HEAD_DIM, :] * (1.0 / ev[V_HEAD_DIM:V_HEAD_DIM + 1, :])

    ot = normalised(ev1) - lam * normalised(ev2)
    inv = lax.rsqrt(jnp.mean(ot * ot, axis=0, keepdims=True) + EPS)
    o_ref[0] = (((ot * inv) * gsub_ref[...]) * (1.0 - lam_init)).T


def _attention(lam_qk, g_subln, qt, k, vt, *, tq, lam_init):
    bsz, n_heads, hw, s = qt.shape
    _, _, n_kb, vt_rows, tk = vt.shape
    kern = functools.partial(_attn_kernel, lam_init=lam_init)
    return pl.pallas_call(
        kern,
        grid=(bsz, n_heads, s // tq),
        in_specs=[pl.BlockSpec(lam_qk.shape, lambda b, h, i: (0, 0)),
                  pl.BlockSpec((V_HEAD_DIM, 1), lambda b, h, i: (0, 0)),
                  pl.BlockSpec((1, 1, hw, tq), lambda b, h, i: (b, h, 0, i)),
                  pl.BlockSpec((1, s, hw), lambda b, h, i: (b, 0, h)),
                  pl.BlockSpec((1, 1, n_kb, vt_rows, tk), lambda b, h, i: (b, h, 0, 0, 0))],
        out_specs=pl.BlockSpec((1, tq, V_HEAD_DIM), lambda b, h, i: (b, i, h)),
        out_shape=jax.ShapeDtypeStruct((bsz, s, n_heads * V_HEAD_DIM), _F32),
        scratch_shapes=[pltpu.VMEM((n_kb, tk, tq), _F32), pltpu.VMEM((n_kb, tk, tq), _F32),
                        pltpu.VMEM((n_kb, tk, tq), _BF), pltpu.VMEM((n_kb, tk, tq), _BF)],
        compiler_params=pltpu.CompilerParams(
            dimension_semantics=("arbitrary", "arbitrary", "arbitrary"),
            vmem_limit_bytes=VMEM_LIMIT_BYTES),
        name="diff_attention",
    )(lam_qk, g_subln.reshape(V_HEAD_DIM, 1), qt, k, vt)


def _ssm_kernel(ut_ref, kst_ref, pz_ref, py_ref, a_ref, dsk_ref, yt_ref, at_ref, z_ref, xp_ref,
                *, bsz, n_c):
    cw = SSM_CHUNK * SSM_GROUP
    sw = 2 * SSM_STATE
    n_cols = n_c * bsz

    def group_rows(ref, gl):
        return ref[:, gl * SSM_GROUP:(gl + 1) * SSM_GROUP, :].reshape(cw, n_cols)

    for gl in range(2):
        at_ref[gl] = group_rows(ut_ref, gl).T.astype(_BF)
    z = _dot(at_ref[0], pz_ref[0, 0]) + _dot(at_ref[1], pz_ref[0, 1])
    for comp in range(4):
        z_ref[comp] = z[:, comp * sw:(comp + 1) * sw]

    a_fr = a_ref[0, 0:1, :]
    a_fi = a_ref[0, 1:2, :]
    a_br = a_ref[0, 2:3, :]
    a_bi = a_ref[0, 3:4, :]

    def step(c, carry):
        fr, fi, br, bi = carry
        rows_f = pl.ds(pl.multiple_of(c * bsz, bsz), bsz)
        rows_b = pl.ds(pl.multiple_of((n_c - 1 - c) * bsz, bsz), bsz)
        xp_ref[0, rows_f, :] = fr
        xp_ref[1, rows_f, :] = fi
        xp_ref[2, rows_b, :] = br
        xp_ref[3, rows_b, :] = bi
        zfr = z_ref[0, rows_f, :]
        zfi = z_ref[1, rows_f, :]
        zbr = z_ref[2, rows_b, :]
        zbi = z_ref[3, rows_b, :]
        return (a_fr * fr - a_fi * fi + zfr, a_fr * fi + a_fi * fr + zfi,
                a_br * br - a_bi * bi + zbr, a_br * bi + a_bi * br + zbi)

    zero = jnp.zeros((bsz, sw), _F32)
    lax.fori_loop(0, n_c, step, (zero, zero, zero, zero), unroll=4)
    xp = jnp.concatenate([xp_ref[comp] for comp in range(4)], axis=1).astype(_BF)

    spread = (lax.broadcasted_iota(jnp.int32, (SSM_GROUP, cw), 1) % SSM_GROUP
              == lax.broadcasted_iota(jnp.int32, (SSM_GROUP, cw), 0)).astype(_BF)
    lag = (lax.broadcasted_iota(jnp.int32, (cw, cw), 0) // SSM_GROUP
           - lax.broadcasted_iota(jnp.int32, (cw, cw), 1) // SSM_GROUP)
    for gl in range(2):
        ktf = _dot(kst_ref[gl, 0].T.astype(_BF), spread)
        ktb = _dot(kst_ref[gl, 1].T.astype(_BF), spread)

        def lag_rows(kt, k):
            blk = kt[k * SSM_GROUP:(k + 1) * SSM_GROUP, :]
            return jnp.broadcast_to(blk[None], (SSM_CHUNK, SSM_GROUP, cw)).reshape(cw, cw)

        tm = jnp.where(lag == 0, lag_rows(ktf, 0) + lag_rows(ktb, 0), 0.0)
        for k in range(1, SSM_CHUNK):
            tm = jnp.where(lag == k, lag_rows(ktf, k), tm)
            tm = jnp.where(lag == -k, lag_rows(ktb, k), tm)
        u_g = group_rows(ut_ref, gl)
        yt = _dot(tm.astype(_BF), u_g.astype(_BF)) + _nt_dot(py_ref[0, gl], xp)
        yt = yt + dsk_ref[0, gl] * u_g
        yt_ref[:, gl * SSM_GROUP:(gl + 1) * SSM_GROUP, :] = yt.reshape(SSM_CHUNK, SSM_GROUP, n_cols)


def _ssm(ut, kst, pz, py, a16, dsk, *, bsz):
    _, u_w, n_cols = ut.shape
    n_c = n_cols // bsz
    n_pairs = u_w // (2 * SSM_GROUP)
    cw = SSM_CHUNK * SSM_GROUP
    st_w = 8 * SSM_STATE
    kern = functools.partial(_ssm_kernel, bsz=bsz, n_c=n_c)
    blk = pl.BlockSpec((SSM_CHUNK, 2 * SSM_GROUP, n_cols), lambda g: (0, g, 0))
    return pl.pallas_call(
        kern,
        grid=(n_pairs,),
        in_specs=[blk,
                  pl.BlockSpec((2, 2, SSM_GROUP, cw), lambda g: (g, 0, 0, 0)),
                  pl.BlockSpec((1, 2, cw, st_w), lambda g: (g, 0, 0, 0)),
                  pl.BlockSpec((1, 2, cw, st_w), lambda g: (g, 0, 0, 0)),
                  pl.BlockSpec((1, 4, 2 * SSM_STATE), lambda g: (g, 0, 0)),
                  pl.BlockSpec((1, 2, cw, 1), lambda g: (g, 0, 0, 0))],
        out_specs=blk,
        out_shape=jax.ShapeDtypeStruct(ut.shape, _F32),
        scratch_shapes=[pltpu.VMEM((2, n_cols, cw), _BF),
                        pltpu.VMEM((4, n_cols, 2 * SSM_STATE), _F32),
                        pltpu.VMEM((4, n_cols, 2 * SSM_STATE), _F32)],
        compiler_params=pltpu.CompilerParams(
            dimension_semantics=("arbitrary",), vmem_limit_bytes=VMEM_LIMIT_BYTES),
        name="s5_chunked",
    )(ut, kst, pz, py, a16, dsk)


def _post_kernel(o_ref, gates_ref, y_ref, x_ref, mod_ref, bglu_ref, gfin_ref,
                 wua_ref, wglu_ref, wus_ref, wout_ref, out_ref, *, d, aw, sw):
    y = jax.nn.gelu(y_ref[0])
    y = y * jax.nn.sigmoid(_dot(y.astype(_BF), wglu_ref[...]) + bglu_ref[...])
    s_br = _dot((y * jax.nn.silu(gates_ref[0, :, aw:aw + sw])).astype(_BF), wus_ref[...])
    a_br = _dot((o_ref[0] * jax.nn.silu(gates_ref[0, :, 0:aw])).astype(_BF), wua_ref[...])
    g_off = aw + sw
    merged = (jax.nn.sigmoid(gates_ref[0, :, g_off:g_off + d]) * a_br
              + jax.nn.sigmoid(gates_ref[0, :, g_off + d:g_off + 2 * d]) * s_br)
    r = _dot(merged.astype(_BF), wout_ref[...])
    xo = x_ref[0] + mod_ref[0, :, 2 * d:3 * d] * r
    inv = lax.rsqrt(jnp.mean(xo * xo, axis=-1, keepdims=True) + EPS)
    out_ref[0] = (xo * inv) * gfin_ref[...]


def _post(o, gates, y, x, mod3, b_glu, g_final, wua, wglu, wus, wout, *, rows):
    bsz, s, d = x.shape
    aw = o.shape[2]
    sw = y.shape[2]
    kern = functools.partial(_post_kernel, d=d, aw=aw, sw=sw)
    full = lambda a: pl.BlockSpec(a.shape, lambda b, t: (0,) * a.ndim)
    seq_spec = lambda w: pl.BlockSpec((1, rows, w), lambda b, t: (b, t, 0))
    bgl = b_glu.reshape(1, sw)
    gfi = g_final.reshape(1, d)
    return pl.pallas_call(
        kern,
        grid=(bsz, s // rows),
        in_specs=[seq_spec(aw), seq_spec(gates.shape[2]), seq_spec(sw), seq_spec(d),
                  pl.BlockSpec((1, 1, 3 * d), lambda b, t: (b, 0, 0)),
                  full(bgl), full(gfi), full(wua), full(wglu), full(wus), full(wout)],
        out_specs=seq_spec(d),
        out_shape=jax.ShapeDtypeStruct(x.shape, _F32),
        compiler_params=pltpu.CompilerParams(
            dimension_semantics=("arbitrary", "arbitrary"), vmem_limit_bytes=VMEM_LIMIT_BYTES),
        name="merge_out",
    )(o, gates, y, x, mod3, bgl, gfi, wua, wglu, wus, wout)


def _rope_tables(positions):
    half = ROT_DIM // 2
    inv_freq = ROPE_THETA ** (-jnp.arange(half, dtype=_F32) * 2.0 / ROT_DIM)
    in_head = jnp.arange(LANES) % HEAD_DIM
    freq = jnp.where(in_head < ROT_DIM, jnp.tile(inv_freq, LANES // half), 0.0)
    ang = positions.astype(_F32)[:, :, None] * freq
    sin = jnp.sin(ang)
    return (jnp.cos(ang), jnp.where(in_head < half, -sin, 0.0), jnp.where(in_head >= half, sin, 0.0))


def _ssm_operators(lam_re, lam_im, log_dt, b_re, b_im, c_re, c_im):
    n_g = lam_re.shape[1]
    el = SSM_CHUNK
    dt = jnp.exp(log_dt.astype(_F32))[..., None]
    lr = jnp.minimum(lam_re.astype(_F32), -1e-4)
    li = lam_im.astype(_F32)
    kk = jnp.arange(el + 1, dtype=_F32)[:, None, None, None]
    mag = jnp.exp(kk * (lr * dt))
    pr = mag * jnp.cos(kk * (li * dt))
    pi = mag * jnp.sin(kk * (li * dt))
    den = lr * lr + li * li
    nr, ni = pr[1] - 1.0, pi[1]
    coef_re = ((nr * lr + ni * li) / den)[:, :, None, :]
    coef_im = ((ni * lr - nr * li) / den)[:, :, None, :]
    bt_re = jnp.swapaxes(b_re, -1, -2).astype(_F32)
    bt_im = jnp.swapaxes(b_im, -1, -2).astype(_F32)
    bbr = coef_re * bt_re - coef_im * bt_im
    bbi = coef_re * bt_im + coef_im * bt_re
    p_r, p_i = pr[:, :, :, None, :], pi[:, :, :, None, :]
    abr = p_r * bbr - p_i * bbi
    abi = p_r * bbi + p_i * bbr
    kern = (jnp.einsum('eghp,kegip->geikh', c_re, abr[:el], precision=_HI)
            - jnp.einsum('eghp,kegip->geikh', c_im, abi[:el], precision=_HI))
    kst = kern.reshape(n_g, 2, SSM_GROUP, el * SSM_GROUP)

    def pair_layout(comps):
        m = jnp.stack(comps, axis=2)
        m = m.reshape(n_g // 2, 2, m.shape[1], 4, SSM_STATE)
        left = jnp.pad(m[:, 0], ((0, 0), (0, 0), (0, 0), (0, SSM_STATE)))
        right = jnp.pad(m[:, 1], ((0, 0), (0, 0), (0, 0), (SSM_STATE, 0)))
        return jnp.stack([left, right], axis=1).reshape(n_g // 2, 2, -1, 8 * SSM_STATE).astype(_BF)

    def rows_of(m):
        return jnp.swapaxes(m, 0, 1).reshape(n_g, el * SSM_GROUP, SSM_STATE)

    pz = pair_layout([rows_of(jnp.flip(abr[:el, 0], 0)), rows_of(jnp.flip(abi[:el, 0], 0)),
                      rows_of(abr[:el, 1]), rows_of(abi[:el, 1])])

    def c_pow(e, q_r, q_i):
        cr, ci = c_re[e][None], c_im[e][None]
        q_r, q_i = q_r[:, :, None, :], q_i[:, :, None, :]
        return rows_of(cr * q_r - ci * q_i), rows_of(cr * q_i + ci * q_r)

    fr_, fi_ = c_pow(0, pr[1:, 0], pi[1:, 0])
    br_, bi_ = c_pow(1, jnp.flip(pr[1:, 1], 0), jnp.flip(pi[1:, 1], 0))
    py = pair_layout([fr_, -fi_, br_, -bi_])
    a16 = jnp.stack([pr[el, 0], pi[el, 0], pr[el, 1], pi[el, 1]], axis=1)
    a16 = a16.reshape(n_g // 2, 2, 4, SSM_STATE).transpose(0, 2, 1, 3).reshape(
        n_g // 2, 4, 2 * SSM_STATE)
    return kst, pz, py, a16


def _layer(x, c, tabs, layer_idx, w_ada, b_ada, g_pre, w_in, lam_qk, g_subln,
           lam_re, lam_im, log_dt, b_re, b_im, c_re, c_im, d_skip,
           w_glu, b_glu, w_up_attn, w_up_ssm, w_out, g_final):
    bsz, s, d = x.shape
    n_c = s // SSM_CHUNK
    qk_w = ATTN_HEADS * 2 * HEAD_DIM
    v_w = ATTN_HEADS * V_HEAD_DIM
    u_w = lam_re.shape[1] * SSM_GROUP
    u_off = 2 * qk_w + 2 * v_w
    lam_init = 0.8 - 0.6 * math.exp(-0.3 * layer_idx)
    mod3 = _modulation(c, w_ada, b_ada).reshape(bsz, 1, 3 * d)
    col_scale = jnp.concatenate([jnp.full((qk_w,), HEAD_DIM ** -0.5, _F32),
                                 jnp.ones((w_in.shape[1] - qk_w,), _F32)])
    w_in_bf = (w_in * col_scale).astype(_BF)
    rows = min(ROW_BLOCK, s)
    wvt_bf = w_in[:, 2 * qk_w:2 * qk_w + v_w].T.astype(_BF)
    qt, k, vt, gates, u = _inproj(x, mod3, g_pre, w_in_bf, wvt_bf, *tabs, rows=rows,
                                  qk_w=qk_w, v_w=v_w, u_off=u_off, u_w=u_w)
    o = _attention(lam_qk, g_subln, qt, k, vt, tq=min(Q_ROWS, s), lam_init=lam_init)
    ut = u.reshape(bsz, n_c, SSM_CHUNK, u_w).transpose(2, 3, 1, 0).reshape(SSM_CHUNK, u_w, n_c * bsz)
    dsk = jnp.broadcast_to(d_skip.reshape(u_w // (2 * SSM_GROUP), 2, 1, SSM_GROUP, 1),
                           (u_w // (2 * SSM_GROUP), 2, SSM_CHUNK, SSM_GROUP, 1)).reshape(
                               u_w // (2 * SSM_GROUP), 2, SSM_CHUNK * SSM_GROUP, 1)
    yt = _ssm(ut, *_ssm_operators(lam_re, lam_im, log_dt, b_re, b_im, c_re, c_im), dsk, bsz=bsz)
    y = yt.reshape(SSM_CHUNK, u_w, n_c, bsz).transpose(3, 2, 0, 1).reshape(bsz, s, u_w)
    return _post(o, gates, y, x, mod3, b_glu, g_final,
                 w_up_attn.astype(_BF), w_glu.astype(_BF), w_up_ssm.astype(_BF), w_out.astype(_BF),
                 rows=rows)


def kernel(x, c, positions, w_ada, b_ada, g_pre, w_in, lam_qk, g_subln, ssm_lam_re, ssm_lam_im,
           ssm_log_dt, ssm_b_re, ssm_b_im, ssm_c_re, ssm_c_im, ssm_d, w_glu, b_glu, w_up_attn,
           w_up_ssm, w_out, g_final):
    depth = w_ada.shape[0]
    assert depth == 1, "the final RMSNorm is fused into the single layer's epilogue"
    assert x.shape[1] % (SSM_CHUNK * 8) == 0
    return _layer(x, c, _rope_tables(positions), 0, w_ada[0], b_ada[0], g_pre[0], w_in[0],
                  lam_qk[0], g_subln[0], ssm_lam_re[0], ssm_lam_im[0], ssm_log_dt[0], ssm_b_re[0],
                  ssm_b_im[0], ssm_c_re[0], ssm_c_im[0], ssm_d[0], w_glu[0], b_glu[0],
                  w_up_attn[0], w_up_ssm[0], w_out[0], g_final)
```

```python
import functools
import math

import jax
import jax.numpy as jnp
from jax import lax
from jax.experimental import pallas as pl
from jax.experimental.pallas import tpu as pltpu

ATTN_HEADS = 4
HEAD_DIM = 64
V_HEAD_DIM = 2 * HEAD_DIM
ROT_DIM = HEAD_DIM // 4
ROPE_THETA = 500000.0
SSM_GROUP = 16
SSM_STATE = 64
SSM_CHUNK = 16
EPS = 1e-6
LANES = 128
VMEM_LIMIT_BYTES = 56 * 1024 * 1024
ROW_BLOCK = 512
Q_ROWS = 512
KEY_BLOCK = 256
VT_PAD = 16

_HI = lax.Precision.HIGHEST
_BF = jnp.bfloat16
_F32 = jnp.float32


def _nt_dot(a, b):
    return lax.dot_general(a, b, (((1,), (1,)), ((), ())), preferred_element_type=_F32)


def _dot(a, b):
    return jnp.dot(a, b, preferred_element_type=_F32)


def _mod_kernel(c_ref, w_ref, b_ref, o_ref):
    c = c_ref[...]
    o_ref[...] = jnp.dot(jax.nn.silu(c), w_ref[...], precision=_HI,
                         preferred_element_type=_F32) + b_ref[...]


def _modulation(c, w_ada, b_ada):
    bsz, d = c.shape
    n = w_ada.shape[1]
    tn = d
    return pl.pallas_call(
        _mod_kernel,
        grid=(n // tn,),
        in_specs=[pl.BlockSpec((bsz, d), lambda j: (0, 0)),
                  pl.BlockSpec((d, tn), lambda j: (0, j)),
                  pl.BlockSpec((1, tn), lambda j: (0, j))],
        out_specs=pl.BlockSpec((bsz, tn), lambda j: (0, j)),
        out_shape=jax.ShapeDtypeStruct((bsz, n), _F32),
        compiler_params=pltpu.CompilerParams(vmem_limit_bytes=VMEM_LIMIT_BYTES),
        name="adaln_mod",
    )(c, w_ada, b_ada.reshape(1, n))


def _inproj_kernel(x_ref, mod_ref, gpre_ref, w_ref, wvt_ref, rope_ref,
                   qt_ref, k_ref, vt_ref, gates_ref, u_ref, *, d, qk_w, v_w, u_off, u_w):
    shift = mod_ref[0, :, 0:d]
    scale = mod_ref[0, :, d:2 * d]
    x = x_ref[0]
    inv = lax.rsqrt(jnp.mean(x * x, axis=-1, keepdims=True) + EPS)
    h = ((x * inv) * (gpre_ref[...] * (1.0 + scale)) + shift).astype(_BF)

    tab = rope_ref[0]
    swapped = pltpu.roll(tab, HEAD_DIM, 1)
    lane = lax.broadcasted_iota(jnp.int32, (1, LANES), 1)
    reps = qk_w // LANES
    rot_c = jnp.concatenate([jnp.where(lane < HEAD_DIM, tab, swapped)] * reps, axis=1)
    rot_s = jnp.concatenate([jnp.where(lane < HEAD_DIM, swapped, tab)] * reps, axis=1)
    half = ROT_DIM // 2
    low = lax.broadcasted_iota(jnp.int32, (1, qk_w), 1) % HEAD_DIM < half

    def rope(t):
        up = pltpu.roll(t, qk_w - half, 1)
        dn = pltpu.roll(t, half, 1)
        return t * rot_c + jnp.where(low, up, dn) * rot_s

    qt = rope(_dot(h, w_ref[:, 0:qk_w]) * HEAD_DIM ** -0.5).T.astype(_BF)
    hw = 2 * HEAD_DIM
    for hd in range(qk_w // hw):
        qt_ref[0, hd] = qt[hd * hw:(hd + 1) * hw, :]
    k_ref[0] = rope(_dot(h, w_ref[:, qk_w:2 * qk_w])).astype(_BF)
    v_off = 2 * qk_w
    vt = _nt_dot(wvt_ref[...], h).astype(_BF)
    rows = h.shape[0]
    pad = (lax.broadcasted_iota(jnp.int32, (VT_PAD, KEY_BLOCK), 0) == 0).astype(_BF)
    for hd in range(v_w // V_HEAD_DIM):
        for c in range(rows // KEY_BLOCK):
            vt_ref[0, hd, c, 0:V_HEAD_DIM, :] = vt[hd * V_HEAD_DIM:(hd + 1) * V_HEAD_DIM,
                                                   c * KEY_BLOCK:(c + 1) * KEY_BLOCK]
            vt_ref[0, hd, c, V_HEAD_DIM:V_HEAD_DIM + VT_PAD, :] = pad
    u_ref[0] = _dot(h, w_ref[:, u_off:u_off + u_w]).astype(_BF)
    za_off = v_off + v_w
    gates_ref[0, :, 0:v_w] = _dot(h, w_ref[:, za_off:za_off + v_w]).astype(_BF)
    zs_off = u_off + u_w
    rest = w_ref.shape[1] - zs_off
    piece = 512
    for p in range(rest // piece):
        gates_ref[0, :, v_w + p * piece:v_w + (p + 1) * piece] = _dot(
            h, w_ref[:, zs_off + p * piece:zs_off + (p + 1) * piece]).astype(_BF)


def _inproj(x, mod3, g_pre, w_in_bf, wvt_bf, rope_tab, *, rows, qk_w, v_w, u_off, u_w):
    bsz, s, d = x.shape
    n_in = w_in_bf.shape[1]
    gates_w = n_in - 2 * qk_w - v_w - u_w
    n_heads = v_w // V_HEAD_DIM
    vt_rows = V_HEAD_DIM + VT_PAD
    kern = functools.partial(_inproj_kernel, d=d, qk_w=qk_w, v_w=v_w, u_off=u_off, u_w=u_w)
    seq_spec = lambda w: pl.BlockSpec((1, rows, w), lambda b, t: (b, t, 0))
    return pl.pallas_call(
        kern,
        grid=(bsz, s // rows),
        in_specs=[seq_spec(d),
                  pl.BlockSpec((1, 1, 3 * d), lambda b, t: (b, 0, 0)),
                  pl.BlockSpec((1, d), lambda b, t: (0, 0)),
                  pl.BlockSpec((d, n_in), lambda b, t: (0, 0)),
                  pl.BlockSpec((v_w, d), lambda b, t: (0, 0)),
                  seq_spec(LANES)],
        out_specs=[pl.BlockSpec((1, n_heads, 2 * HEAD_DIM, rows), lambda b, t: (b, 0, 0, t)),
                   seq_spec(qk_w),
                   pl.BlockSpec((1, n_heads, rows // KEY_BLOCK, vt_rows, KEY_BLOCK),
                                lambda b, t: (b, 0, t, 0, 0)),
                   seq_spec(gates_w), seq_spec(u_w)],
        out_shape=[jax.ShapeDtypeStruct((bsz, n_heads, 2 * HEAD_DIM, s), _BF),
                   jax.ShapeDtypeStruct((bsz, s, qk_w), _BF),
                   jax.ShapeDtypeStruct((bsz, n_heads, s // KEY_BLOCK, vt_rows, KEY_BLOCK), _BF),
                   jax.ShapeDtypeStruct((bsz, s, gates_w), _BF),
                   jax.ShapeDtypeStruct((bsz, s, u_w), _BF)],
        compiler_params=pltpu.CompilerParams(
            dimension_semantics=("arbitrary", "arbitrary"), vmem_limit_bytes=VMEM_LIMIT_BYTES),
        name="norm_inproj",
    )(x, mod3, g_pre.reshape(1, d), w_in_bf, wvt_bf, rope_tab)


def _attn_kernel(lamqk_ref, gsub_ref, qt_ref, k_ref, vt_ref, o_ref,
                 s1_ref, s2_ref, e1_ref, e2_ref, *, lam_init):
    n_kb, tk, tq = s1_ref.shape
    lf = lamqk_ref[...]
    lam = (jnp.exp(jnp.sum(lf[0:1, :] * lf[1:2, :], axis=-1, keepdims=True))
           - jnp.exp(jnp.sum(lf[2:3, :] * lf[3:4, :], axis=-1, keepdims=True)) + lam_init)
    qt = qt_ref[0, 0]
    first = lax.broadcasted_iota(jnp.int32, (2 * HEAD_DIM, 1), 0) < HEAD_DIM
    zero = jnp.zeros_like(qt)
    q1t = jnp.where(first, qt, zero)
    q2t = jnp.where(first, zero, qt)

    def scores(kb, qmt, s_ref, m):
        st = _dot(k_ref[0, kb * tk:(kb + 1) * tk, :], qmt)
        s_ref[kb] = st
        return jnp.maximum(m, jnp.max(st.reshape(tk // 8, 8, tq), axis=0))

    def probs(kb, s_ref, m_row, e_ref):
        e_ref[kb] = jnp.exp(s_ref[kb] - m_row).astype(_BF)

    def values(kb, e_ref):
        return _dot(vt_ref[0, 0, kb], e_ref[kb])

    m1 = m2 = jnp.full((8, tq), -jnp.inf, _F32)
    for kb in range(n_kb):
        m1 = scores(kb, q1t, s1_ref, m1)
    m1 = jnp.max(m1, axis=0, keepdims=True)
    for kb in range(n_kb):
        m2 = scores(kb, q2t, s2_ref, m2)
        probs(kb, s1_ref, m1, e1_ref)
    m2 = jnp.max(m2, axis=0, keepdims=True)
    ev1 = ev2 = None
    for kb in range(n_kb):
        probs(kb, s2_ref, m2, e2_ref)
        ev1 = values(kb, e1_ref) if ev1 is None else ev1 + values(kb, e1_ref)
    for kb in range(n_kb):
        ev2 = values(kb, e2_ref) if ev2 is None else ev2 + values(kb, e2_ref)

    def normalised(ev):
        return ev[0:V_HEAD_DIM, :] * (1.0 / ev[V_HEAD_DIM:V_HEAD_DIM + 1, :])

    ot = normalised(ev1) - lam * normalised(ev2)
    inv = lax.rsqrt(jnp.mean(ot * ot, axis=0, keepdims=True) + EPS)
    o_ref[0] = (((ot * inv) * gsub_ref[...]) * (1.0 - lam_init)).T.astype(_BF)


def _attention(lam_qk, g_subln, qt, k, vt, *, tq, lam_init):
    bsz, n_heads, hw, s = qt.shape
    _, _, n_kb, vt_rows, tk = vt.shape
    kern = functools.partial(_attn_kernel, lam_init=lam_init)
    return pl.pallas_call(
        kern,
        grid=(bsz, n_heads, s // tq),
        in_specs=[pl.BlockSpec(lam_qk.shape, lambda b, h, i: (0, 0)),
                  pl.BlockSpec((V_HEAD_DIM, 1), lambda b, h, i: (0, 0)),
                  pl.BlockSpec((1, 1, hw, tq), lambda b, h, i: (b, h, 0, i)),
                  pl.BlockSpec((1, s, hw), lambda b, h, i: (b, 0, h)),
                  pl.BlockSpec((1, 1, n_kb, vt_rows, tk), lambda b, h, i: (b, h, 0, 0, 0))],
        out_specs=pl.BlockSpec((1, tq, V_HEAD_DIM), lambda b, h, i: (b, i, h)),
        out_shape=jax.ShapeDtypeStruct((bsz, s, n_heads * V_HEAD_DIM), _BF),
        scratch_shapes=[pltpu.VMEM((n_kb, tk, tq), _F32), pltpu.VMEM((n_kb, tk, tq), _F32),
                        pltpu.VMEM((n_kb, tk, tq), _BF), pltpu.VMEM((n_kb, tk, tq), _BF)],
        compiler_params=pltpu.CompilerParams(
            dimension_semantics=("arbitrary", "arbitrary", "arbitrary"),
            vmem_limit_bytes=VMEM_LIMIT_BYTES),
        name="diff_attention",
    )(lam_qk, g_subln.reshape(V_HEAD_DIM, 1), qt, k, vt)


def _ssm_kernel(ut_ref, lam_ref, bt_ref, ct_ref, dsk_ref, yt_ref, at_ref, z_ref, xp_ref,
                *, bsz, n_c):
    el = SSM_CHUNK
    cw = el * SSM_GROUP
    sw = 2 * SSM_STATE
    n_cols = n_c * bsz
    in_group = [lax.broadcasted_iota(jnp.int32, (1, sw), 1) // SSM_STATE == gl for gl in range(2)]

    def group_rows(ref, gl):
        return ref[:, gl * SSM_GROUP:(gl + 1) * SSM_GROUP, :].reshape(cw, n_cols)

    def per_row(rows, cols):
        return (jnp.broadcast_to(rows[:, None, :], (el, SSM_GROUP, sw)),
                jnp.broadcast_to(cols[None, :, :], (el, SSM_GROUP, sw)))

    def cmul_rows(p_r, p_i, m_r, m_i):
        pr3, mr3 = per_row(p_r, m_r)
        pi3, mi3 = per_row(p_i, m_i)
        return (pr3 * mr3 - pi3 * mi3).reshape(cw, sw), (pr3 * mi3 + pi3 * mr3).reshape(cw, sw)

    idx = lax.broadcasted_iota(jnp.int32, (el, 1), 0).astype(_F32)
    ops = []
    for e in range(2):
        lr = jnp.minimum(lam_ref[0, e, 0:1, :], -1e-4)
        li = lam_ref[0, e, 1:2, :]
        dt = jnp.exp(lam_ref[0, e, 2:3, :])

        def power(k, lr=lr, li=li, dt=dt):
            mag = jnp.exp(k * (lr * dt))
            return mag * jnp.cos(k * (li * dt)), mag * jnp.sin(k * (li * dt))

        one_r, one_i = power(1.0)
        den = lr * lr + li * li
        nr, ni = one_r - 1.0, one_i
        coef_r = (nr * lr + ni * li) / den
        coef_i = (ni * lr - nr * li) / den
        b_r, b_i = bt_ref[0, e, 0], bt_ref[0, e, 1]
        bb_r = coef_r * b_r - coef_i * b_i
        bb_i = coef_r * b_i + coef_i * b_r
        c_r, c_i = ct_ref[0, e, 0], ct_ref[0, e, 1]
        ops.append(dict(power=power, bb=(bb_r, bb_i), c=(c_r, c_i)))
    fwd, bwd = ops

    zf_r, zf_i = cmul_rows(*fwd["power"](el - 1.0 - idx), *fwd["bb"])
    zb_r, zb_i = cmul_rows(*bwd["power"](idx), *bwd["bb"])
    yf_r, yf_i = cmul_rows(*fwd["power"](idx + 1.0), *fwd["c"])
    yb_r, yb_i = cmul_rows(*bwd["power"](el - idx), *bwd["c"])
    kf_r, kf_i = cmul_rows(*fwd["power"](idx), *fwd["c"])
    kb_r, kb_i = cmul_rows(*bwd["power"](idx), *bwd["c"])

    def pair_slot(parts, gl):
        return jnp.concatenate([jnp.where(in_group[gl], p, 0.0) for p in parts], axis=1).astype(_BF)

    for gl in range(2):
        at_ref[gl] = group_rows(ut_ref, gl).T
    z = (_dot(at_ref[0], pair_slot([zf_r, zf_i, zb_r, zb_i], 0))
         + _dot(at_ref[1], pair_slot([zf_r, zf_i, zb_r, zb_i], 1)))
    for comp in range(4):
        z_ref[comp] = z[:, comp * sw:(comp + 1) * sw]

    a_fr, a_fi = fwd["power"](float(el))
    a_br, a_bi = bwd["power"](float(el))

    def step(c, carry):
        fr, fi, br, bi = carry
        rows_f = pl.ds(pl.multiple_of(c * bsz, bsz), bsz)
        rows_b = pl.ds(pl.multiple_of((n_c - 1 - c) * bsz, bsz), bsz)
        xp_ref[0, rows_f, :] = fr
        xp_ref[1, rows_f, :] = fi
        xp_ref[2, rows_b, :] = br
        xp_ref[3, rows_b, :] = bi
        zfr = z_ref[0, rows_f, :]
        zfi = z_ref[1, rows_f, :]
        zbr = z_ref[2, rows_b, :]
        zbi = z_ref[3, rows_b, :]
        return (a_fr * fr - a_fi * fi + zfr, a_fr * fi + a_fi * fr + zfi,
                a_br * br - a_bi * bi + zbr, a_br * bi + a_bi * br + zbi)

    zero = jnp.zeros((bsz, sw), _F32)
    lax.fori_loop(0, n_c, step, (zero, zero, zero, zero), unroll=4)
    xp = jnp.concatenate([xp_ref[comp] for comp in range(4)], axis=1).astype(_BF)

    spread = (lax.broadcasted_iota(jnp.int32, (SSM_GROUP, cw), 1) % SSM_GROUP
              == lax.broadcasted_iota(jnp.int32, (SSM_GROUP, cw), 0)).astype(_BF)
    lag = (lax.broadcasted_iota(jnp.int32, (cw, cw), 0) // SSM_GROUP
           - lax.broadcasted_iota(jnp.int32, (cw, cw), 1) // SSM_GROUP)

    def lag_kernels(k_r, k_i, bb, gl):
        k_r, k_i = jnp.where(in_group[gl], k_r, 0.0), jnp.where(in_group[gl], k_i, 0.0)
        hi_dot = functools.partial(lax.dot_general, dimension_numbers=(((1,), (1,)), ((), ())),
                                   precision=_HI, preferred_element_type=_F32)
        return hi_dot(k_r, bb[0]) - hi_dot(k_i, bb[1])

    for gl in range(2):
        ktf = _dot(lag_kernels(kf_r, kf_i, fwd["bb"], gl).astype(_BF), spread)
        ktb = _dot(lag_kernels(kb_r, kb_i, bwd["bb"], gl).astype(_BF), spread)

        def lag_rows(kt, k):
            blk = kt[k * SSM_GROUP:(k + 1) * SSM_GROUP, :]
            return jnp.broadcast_to(blk[None], (el, SSM_GROUP, cw)).reshape(cw, cw)

        tm = jnp.where(lag == 0, lag_rows(ktf, 0) + lag_rows(ktb, 0), 0.0)
        for k in range(1, el):
            tm = jnp.where(lag == k, lag_rows(ktf, k), tm)
            tm = jnp.where(lag == -k, lag_rows(ktb, k), tm)
        u_g = group_rows(ut_ref, gl)
        yt = _dot(tm.astype(_BF), u_g) + _nt_dot(pair_slot([yf_r, -yf_i, yb_r, -yb_i], gl), xp)
        yt = yt + dsk_ref[0, gl] * u_g.astype(_F32)
        yt_ref[:, gl * SSM_GROUP:(gl + 1) * SSM_GROUP, :] = yt.reshape(
            el, SSM_GROUP, n_cols).astype(_BF)


def _ssm(ut, lam, bt, ct, dsk, *, bsz):
    _, u_w, n_cols = ut.shape
    n_c = n_cols // bsz
    n_pairs = u_w // (2 * SSM_GROUP)
    cw = SSM_CHUNK * SSM_GROUP
    sw = 2 * SSM_STATE
    kern = functools.partial(_ssm_kernel, bsz=bsz, n_c=n_c)
    blk = pl.BlockSpec((SSM_CHUNK, 2 * SSM_GROUP, n_cols), lambda g: (0, g, 0))
    par = pl.BlockSpec((1, 2, 2, SSM_GROUP, sw), lambda g: (g, 0, 0, 0, 0))
    return pl.pallas_call(
        kern,
        grid=(n_pairs,),
        in_specs=[blk,
                  pl.BlockSpec((1, 2, 3, sw), lambda g: (g, 0, 0, 0)),
                  par, par,
                  pl.BlockSpec((1, 2, cw, 1), lambda g: (g, 0, 0, 0))],
        out_specs=blk,
        out_shape=jax.ShapeDtypeStruct(ut.shape, _BF),
        scratch_shapes=[pltpu.VMEM((2, n_cols, cw), _BF),
                        pltpu.VMEM((4, n_cols, sw), _F32),
                        pltpu.VMEM((4, n_cols, sw), _F32)],
        compiler_params=pltpu.CompilerParams(
            dimension_semantics=("arbitrary",), vmem_limit_bytes=VMEM_LIMIT_BYTES),
        name="s5_chunked",
    )(ut, lam, bt, ct, dsk)


def _post_kernel(o_ref, gates_ref, y_ref, x_ref, mod_ref, bglu_ref, gfin_ref,
                 wua_ref, wglu_ref, wus_ref, wout_ref, out_ref, *, d, aw, sw):
    y = jax.nn.gelu(y_ref[0].astype(_F32))
    y = y * jax.nn.sigmoid(_dot(y.astype(_BF), wglu_ref[...]) + bglu_ref[...])
    z_s = gates_ref[0, :, aw:aw + sw].astype(_F32)
    s_br = _dot((y * jax.nn.silu(z_s)).astype(_BF), wus_ref[...])
    z_a = gates_ref[0, :, 0:aw].astype(_F32)
    a_br = _dot((o_ref[0].astype(_F32) * jax.nn.silu(z_a)).astype(_BF), wua_ref[...])
    g_off = aw + sw
    g_a = gates_ref[0, :, g_off:g_off + d].astype(_F32)
    g_s = gates_ref[0, :, g_off + d:g_off + 2 * d].astype(_F32)
    merged = jax.nn.sigmoid(g_a) * a_br + jax.nn.sigmoid(g_s) * s_br
    r = _dot(merged.astype(_BF), wout_ref[...])
    xo = x_ref[0] + mod_ref[0, :, 2 * d:3 * d] * r
    inv = lax.rsqrt(jnp.mean(xo * xo, axis=-1, keepdims=True) + EPS)
    out_ref[0] = (xo * inv) * gfin_ref[...]


def _post(o, gates, y, x, mod3, b_glu, g_final, wua, wglu, wus, wout, *, rows):
    bsz, s, d = x.shape
    aw = o.shape[2]
    sw = y.shape[2]
    kern = functools.partial(_post_kernel, d=d, aw=aw, sw=sw)
    full = lambda a: pl.BlockSpec(a.shape, lambda b, t: (0,) * a.ndim)
    seq_spec = lambda w: pl.BlockSpec((1, rows, w), lambda b, t: (b, t, 0))
    bgl = b_glu.reshape(1, sw)
    gfi = g_final.reshape(1, d)
    return pl.pallas_call(
        kern,
        grid=(bsz, s // rows),
        in_specs=[seq_spec(aw), seq_spec(gates.shape[2]), seq_spec(sw), seq_spec(d),
                  pl.BlockSpec((1, 1, 3 * d), lambda b, t: (b, 0, 0)),
                  full(bgl), full(gfi), full(wua), full(wglu), full(wus), full(wout)],
        out_specs=seq_spec(d),
        out_shape=jax.ShapeDtypeStruct(x.shape, _F32),
        compiler_params=pltpu.CompilerParams(
            dimension_semantics=("arbitrary", "arbitrary"), vmem_limit_bytes=VMEM_LIMIT_BYTES),
        name="merge_out",
    )(o, gates, y, x, mod3, bgl, gfi, wua, wglu, wus, wout)


def _rope_table(positions):
    half = ROT_DIM // 2
    inv_freq = ROPE_THETA ** (-jnp.arange(half, dtype=_F32) * 2.0 / ROT_DIM)
    lane = jnp.arange(LANES)
    in_head = lane % HEAD_DIM
    freq = jnp.where(in_head < ROT_DIM, jnp.tile(inv_freq, LANES // half), 0.0)
    ang = positions.astype(_F32)[:, :, None] * freq
    sin = jnp.sin(ang)
    return jnp.where(lane < HEAD_DIM, jnp.cos(ang), jnp.where(in_head < half, -sin, sin))


def _ssm_params(lam_re, lam_im, log_dt, b_re, b_im, c_re, c_im):
    n_g, n_p = lam_re.shape[1], lam_re.shape[2]
    pairs = n_g // 2

    def lanes(m):
        rows = m.shape[2]
        return m.reshape(2, pairs, 2, rows, n_p).transpose(1, 0, 3, 2, 4).reshape(pairs, 2, rows, 2 * n_p)

    lam = lanes(jnp.stack([lam_re, lam_im, jnp.broadcast_to(log_dt[..., None], lam_re.shape)],
                          axis=2).astype(_F32))
    bt = jnp.stack([lanes(jnp.swapaxes(b_re, -1, -2)), lanes(jnp.swapaxes(b_im, -1, -2))], axis=2)
    ct = jnp.stack([lanes(c_re), lanes(c_im)], axis=2)
    return lam, bt.astype(_F32), ct.astype(_F32)


def _layer(x, c, rope_tab, layer_idx, w_ada, b_ada, g_pre, w_in, lam_qk, g_subln,
           lam_re, lam_im, log_dt, b_re, b_im, c_re, c_im, d_skip,
           w_glu, b_glu, w_up_attn, w_up_ssm, w_out, g_final):
    bsz, s, d = x.shape
    n_c = s // SSM_CHUNK
    qk_w = ATTN_HEADS * 2 * HEAD_DIM
    v_w = ATTN_HEADS * V_HEAD_DIM
    u_w = lam_re.shape[1] * SSM_GROUP
    u_off = 2 * qk_w + 2 * v_w
    lam_init = 0.8 - 0.6 * math.exp(-0.3 * layer_idx)
    mod3 = _modulation(c, w_ada, b_ada).reshape(bsz, 1, 3 * d)
    w_in_bf = w_in.astype(_BF)
    rows = min(ROW_BLOCK, s)
    wvt_bf = w_in_bf[:, 2 * qk_w:2 * qk_w + v_w].T
    qt, k, vt, gates, u = _inproj(x, mod3, g_pre, w_in_bf, wvt_bf, rope_tab, rows=rows,
                                  qk_w=qk_w, v_w=v_w, u_off=u_off, u_w=u_w)
    o = _attention(lam_qk, g_subln, qt, k, vt, tq=min(Q_ROWS, s), lam_init=lam_init)
    ut = u.reshape(bsz, n_c, SSM_CHUNK, u_w).transpose(2, 3, 1, 0).reshape(SSM_CHUNK, u_w, n_c * bsz)
    dsk = jnp.broadcast_to(d_skip.reshape(u_w // (2 * SSM_GROUP), 2, 1, SSM_GROUP, 1),
                           (u_w // (2 * SSM_GROUP), 2, SSM_CHUNK, SSM_GROUP, 1)).reshape(
                               u_w // (2 * SSM_GROUP), 2, SSM_CHUNK * SSM_GROUP, 1)
    yt = _ssm(ut, *_ssm_params(lam_re, lam_im, log_dt, b_re, b_im, c_re, c_im), dsk, bsz=bsz)
    y = yt.reshape(SSM_CHUNK, u_w, n_c, bsz).transpose(3, 2, 0, 1).reshape(bsz, s, u_w)
    return _post(o, gates, y, x, mod3, b_glu, g_final,
                 w_up_attn.astype(_BF), w_glu.astype(_BF), w_up_ssm.astype(_BF), w_out.astype(_BF),
                 rows=rows)


def kernel(x, c, positions, w_ada, b_ada, g_pre, w_in, lam_qk, g_subln, ssm_lam_re, ssm_lam_im,
           ssm_log_dt, ssm_b_re, ssm_b_im, ssm_c_re, ssm_c_im, ssm_d, w_glu, b_glu, w_up_attn,
           w_up_ssm, w_out, g_final):
    depth = w_ada.shape[0]
    assert depth == 1, "the final RMSNorm is fused into the single layer's epilogue"
    assert x.shape[1] % (SSM_CHUNK * 8) == 0
    return _layer(x, c, _rope_table(positions), 0, w_ada[0], b_ada[0], g_pre[0], w_in[0],
                  lam_qk[0], g_subln[0], ssm_lam_re[0], ssm_lam_im[0], ssm_log_dt[0], ssm_b_re[0],
                  ssm_b_im[0], ssm_c_re[0], ssm_c_im[0], ssm_d[0], w_glu[0], b_glu[0],
                  w_up_attn[0], w_up_ssm[0], w_out[0], g_final)
```

```python
import functools
import math

import jax
import jax.numpy as jnp
from jax import lax
from jax.experimental import pallas as pl
from jax.experimental.pallas import tpu as pltpu

ATTN_HEADS = 4
HEAD_DIM = 64
V_HEAD_DIM = 2 * HEAD_DIM
ROT_DIM = HEAD_DIM // 4
ROPE_THETA = 500000.0
SSM_GROUP = 16
SSM_STATE = 64
SSM_CHUNK = 16
EPS = 1e-6
LANES = 128
VMEM_LIMIT_BYTES = 56 * 1024 * 1024
ROW_BLOCK = 512
Q_ROWS = 512
KEY_BLOCK = 256
VT_PAD = 16

_HI = lax.Precision.HIGHEST
_BF = jnp.bfloat16
_F32 = jnp.float32


def _nt_dot(a, b):
    return lax.dot_general(a, b, (((1,), (1,)), ((), ())), preferred_element_type=_F32)


def _dot(a, b):
    return jnp.dot(a, b, preferred_element_type=_F32)


def _mod_kernel(c_ref, w_ref, b_ref, o_ref):
    c = c_ref[...]
    o_ref[...] = jnp.dot(jax.nn.silu(c), w_ref[...], precision=_HI,
                         preferred_element_type=_F32) + b_ref[...]


def _modulation(c, w_ada, b_ada):
    bsz, d = c.shape
    n = w_ada.shape[1]
    tn = d
    return pl.pallas_call(
        _mod_kernel,
        grid=(n // tn,),
        in_specs=[pl.BlockSpec((bsz, d), lambda j: (0, 0)),
                  pl.BlockSpec((d, tn), lambda j: (0, j)),
                  pl.BlockSpec((1, tn), lambda j: (0, j))],
        out_specs=pl.BlockSpec((bsz, tn), lambda j: (0, j)),
        out_shape=jax.ShapeDtypeStruct((bsz, n), _F32),
        compiler_params=pltpu.CompilerParams(vmem_limit_bytes=VMEM_LIMIT_BYTES),
        name="adaln_mod",
    )(c, w_ada, b_ada.reshape(1, n))


def _inproj_kernel(x_ref, mod_ref, gpre_ref, w_ref, wvt_ref, pos_ref, freq_ref, spread_ref,
                   qt_ref, k_ref, vt_ref, gates_ref, u_ref, *, d, qk_w, v_w, u_off, u_w):
    shift = mod_ref[0, :, 0:d]
    scale = mod_ref[0, :, d:2 * d]
    x = x_ref[0]
    inv = lax.rsqrt(jnp.mean(x * x, axis=-1, keepdims=True) + EPS)
    h = ((x * inv) * (gpre_ref[...] * (1.0 + scale)) + shift).astype(_BF)

    rows = x.shape[0]
    half = ROT_DIM // 2
    ang = freq_ref[...] * pos_ref[0]
    cs = jnp.concatenate([jnp.cos(ang), jnp.sin(ang)], axis=0)
    tab = jnp.dot(cs.T, spread_ref[...], precision=_HI, preferred_element_type=_F32)
    lane = lax.broadcasted_iota(jnp.int32, (1, LANES), 1)
    reps = qk_w // LANES
    rot_c = jnp.concatenate([jnp.where(lane % HEAD_DIM < ROT_DIM, tab[:, 0:LANES], 1.0)] * reps,
                            axis=1)
    rot_s = jnp.concatenate([tab[:, LANES:2 * LANES]] * reps, axis=1)
    low = lax.broadcasted_iota(jnp.int32, (1, qk_w), 1) % HEAD_DIM < half

    def rope(t):
        up = pltpu.roll(t, qk_w - half, 1)
        dn = pltpu.roll(t, half, 1)
        return t * rot_c + jnp.where(low, up, dn) * rot_s

    qt = rope(_dot(h, w_ref[:, 0:qk_w]) * HEAD_DIM ** -0.5).T.astype(_BF)
    hw = 2 * HEAD_DIM
    for hd in range(qk_w // hw):
        qt_ref[0, hd] = qt[hd * hw:(hd + 1) * hw, :]
    k_ref[0] = rope(_dot(h, w_ref[:, qk_w:2 * qk_w])).astype(_BF)
    v_off = 2 * qk_w
    vt = _nt_dot(wvt_ref[...], h).astype(_BF)
    pad = (lax.broadcasted_iota(jnp.int32, (VT_PAD, KEY_BLOCK), 0) == 0).astype(_BF)
    for hd in range(v_w // V_HEAD_DIM):
        for c in range(rows // KEY_BLOCK):
            vt_ref[0, hd, c, 0:V_HEAD_DIM, :] = vt[hd * V_HEAD_DIM:(hd + 1) * V_HEAD_DIM,
                                                   c * KEY_BLOCK:(c + 1) * KEY_BLOCK]
            vt_ref[0, hd, c, V_HEAD_DIM:V_HEAD_DIM + VT_PAD, :] = pad
    u_ref[0] = _dot(h, w_ref[:, u_off:u_off + u_w]).astype(_BF)
    za_off = v_off + v_w
    gates_ref[0, :, 0:v_w] = _dot(h, w_ref[:, za_off:za_off + v_w]).astype(_BF)
    zs_off = u_off + u_w
    rest = w_ref.shape[1] - zs_off
    piece = 512
    for p in range(rest // piece):
        gates_ref[0, :, v_w + p * piece:v_w + (p + 1) * piece] = _dot(
            h, w_ref[:, zs_off + p * piece:zs_off + (p + 1) * piece]).astype(_BF)


def _inproj(x, mod3, g_pre, w_in_bf, wvt_bf, pos_rows, *, rows, qk_w, v_w, u_off, u_w):
    bsz, s, d = x.shape
    n_in = w_in_bf.shape[1]
    gates_w = n_in - 2 * qk_w - v_w - u_w
    n_heads = v_w // V_HEAD_DIM
    vt_rows = V_HEAD_DIM + VT_PAD
    kern = functools.partial(_inproj_kernel, d=d, qk_w=qk_w, v_w=v_w, u_off=u_off, u_w=u_w)
    freq, spread = _rope_constants()
    seq_spec = lambda w: pl.BlockSpec((1, rows, w), lambda b, t: (b, t, 0))
    return pl.pallas_call(
        kern,
        grid=(bsz, s // rows),
        in_specs=[seq_spec(d),
                  pl.BlockSpec((1, 1, 3 * d), lambda b, t: (b, 0, 0)),
                  pl.BlockSpec((1, d), lambda b, t: (0, 0)),
                  pl.BlockSpec((d, n_in), lambda b, t: (0, 0)),
                  pl.BlockSpec((v_w, d), lambda b, t: (0, 0)),
                  pl.BlockSpec((1, 1, rows), lambda b, t: (b * (s // rows) + t, 0, 0)),
                  pl.BlockSpec(freq.shape, lambda b, t: (0, 0)),
                  pl.BlockSpec(spread.shape, lambda b, t: (0, 0))],
        out_specs=[pl.BlockSpec((1, n_heads, 2 * HEAD_DIM, rows), lambda b, t: (b, 0, 0, t)),
                   seq_spec(qk_w),
                   pl.BlockSpec((1, n_heads, rows // KEY_BLOCK, vt_rows, KEY_BLOCK),
                                lambda b, t: (b, 0, t, 0, 0)),
                   seq_spec(gates_w), seq_spec(u_w)],
        out_shape=[jax.ShapeDtypeStruct((bsz, n_heads, 2 * HEAD_DIM, s), _BF),
                   jax.ShapeDtypeStruct((bsz, s, qk_w), _BF),
                   jax.ShapeDtypeStruct((bsz, n_heads, s // KEY_BLOCK, vt_rows, KEY_BLOCK), _BF),
                   jax.ShapeDtypeStruct((bsz, s, gates_w), _BF),
                   jax.ShapeDtypeStruct((bsz, s, u_w), _BF)],
        compiler_params=pltpu.CompilerParams(
            dimension_semantics=("arbitrary", "arbitrary"), vmem_limit_bytes=VMEM_LIMIT_BYTES),
        name="norm_inproj",
    )(x, mod3, g_pre.reshape(1, d), w_in_bf, wvt_bf, pos_rows, freq, spread)


def _attn_kernel(lamqk_ref, gsub_ref, qt_ref, k_ref, vt_ref, o_ref, *scratch, lam_init):
    s_ref = (scratch[0:2], scratch[2:4])
    e_ref = (scratch[4:6], scratch[6:8])
    m_ref = (scratch[8:10], scratch[10:12])
    n_kb, tk, tq = s_ref[0][0].shape
    step = pl.program_id(0)

    @pl.when(step == 0)
    def _():
        for ref in scratch:
            ref[...] = jnp.zeros_like(ref)

    lf = lamqk_ref[...]
    lam = (jnp.exp(jnp.sum(lf[0:1, :] * lf[1:2, :], axis=-1, keepdims=True))
           - jnp.exp(jnp.sum(lf[2:3, :] * lf[3:4, :], axis=-1, keepdims=True)) + lam_init)

    def tick(new, old):
        qt = qt_ref[0, 0]
        first = lax.broadcasted_iota(jnp.int32, (2 * HEAD_DIM, 1), 0) < HEAD_DIM
        zero = jnp.zeros_like(qt)
        qmt = (jnp.where(first, qt, zero), jnp.where(first, zero, qt))
        m_rows = [jnp.max(m_ref[old][mp][...], axis=0, keepdims=True) for mp in range(2)]
        evs = []
        for mp in range(2):
            ev = None
            m = jnp.full((8, tq), -jnp.inf, _F32)
            for kb in range(n_kb):
                st = _dot(k_ref[0, kb * tk:(kb + 1) * tk, :], qmt[mp])
                s_ref[new][mp][kb] = st
                m = jnp.maximum(m, jnp.max(st.reshape(tk // 8, 8, tq), axis=0))
                part = _dot(vt_ref[0, 0, kb], e_ref[new][mp][kb])
                ev = part if ev is None else ev + part
                e_ref[old][mp][kb] = jnp.exp(s_ref[old][mp][kb] - m_rows[mp]).astype(_BF)
            m_ref[new][mp][...] = m
            evs.append(ev[0:V_HEAD_DIM, :] * (1.0 / ev[V_HEAD_DIM:V_HEAD_DIM + 1, :]))
        ot = evs[0] - lam * evs[1]
        inv = lax.rsqrt(jnp.mean(ot * ot, axis=0, keepdims=True) + EPS)
        o_ref[0] = (((ot * inv) * gsub_ref[...]) * (1.0 - lam_init)).T.astype(_BF)

    @pl.when(step % 2 == 0)
    def _():
        tick(0, 1)

    @pl.when(step % 2 == 1)
    def _():
        tick(1, 0)


def _attention(lam_qk, g_subln, qt, k, vt, *, tq, lam_init):
    bsz, n_heads, hw, s = qt.shape
    _, _, n_kb, vt_rows, tk = vt.shape
    n_q = s // tq
    n_items = bsz * n_heads * n_q
    kern = functools.partial(_attn_kernel, lam_init=lam_init)

    def item(j):
        j = jnp.clip(j, 0, n_items - 1)
        return j // (n_heads * n_q), (j // n_q) % n_heads, j % n_q

    def qt_map(g):
        b, h, i = item(g)
        return b, h, 0, i

    def k_map(g):
        b, h, _ = item(g)
        return b, 0, h

    def vt_map(g):
        b, h, _ = item(g - 2)
        return b, h, 0, 0, 0

    def o_map(g):
        b, h, i = item(g - 2)
        return b, i, h

    return pl.pallas_call(
        kern,
        grid=(n_items + 2,),
        in_specs=[pl.BlockSpec(lam_qk.shape, lambda g: (0, 0)),
                  pl.BlockSpec((V_HEAD_DIM, 1), lambda g: (0, 0)),
                  pl.BlockSpec((1, 1, hw, tq), qt_map),
                  pl.BlockSpec((1, s, hw), k_map),
                  pl.BlockSpec((1, 1, n_kb, vt_rows, tk), vt_map)],
        out_specs=pl.BlockSpec((1, tq, V_HEAD_DIM), o_map),
        out_shape=jax.ShapeDtypeStruct((bsz, s, n_heads * V_HEAD_DIM), _BF),
        scratch_shapes=([pltpu.VMEM((n_kb, tk, tq), _F32)] * 4 + [pltpu.VMEM((n_kb, tk, tq), _BF)] * 4
                        + [pltpu.VMEM((8, tq), _F32)] * 4),
        compiler_params=pltpu.CompilerParams(
            dimension_semantics=("arbitrary",), vmem_limit_bytes=VMEM_LIMIT_BYTES),
        name="diff_attention",
    )(lam_qk, g_subln.reshape(V_HEAD_DIM, 1), qt, k, vt)


def _ssm_kernel(ut_ref, lam_ref, bt_ref, ct_ref, dsk_ref, yt_ref, at_ref, z_ref, xp_ref,
                *, bsz, n_c):
    el = SSM_CHUNK
    cw = el * SSM_GROUP
    sw = 2 * SSM_STATE
    n_cols = n_c * bsz
    in_group = [lax.broadcasted_iota(jnp.int32, (1, sw), 1) // SSM_STATE == gl for gl in range(2)]

    def group_rows(ref, gl):
        return ref[:, gl * SSM_GROUP:(gl + 1) * SSM_GROUP, :].reshape(cw, n_cols)

    def per_row(rows, cols):
        return (jnp.broadcast_to(rows[:, None, :], (el, SSM_GROUP, sw)),
                jnp.broadcast_to(cols[None, :, :], (el, SSM_GROUP, sw)))

    def cmul_rows(p_r, p_i, m_r, m_i):
        pr3, mr3 = per_row(p_r, m_r)
        pi3, mi3 = per_row(p_i, m_i)
        return (pr3 * mr3 - pi3 * mi3).reshape(cw, sw), (pr3 * mi3 + pi3 * mr3).reshape(cw, sw)

    idx = lax.broadcasted_iota(jnp.int32, (el, 1), 0).astype(_F32)
    ops = []
    for e in range(2):
        lr = jnp.minimum(lam_ref[0, e, 0:1, :], -1e-4)
        li = lam_ref[0, e, 1:2, :]
        dt = jnp.exp(lam_ref[0, e, 2:3, :])

        def power(k, lr=lr, li=li, dt=dt):
            mag = jnp.exp(k * (lr * dt))
            return mag * jnp.cos(k * (li * dt)), mag * jnp.sin(k * (li * dt))

        one_r, one_i = power(1.0)
        den = lr * lr + li * li
        nr, ni = one_r - 1.0, one_i
        coef_r = (nr * lr + ni * li) / den
        coef_i = (ni * lr - nr * li) / den
        b_r, b_i = bt_ref[0, e, 0], bt_ref[0, e, 1]
        bb_r = coef_r * b_r - coef_i * b_i
        bb_i = coef_r * b_i + coef_i * b_r
        c_r, c_i = ct_ref[0, e, 0], ct_ref[0, e, 1]
        ops.append(dict(power=power, bb=(bb_r, bb_i), c=(c_r, c_i)))
    fwd, bwd = ops

    zf_r, zf_i = cmul_rows(*fwd["power"](el - 1.0 - idx), *fwd["bb"])
    zb_r, zb_i = cmul_rows(*bwd["power"](idx), *bwd["bb"])
    yf_r, yf_i = cmul_rows(*fwd["power"](idx + 1.0), *fwd["c"])
    yb_r, yb_i = cmul_rows(*bwd["power"](el - idx), *bwd["c"])
    kf_r, kf_i = cmul_rows(*fwd["power"](idx), *fwd["c"])
    kb_r, kb_i = cmul_rows(*bwd["power"](idx), *bwd["c"])

    def pair_slot(parts, gl):
        return jnp.concatenate([jnp.where(in_group[gl], p, 0.0) for p in parts], axis=1).astype(_BF)

    for gl in range(2):
        at_ref[gl] = group_rows(ut_ref, gl).T
    z = (_dot(at_ref[0], pair_slot([zf_r, zf_i, zb_r, zb_i], 0))
         + _dot(at_ref[1], pair_slot([zf_r, zf_i, zb_r, zb_i], 1)))
    for comp in range(4):
        z_ref[comp] = z[:, comp * sw:(comp + 1) * sw]

    a_fr, a_fi = fwd["power"](float(el))
    a_br, a_bi = bwd["power"](float(el))

    def step(c, carry):
        fr, fi, br, bi = carry
        rows_f = pl.ds(pl.multiple_of(c * bsz, bsz), bsz)
        rows_b = pl.ds(pl.multiple_of((n_c - 1 - c) * bsz, bsz), bsz)
        xp_ref[0, rows_f, :] = fr
        xp_ref[1, rows_f, :] = fi
        xp_ref[2, rows_b, :] = br
        xp_ref[3, rows_b, :] = bi
        zfr = z_ref[0, rows_f, :]
        zfi = z_ref[1, rows_f, :]
        zbr = z_ref[2, rows_b, :]
        zbi = z_ref[3, rows_b, :]
        return (a_fr * fr - a_fi * fi + zfr, a_fr * fi + a_fi * fr + zfi,
                a_br * br - a_bi * bi + zbr, a_br * bi + a_bi * br + zbi)

    zero = jnp.zeros((bsz, sw), _F32)
    lax.fori_loop(0, n_c, step, (zero, zero, zero, zero), unroll=4)
    xp = jnp.concatenate([xp_ref[comp] for comp in range(4)], axis=1).astype(_BF)

    spread = (lax.broadcasted_iota(jnp.int32, (SSM_GROUP, cw), 1) % SSM_GROUP
              == lax.broadcasted_iota(jnp.int32, (SSM_GROUP, cw), 0)).astype(_BF)
    lag = (lax.broadcasted_iota(jnp.int32, (cw, cw), 0) // SSM_GROUP
           - lax.broadcasted_iota(jnp.int32, (cw, cw), 1) // SSM_GROUP)

    def lag_kernels(k_r, k_i, bb, gl):
        k_r, k_i = jnp.where(in_group[gl], k_r, 0.0), jnp.where(in_group[gl], k_i, 0.0)
        hi_dot = functools.partial(lax.dot_general, dimension_numbers=(((1,), (1,)), ((), ())),
                                   precision=_HI, preferred_element_type=_F32)
        return hi_dot(k_r, bb[0]) - hi_dot(k_i, bb[1])

    for gl in range(2):
        ktf = _dot(lag_kernels(kf_r, kf_i, fwd["bb"], gl).astype(_BF), spread)
        ktb = _dot(lag_kernels(kb_r, kb_i, bwd["bb"], gl).astype(_BF), spread)

        def lag_rows(kt, k):
            blk = kt[k * SSM_GROUP:(k + 1) * SSM_GROUP, :]
            return jnp.broadcast_to(blk[None], (el, SSM_GROUP, cw)).reshape(cw, cw)

        tm = jnp.where(lag == 0, lag_rows(ktf, 0) + lag_rows(ktb, 0), 0.0)
        for k in range(1, el):
            tm = jnp.where(lag == k, lag_rows(ktf, k), tm)
            tm = jnp.where(lag == -k, lag_rows(ktb, k), tm)
        u_g = group_rows(ut_ref, gl)
        yt = _dot(tm.astype(_BF), u_g) + _nt_dot(pair_slot([yf_r, -yf_i, yb_r, -yb_i], gl), xp)
        yt = yt + dsk_ref[0, gl] * u_g.astype(_F32)
        yt_ref[:, gl * SSM_GROUP:(gl + 1) * SSM_GROUP, :] = yt.reshape(
            el, SSM_GROUP, n_cols).astype(_BF)


def _ssm(ut, lam, bt, ct, dsk, *, bsz):
    _, u_w, n_cols = ut.shape
    n_c = n_cols // bsz
    n_pairs = u_w // (2 * SSM_GROUP)
    cw = SSM_CHUNK * SSM_GROUP
    sw = 2 * SSM_STATE
    kern = functools.partial(_ssm_kernel, bsz=bsz, n_c=n_c)
    blk = pl.BlockSpec((SSM_CHUNK, 2 * SSM_GROUP, n_cols), lambda g: (0, g, 0))
    par = pl.BlockSpec((1, 2, 2, SSM_GROUP, sw), lambda g: (g, 0, 0, 0, 0))
    return pl.pallas_call(
        kern,
        grid=(n_pairs,),
        in_specs=[blk,
                  pl.BlockSpec((1, 2, 3, sw), lambda g: (g, 0, 0, 0)),
                  par, par,
                  pl.BlockSpec((1, 2, cw, 1), lambda g: (g, 0, 0, 0))],
        out_specs=blk,
        out_shape=jax.ShapeDtypeStruct(ut.shape, _BF),
        scratch_shapes=[pltpu.VMEM((2, n_cols, cw), _BF),
                        pltpu.VMEM((4, n_cols, sw), _F32),
                        pltpu.VMEM((4, n_cols, sw), _F32)],
        compiler_params=pltpu.CompilerParams(
            dimension_semantics=("arbitrary",), vmem_limit_bytes=VMEM_LIMIT_BYTES),
        name="s5_chunked",
    )(ut, lam, bt, ct, dsk)


def _post_kernel(o_ref, gates_ref, y_ref, x_ref, mod_ref, bglu_ref, gfin_ref,
                 wua_ref, wglu_ref, wus_ref, wout_ref, out_ref, *, d, aw, sw):
    y = jax.nn.gelu(y_ref[0].astype(_F32))
    y = y * jax.nn.sigmoid(_dot(y.astype(_BF), wglu_ref[...]) + bglu_ref[...])
    z_s = gates_ref[0, :, aw:aw + sw].astype(_F32)
    s_br = _dot((y * jax.nn.silu(z_s)).astype(_BF), wus_ref[...])
    z_a = gates_ref[0, :, 0:aw].astype(_F32)
    a_br = _dot((o_ref[0].astype(_F32) * jax.nn.silu(z_a)).astype(_BF), wua_ref[...])
    g_off = aw + sw
    g_a = gates_ref[0, :, g_off:g_off + d].astype(_F32)
    g_s = gates_ref[0, :, g_off + d:g_off + 2 * d].astype(_F32)
    merged = jax.nn.sigmoid(g_a) * a_br + jax.nn.sigmoid(g_s) * s_br
    r = _dot(merged.astype(_BF), wout_ref[...])
    xo = x_ref[0] + mod_ref[0, :, 2 * d:3 * d] * r
    inv = lax.rsqrt(jnp.mean(xo * xo, axis=-1, keepdims=True) + EPS)
    out_ref[0] = (xo * inv) * gfin_ref[...]


def _post(o, gates, y, x, mod3, b_glu, g_final, wua, wglu, wus, wout, *, rows):
    bsz, s, d = x.shape
    aw = o.shape[2]
    sw = y.shape[2]
    kern = functools.partial(_post_kernel, d=d, aw=aw, sw=sw)
    full = lambda a: pl.BlockSpec(a.shape, lambda b, t: (0,) * a.ndim)
    seq_spec = lambda w: pl.BlockSpec((1, rows, w), lambda b, t: (b, t, 0))
    bgl = b_glu.reshape(1, sw)
    gfi = g_final.reshape(1, d)
    return pl.pallas_call(
        kern,
        grid=(bsz, s // rows),
        in_specs=[seq_spec(aw), seq_spec(gates.shape[2]), seq_spec(sw), seq_spec(d),
                  pl.BlockSpec((1, 1, 3 * d), lambda b, t: (b, 0, 0)),
                  full(bgl), full(gfi), full(wua), full(wglu), full(wus), full(wout)],
        out_specs=seq_spec(d),
        out_shape=jax.ShapeDtypeStruct(x.shape, _F32),
        compiler_params=pltpu.CompilerParams(
            dimension_semantics=("arbitrary", "arbitrary"), vmem_limit_bytes=VMEM_LIMIT_BYTES),
        name="merge_out",
    )(o, gates, y, x, mod3, bgl, gfi, wua, wglu, wus, wout)


def _rope_constants():
    half = ROT_DIM // 2
    freq = (ROPE_THETA ** (-jnp.arange(half, dtype=_F32) * 2.0 / ROT_DIM)).reshape(half, 1)
    in_head = jnp.arange(LANES) % HEAD_DIM
    j = jnp.arange(half)[:, None]
    cos_sel = ((in_head == j) | (in_head == half + j)).astype(_F32)
    sin_sel = (in_head == half + j).astype(_F32) - (in_head == j).astype(_F32)
    zero = jnp.zeros_like(cos_sel)
    return freq, jnp.concatenate([jnp.concatenate([cos_sel, zero], axis=1),
                                  jnp.concatenate([zero, sin_sel], axis=1)], axis=0)


def _ssm_params(lam_re, lam_im, log_dt, b_re, b_im, c_re, c_im):
    n_g, n_p = lam_re.shape[1], lam_re.shape[2]
    pairs = n_g // 2

    def lanes(m):
        rows = m.shape[2]
        return m.reshape(2, pairs, 2, rows, n_p).transpose(1, 0, 3, 2, 4).reshape(pairs, 2, rows, 2 * n_p)

    lam = lanes(jnp.stack([lam_re, lam_im, jnp.broadcast_to(log_dt[..., None], lam_re.shape)],
                          axis=2).astype(_F32))
    bt = jnp.stack([lanes(jnp.swapaxes(b_re, -1, -2)), lanes(jnp.swapaxes(b_im, -1, -2))], axis=2)
    ct = jnp.stack([lanes(c_re), lanes(c_im)], axis=2)
    return lam, bt.astype(_F32), ct.astype(_F32)


def _layer(x, c, positions, layer_idx, w_ada, b_ada, g_pre, w_in, lam_qk, g_subln,
           lam_re, lam_im, log_dt, b_re, b_im, c_re, c_im, d_skip,
           w_glu, b_glu, w_up_attn, w_up_ssm, w_out, g_final):
    bsz, s, d = x.shape
    n_c = s // SSM_CHUNK
    qk_w = ATTN_HEADS * 2 * HEAD_DIM
    v_w = ATTN_HEADS * V_HEAD_DIM
    u_w = lam_re.shape[1] * SSM_GROUP
    u_off = 2 * qk_w + 2 * v_w
    lam_init = 0.8 - 0.6 * math.exp(-0.3 * layer_idx)
    mod3 = _modulation(c, w_ada, b_ada).reshape(bsz, 1, 3 * d)
    w_in_bf = w_in.astype(_BF)
    rows = min(ROW_BLOCK, s)
    wvt_bf = w_in_bf[:, 2 * qk_w:2 * qk_w + v_w].T
    pos_rows = positions.astype(_F32).reshape(bsz * (s // rows), 1, rows)
    qt, k, vt, gates, u = _inproj(x, mod3, g_pre, w_in_bf, wvt_bf, pos_rows, rows=rows,
                                  qk_w=qk_w, v_w=v_w, u_off=u_off, u_w=u_w)
    o = _attention(lam_qk, g_subln, qt, k, vt, tq=min(Q_ROWS, s), lam_init=lam_init)
    ut = u.reshape(bsz, n_c, SSM_CHUNK, u_w).transpose(2, 3, 1, 0).reshape(SSM_CHUNK, u_w, n_c * bsz)
    dsk = jnp.broadcast_to(d_skip.reshape(u_w // (2 * SSM_GROUP), 2, 1, SSM_GROUP, 1),
                           (u_w // (2 * SSM_GROUP), 2, SSM_CHUNK, SSM_GROUP, 1)).reshape(
                               u_w // (2 * SSM_GROUP), 2, SSM_CHUNK * SSM_GROUP, 1)
    yt = _ssm(ut, *_ssm_params(lam_re, lam_im, log_dt, b_re, b_im, c_re, c_im), dsk, bsz=bsz)
    y = yt.reshape(SSM_CHUNK, u_w, n_c, bsz).transpose(3, 2, 0, 1).reshape(bsz, s, u_w)
    return _post(o, gates, y, x, mod3, b_glu, g_final,
                 w_up_attn.astype(_BF), w_glu.astype(_BF), w_up_ssm.astype(_BF), w_out.astype(_BF),
                 rows=rows)


def kernel(x, c, positions, w_ada, b_ada, g_pre, w_in, lam_qk, g_subln, ssm_lam_re, ssm_lam_im,
           ssm_log_dt, ssm_b_re, ssm_b_im, ssm_c_re, ssm_c_im, ssm_d, w_glu, b_glu, w_up_attn,
           w_up_ssm, w_out, g_final):
    depth = w_ada.shape[0]
    assert depth == 1, "the final RMSNorm is fused into the single layer's epilogue"
    assert x.shape[1] % (SSM_CHUNK * 8) == 0
    return _layer(x, c, positions, 0, w_ada[0], b_ada[0], g_pre[0], w_in[0],
                  lam_qk[0], g_subln[0], ssm_lam_re[0], ssm_lam_im[0], ssm_log_dt[0], ssm_b_re[0],
                  ssm_b_im[0], ssm_c_re[0], ssm_c_im[0], ssm_d[0], w_glu[0], b_glu[0],
                  w_up_attn[0], w_up_ssm[0], w_out[0], g_final)
```

```python
import functools
import math

import jax
import jax.numpy as jnp
from jax import lax
from jax.experimental import pallas as pl
from jax.experimental.pallas import tpu as pltpu

ATTN_HEADS = 4
HEAD_DIM = 64
V_HEAD_DIM = 2 * HEAD_DIM
ROT_DIM = HEAD_DIM // 4
ROPE_THETA = 500000.0
SSM_GROUP = 16
SSM_STATE = 64
SSM_CHUNK = 16
EPS = 1e-6
LOG2_E = math.log2(math.e)
LANES = 128
VMEM_LIMIT_BYTES = 56 * 1024 * 1024
ROW_BLOCK = 512
Q_ROWS = 512
KEY_BLOCK = 256
VT_PAD = 16

_HI = lax.Precision.HIGHEST
_BF = jnp.bfloat16
_F32 = jnp.float32


def _nt_dot(a, b):
    return lax.dot_general(a, b, (((1,), (1,)), ((), ())), preferred_element_type=_F32)


def _dot(a, b):
    return jnp.dot(a, b, preferred_element_type=_F32)


def _sigmoid(x):
    return 0.5 * jnp.tanh(0.5 * x) + 0.5


def _mod_kernel(c_ref, w_ref, b_ref, lamqk_ref, o_ref, lam_ref, *, lam_init):
    c = c_ref[...]
    o_ref[...] = jnp.dot(jax.nn.silu(c), w_ref[...], precision=_HI,
                         preferred_element_type=_F32) + b_ref[...]
    lf = lamqk_ref[...]
    lam = (jnp.exp(jnp.sum(lf[0:1, :] * lf[1:2, :], axis=-1, keepdims=True))
           - jnp.exp(jnp.sum(lf[2:3, :] * lf[3:4, :], axis=-1, keepdims=True)) + lam_init)
    lam_ref[...] = jnp.broadcast_to(lam, lam_ref.shape)


def _modulation(c, w_ada, b_ada, lam_qk, *, lam_init):
    bsz, d = c.shape
    n = w_ada.shape[1]
    tn = d
    return pl.pallas_call(
        functools.partial(_mod_kernel, lam_init=lam_init),
        grid=(n // tn,),
        in_specs=[pl.BlockSpec((bsz, d), lambda j: (0, 0)),
                  pl.BlockSpec((d, tn), lambda j: (0, j)),
                  pl.BlockSpec((1, tn), lambda j: (0, j)),
                  pl.BlockSpec(lam_qk.shape, lambda j: (0, 0))],
        out_specs=[pl.BlockSpec((bsz, tn), lambda j: (0, j)),
                   pl.BlockSpec((1, LANES), lambda j: (0, 0))],
        out_shape=[jax.ShapeDtypeStruct((bsz, n), _F32), jax.ShapeDtypeStruct((1, LANES), _F32)],
        compiler_params=pltpu.CompilerParams(
            dimension_semantics=("arbitrary",), vmem_limit_bytes=VMEM_LIMIT_BYTES),
        name="adaln_mod",
    )(c, w_ada, b_ada.reshape(1, n), lam_qk)


def _inproj_kernel(x_ref, mod_ref, gpre_ref, w_ref, wvt_ref, pos_ref, freq_ref, spread_ref,
                   qt_ref, k_ref, vt_ref, gates_ref, u_ref, *, d, qk_w, v_w, u_off, u_w):
    shift = mod_ref[0, :, 0:d]
    scale = mod_ref[0, :, d:2 * d]
    x = x_ref[0]
    inv = lax.rsqrt(jnp.mean(x * x, axis=-1, keepdims=True) + EPS)
    h = ((x * inv) * (gpre_ref[...] * (1.0 + scale)) + shift).astype(_BF)

    rows = x.shape[0]
    half = ROT_DIM // 2
    ang = freq_ref[...] * pos_ref[0]
    cs = jnp.concatenate([jnp.cos(ang), jnp.sin(ang)], axis=0)
    tab = jnp.dot(cs.T, spread_ref[...], precision=_HI, preferred_element_type=_F32)
    lane = lax.broadcasted_iota(jnp.int32, (1, LANES), 1)
    reps = qk_w // LANES
    rot_c = jnp.concatenate([jnp.where(lane % HEAD_DIM < ROT_DIM, tab[:, 0:LANES], 1.0)] * reps,
                            axis=1)
    rot_s = jnp.concatenate([tab[:, LANES:2 * LANES]] * reps, axis=1)
    low = lax.broadcasted_iota(jnp.int32, (1, qk_w), 1) % HEAD_DIM < half

    def rope(t):
        up = pltpu.roll(t, qk_w - half, 1)
        dn = pltpu.roll(t, half, 1)
        return t * rot_c + jnp.where(low, up, dn) * rot_s

    v_off = 2 * qk_w
    vt = _nt_dot(wvt_ref[...], h).astype(_BF)
    pad = (lax.broadcasted_iota(jnp.int32, (VT_PAD, KEY_BLOCK), 0) == 0).astype(_BF)
    for hd in range(v_w // V_HEAD_DIM):
        for c in range(rows // KEY_BLOCK):
            vt_ref[0, hd, c, 0:V_HEAD_DIM, :] = vt[hd * V_HEAD_DIM:(hd + 1) * V_HEAD_DIM,
                                                   c * KEY_BLOCK:(c + 1) * KEY_BLOCK]
            vt_ref[0, hd, c, V_HEAD_DIM:V_HEAD_DIM + VT_PAD, :] = pad
    u_ref[0] = _dot(h, w_ref[:, u_off:u_off + u_w]).astype(_BF)
    qt = rope(_dot(h, w_ref[:, 0:qk_w]) * (HEAD_DIM ** -0.5 * LOG2_E)).T.astype(_BF)
    hw = 2 * HEAD_DIM
    for hd in range(qk_w // hw):
        qt_ref[0, hd] = qt[hd * hw:(hd + 1) * hw, :]
    k_ref[0] = rope(_dot(h, w_ref[:, qk_w:2 * qk_w])).astype(_BF)
    za_off = v_off + v_w
    z_a = _dot(h, w_ref[:, za_off:za_off + v_w])
    gates_ref[0, :, 0:v_w] = (z_a * _sigmoid(z_a)).astype(_BF)
    zs_off = u_off + u_w
    z_s = _dot(h, w_ref[:, zs_off:zs_off + u_w])
    gates_ref[0, :, v_w:v_w + u_w] = (z_s * _sigmoid(z_s)).astype(_BF)
    g_off = zs_off + u_w
    rest = w_ref.shape[1] - g_off
    piece = 512
    for p in range(rest // piece):
        gates_ref[0, :, v_w + u_w + p * piece:v_w + u_w + (p + 1) * piece] = _sigmoid(_dot(
            h, w_ref[:, g_off + p * piece:g_off + (p + 1) * piece])).astype(_BF)


def _inproj(x, mod3, g_pre, w_in_bf, wvt_bf, pos_rows, *, rows, qk_w, v_w, u_off, u_w):
    bsz, s, d = x.shape
    n_in = w_in_bf.shape[1]
    gates_w = n_in - 2 * qk_w - v_w - u_w
    n_heads = v_w // V_HEAD_DIM
    vt_rows = V_HEAD_DIM + VT_PAD
    kern = functools.partial(_inproj_kernel, d=d, qk_w=qk_w, v_w=v_w, u_off=u_off, u_w=u_w)
    freq, spread = _rope_constants()
    seq_spec = lambda w: pl.BlockSpec((1, rows, w), lambda b, t: (b, t, 0))
    return pl.pallas_call(
        kern,
        grid=(bsz, s // rows),
        in_specs=[seq_spec(d),
                  pl.BlockSpec((1, 1, 3 * d), lambda b, t: (b, 0, 0)),
                  pl.BlockSpec((1, d), lambda b, t: (0, 0)),
                  pl.BlockSpec((d, n_in), lambda b, t: (0, 0)),
                  pl.BlockSpec((v_w, d), lambda b, t: (0, 0)),
                  pl.BlockSpec((1, 1, rows), lambda b, t: (b * (s // rows) + t, 0, 0)),
                  pl.BlockSpec(freq.shape, lambda b, t: (0, 0)),
                  pl.BlockSpec(spread.shape, lambda b, t: (0, 0))],
        out_specs=[pl.BlockSpec((1, n_heads, 2 * HEAD_DIM, rows), lambda b, t: (b, 0, 0, t)),
                   seq_spec(qk_w),
                   pl.BlockSpec((1, n_heads, rows // KEY_BLOCK, vt_rows, KEY_BLOCK),
                                lambda b, t: (b, 0, t, 0, 0)),
                   seq_spec(gates_w), seq_spec(u_w)],
        out_shape=[jax.ShapeDtypeStruct((bsz, n_heads, 2 * HEAD_DIM, s), _BF),
                   jax.ShapeDtypeStruct((bsz, s, qk_w), _BF),
                   jax.ShapeDtypeStruct((bsz, n_heads, s // KEY_BLOCK, vt_rows, KEY_BLOCK), _BF),
                   jax.ShapeDtypeStruct((bsz, s, gates_w), _BF),
                   jax.ShapeDtypeStruct((bsz, s, u_w), _BF)],
        compiler_params=pltpu.CompilerParams(
            dimension_semantics=("arbitrary", "arbitrary"), vmem_limit_bytes=VMEM_LIMIT_BYTES),
        name="norm_inproj",
    )(x, mod3, g_pre.reshape(1, d), w_in_bf, wvt_bf, pos_rows, freq, spread)


def _attn_kernel(lam_ref, gsub_ref, qt_ref, k_ref, vt_ref, o_ref, *scratch, lam_init):
    s_ref = (scratch[0:2], scratch[2:4])
    e_ref = (scratch[4:6], scratch[6:8])
    m_ref = (scratch[8:10], scratch[10:12])
    ev_ref = (scratch[12:14], scratch[14:16])
    n_kb, tk, tq = s_ref[0][0].shape
    step = pl.program_id(0)

    @pl.when(step == 0)
    def _():
        for ref in scratch:
            ref[...] = jnp.zeros_like(ref)

    lam = lam_ref[0:1, 0:1]

    def tick(new, old):
        ot = None
        for mp in range(2):
            ev = ev_ref[old][mp][...]
            nrm = ev[0:V_HEAD_DIM, :] * (1.0 / ev[V_HEAD_DIM:V_HEAD_DIM + 1, :])
            ot = nrm if ot is None else ot - lam * nrm
        inv = lax.rsqrt(jnp.mean(ot * ot, axis=0, keepdims=True) + EPS)
        o_ref[0] = (((ot * inv) * gsub_ref[...]) * (1.0 - lam_init)).T.astype(_BF)
        qt = qt_ref[0, 0]
        first = lax.broadcasted_iota(jnp.int32, (2 * HEAD_DIM, 1), 0) < HEAD_DIM
        zero = jnp.zeros_like(qt)
        qmt = (jnp.where(first, qt, zero), jnp.where(first, zero, qt))
        m_rows = [jnp.max(m_ref[old][mp][...], axis=0, keepdims=True) for mp in range(2)]
        for mp in range(2):
            ev = None
            m = jnp.full((8, tq), -jnp.inf, _F32)
            for kb in range(n_kb):
                st = _dot(k_ref[0, kb * tk:(kb + 1) * tk, :], qmt[mp])
                s_ref[new][mp][kb] = st
                m = jnp.maximum(m, jnp.max(st.reshape(tk // 8, 8, tq), axis=0))
                part = _dot(vt_ref[0, 0, kb], e_ref[new][mp][kb])
                ev = part if ev is None else ev + part
                e_ref[old][mp][kb] = jnp.exp2(s_ref[old][mp][kb] - m_rows[mp]).astype(_BF)
            m_ref[new][mp][...] = m
            ev_ref[new][mp][...] = ev

    @pl.when(step % 2 == 0)
    def _():
        tick(0, 1)

    @pl.when(step % 2 == 1)
    def _():
        tick(1, 0)


def _attention(lam, g_subln, qt, k, vt, *, tq, lam_init):
    bsz, n_heads, hw, s = qt.shape
    _, _, n_kb, vt_rows, tk = vt.shape
    n_q = s // tq
    n_items = bsz * n_heads * n_q
    kern = functools.partial(_attn_kernel, lam_init=lam_init)

    def item(j):
        j = jnp.clip(j, 0, n_items - 1)
        return j // (n_heads * n_q), (j // n_q) % n_heads, j % n_q

    def qt_map(g):
        b, h, i = item(g)
        return b, h, 0, i

    def k_map(g):
        b, h, _ = item(g)
        return b, 0, h

    def vt_map(g):
        b, h, _ = item(g - 2)
        return b, h, 0, 0, 0

    def o_map(g):
        b, h, i = item(g - 3)
        return b, i, h

    return pl.pallas_call(
        kern,
        grid=(n_items + 3,),
        in_specs=[pl.BlockSpec(lam.shape, lambda g: (0, 0)),
                  pl.BlockSpec((V_HEAD_DIM, 1), lambda g: (0, 0)),
                  pl.BlockSpec((1, 1, hw, tq), qt_map),
                  pl.BlockSpec((1, s, hw), k_map),
                  pl.BlockSpec((1, 1, n_kb, vt_rows, tk), vt_map)],
        out_specs=pl.BlockSpec((1, tq, V_HEAD_DIM), o_map),
        out_shape=jax.ShapeDtypeStruct((bsz, s, n_heads * V_HEAD_DIM), _BF),
        scratch_shapes=([pltpu.VMEM((n_kb, tk, tq), _F32)] * 4 + [pltpu.VMEM((n_kb, tk, tq), _BF)] * 4
                        + [pltpu.VMEM((8, tq), _F32)] * 4 + [pltpu.VMEM((vt_rows, tq), _F32)] * 4),
        compiler_params=pltpu.CompilerParams(
            dimension_semantics=("arbitrary",), vmem_limit_bytes=VMEM_LIMIT_BYTES),
        name="diff_attention",
    )(lam, g_subln.reshape(V_HEAD_DIM, 1), qt, k, vt)


def _ssm_kernel(ut_ref, lam_ref, bt_ref, ct_ref, dsk_ref, yt_ref, at_ref, z_ref, xp_ref,
                *, bsz, n_c):
    el = SSM_CHUNK
    cw = el * SSM_GROUP
    sw = 2 * SSM_STATE
    n_cols = n_c * bsz
    in_group = [lax.broadcasted_iota(jnp.int32, (1, sw), 1) // SSM_STATE == gl for gl in range(2)]

    def group_rows(ref, gl):
        return ref[:, gl * SSM_GROUP:(gl + 1) * SSM_GROUP, :].reshape(cw, n_cols)

    def per_row(rows, cols):
        return (jnp.broadcast_to(rows[:, None, :], (el, SSM_GROUP, sw)),
                jnp.broadcast_to(cols[None, :, :], (el, SSM_GROUP, sw)))

    def cmul_rows(p_r, p_i, m_r, m_i):
        pr3, mr3 = per_row(p_r, m_r)
        pi3, mi3 = per_row(p_i, m_i)
        return (pr3 * mr3 - pi3 * mi3).reshape(cw, sw), (pr3 * mi3 + pi3 * mr3).reshape(cw, sw)

    idx = lax.broadcasted_iota(jnp.int32, (el, 1), 0).astype(_F32)
    ops = []
    for e in range(2):
        lr = jnp.minimum(lam_ref[0, e, 0:1, :], -1e-4)
        li = lam_ref[0, e, 1:2, :]
        dt = jnp.exp(lam_ref[0, e, 2:3, :])

        def power(k, lr=lr, li=li, dt=dt):
            mag = jnp.exp(k * (lr * dt))
            return mag * jnp.cos(k * (li * dt)), mag * jnp.sin(k * (li * dt))

        one_r, one_i = power(1.0)
        den = lr * lr + li * li
        nr, ni = one_r - 1.0, one_i
        coef_r = (nr * lr + ni * li) / den
        coef_i = (ni * lr - nr * li) / den
        b_r, b_i = bt_ref[0, e, 0], bt_ref[0, e, 1]
        bb_r = coef_r * b_r - coef_i * b_i
        bb_i = coef_r * b_i + coef_i * b_r
        c_r, c_i = ct_ref[0, e, 0], ct_ref[0, e, 1]
        ops.append(dict(power=power, bb=(bb_r, bb_i), c=(c_r, c_i)))
    fwd, bwd = ops

    zf_r, zf_i = cmul_rows(*fwd["power"](el - 1.0 - idx), *fwd["bb"])
    zb_r, zb_i = cmul_rows(*bwd["power"](idx), *bwd["bb"])
    yf_r, yf_i = cmul_rows(*fwd["power"](idx + 1.0), *fwd["c"])
    yb_r, yb_i = cmul_rows(*bwd["power"](el - idx), *bwd["c"])
    kf_r, kf_i = cmul_rows(*fwd["power"](idx), *fwd["c"])
    kb_r, kb_i = cmul_rows(*bwd["power"](idx), *bwd["c"])

    def pair_slot(parts, gl):
        return jnp.concatenate([jnp.where(in_group[gl], p, 0.0) for p in parts], axis=1).astype(_BF)

    for gl in range(2):
        at_ref[gl] = group_rows(ut_ref, gl).T
    z = (_dot(at_ref[0], pair_slot([zf_r, zf_i, zb_r, zb_i], 0))
         + _dot(at_ref[1], pair_slot([zf_r, zf_i, zb_r, zb_i], 1)))
    for comp in range(4):
        z_ref[comp] = z[:, comp * sw:(comp + 1) * sw]

    a_fr, a_fi = fwd["power"](float(el))
    a_br, a_bi = bwd["power"](float(el))

    def step(c, carry):
        fr, fi, br, bi = carry
        rows_f = pl.ds(pl.multiple_of(c * bsz, bsz), bsz)
        rows_b = pl.ds(pl.multiple_of((n_c - 1 - c) * bsz, bsz), bsz)
        xp_ref[0, rows_f, :] = fr
        xp_ref[1, rows_f, :] = fi
        xp_ref[2, rows_b, :] = br
        xp_ref[3, rows_b, :] = bi
        zfr = z_ref[0, rows_f, :]
        zfi = z_ref[1, rows_f, :]
        zbr = z_ref[2, rows_b, :]
        zbi = z_ref[3, rows_b, :]
        return (a_fr * fr - a_fi * fi + zfr, a_fr * fi + a_fi * fr + zfi,
                a_br * br - a_bi * bi + zbr, a_br * bi + a_bi * br + zbi)

    zero = jnp.zeros((bsz, sw), _F32)
    lax.fori_loop(0, n_c, step, (zero, zero, zero, zero), unroll=4)
    xp = jnp.concatenate([xp_ref[comp] for comp in range(4)], axis=1).astype(_BF)

    spread = (lax.broadcasted_iota(jnp.int32, (SSM_GROUP, cw), 1) % SSM_GROUP
              == lax.broadcasted_iota(jnp.int32, (SSM_GROUP, cw), 0)).astype(_BF)
    lag = (lax.broadcasted_iota(jnp.int32, (cw, cw), 0) // SSM_GROUP
           - lax.broadcasted_iota(jnp.int32, (cw, cw), 1) // SSM_GROUP)

    def lag_kernels(k_r, k_i, bb, gl):
        k_r, k_i = jnp.where(in_group[gl], k_r, 0.0), jnp.where(in_group[gl], k_i, 0.0)
        hi_dot = functools.partial(lax.dot_general, dimension_numbers=(((1,), (1,)), ((), ())),
                                   precision=_HI, preferred_element_type=_F32)
        return hi_dot(k_r, bb[0]) - hi_dot(k_i, bb[1])

    for gl in range(2):
        ktf = _dot(lag_kernels(kf_r, kf_i, fwd["bb"], gl).astype(_BF), spread)
        ktb = _dot(lag_kernels(kb_r, kb_i, bwd["bb"], gl).astype(_BF), spread)

        def lag_rows(kt, k):
            blk = kt[k * SSM_GROUP:(k + 1) * SSM_GROUP, :]
            return jnp.broadcast_to(blk[None], (el, SSM_GROUP, cw)).reshape(cw, cw)

        tm = jnp.where(lag == 0, lag_rows(ktf, 0) + lag_rows(ktb, 0), 0.0)
        for k in range(1, el):
            tm = jnp.where(lag == k, lag_rows(ktf, k), tm)
            tm = jnp.where(lag == -k, lag_rows(ktb, k), tm)
        u_g = group_rows(ut_ref, gl)
        yt = _dot(tm.astype(_BF), u_g) + _nt_dot(pair_slot([yf_r, -yf_i, yb_r, -yb_i], gl), xp)
        yt = yt + dsk_ref[0, gl] * u_g.astype(_F32)
        yt_ref[:, gl * SSM_GROUP:(gl + 1) * SSM_GROUP, :] = yt.reshape(
            el, SSM_GROUP, n_cols).astype(_BF)


def _ssm(ut, lam, bt, ct, dsk, *, bsz):
    _, u_w, n_cols = ut.shape
    n_c = n_cols // bsz
    n_pairs = u_w // (2 * SSM_GROUP)
    cw = SSM_CHUNK * SSM_GROUP
    sw = 2 * SSM_STATE
    kern = functools.partial(_ssm_kernel, bsz=bsz, n_c=n_c)
    blk = pl.BlockSpec((SSM_CHUNK, 2 * SSM_GROUP, n_cols), lambda g: (0, g, 0))
    par = pl.BlockSpec((1, 2, 2, SSM_GROUP, sw), lambda g: (g, 0, 0, 0, 0))
    return pl.pallas_call(
        kern,
        grid=(n_pairs,),
        in_specs=[blk,
                  pl.BlockSpec((1, 2, 3, sw), lambda g: (g, 0, 0, 0)),
                  par, par,
                  pl.BlockSpec((1, 2, cw, 1), lambda g: (g, 0, 0, 0))],
        out_specs=blk,
        out_shape=jax.ShapeDtypeStruct(ut.shape, _BF),
        scratch_shapes=[pltpu.VMEM((2, n_cols, cw), _BF),
                        pltpu.VMEM((4, n_cols, sw), _F32),
                        pltpu.VMEM((4, n_cols, sw), _F32)],
        compiler_params=pltpu.CompilerParams(
            dimension_semantics=("arbitrary",), vmem_limit_bytes=VMEM_LIMIT_BYTES),
        name="s5_chunked",
    )(ut, lam, bt, ct, dsk)


def _post_kernel(o_ref, gates_ref, y_ref, x_ref, mod_ref, bglu_ref, gfin_ref,
                 wua_ref, wglu_ref, wus_ref, wout_ref, out_ref, *, d, aw, sw):
    y = jax.nn.gelu(y_ref[0].astype(_F32))
    y = y * _sigmoid(_dot(y.astype(_BF), wglu_ref[...]) + bglu_ref[...])
    s_br = _dot((y * gates_ref[0, :, aw:aw + sw].astype(_F32)).astype(_BF), wus_ref[...])
    a_br = _dot((o_ref[0].astype(_F32) * gates_ref[0, :, 0:aw].astype(_F32)).astype(_BF),
                wua_ref[...])
    g_off = aw + sw
    merged = (gates_ref[0, :, g_off:g_off + d].astype(_F32) * a_br
              + gates_ref[0, :, g_off + d:g_off + 2 * d].astype(_F32) * s_br)
    r = _dot(merged.astype(_BF), wout_ref[...])
    xo = x_ref[0] + mod_ref[0, :, 2 * d:3 * d] * r
    inv = lax.rsqrt(jnp.mean(xo * xo, axis=-1, keepdims=True) + EPS)
    out_ref[0] = (xo * inv) * gfin_ref[...]


def _post(o, gates, y, x, mod3, b_glu, g_final, wua, wglu, wus, wout, *, rows):
    bsz, s, d = x.shape
    aw = o.shape[2]
    sw = y.shape[2]
    kern = functools.partial(_post_kernel, d=d, aw=aw, sw=sw)
    full = lambda a: pl.BlockSpec(a.shape, lambda b, t: (0,) * a.ndim)
    seq_spec = lambda w: pl.BlockSpec((1, rows, w), lambda b, t: (b, t, 0))
    bgl = b_glu.reshape(1, sw)
    gfi = g_final.reshape(1, d)
    return pl.pallas_call(
        kern,
        grid=(bsz, s // rows),
        in_specs=[seq_spec(aw), seq_spec(gates.shape[2]), seq_spec(sw), seq_spec(d),
                  pl.BlockSpec((1, 1, 3 * d), lambda b, t: (b, 0, 0)),
                  full(bgl), full(gfi), full(wua), full(wglu), full(wus), full(wout)],
        out_specs=seq_spec(d),
        out_shape=jax.ShapeDtypeStruct(x.shape, _F32),
        compiler_params=pltpu.CompilerParams(
            dimension_semantics=("arbitrary", "arbitrary"), vmem_limit_bytes=VMEM_LIMIT_BYTES),
        name="merge_out",
    )(o, gates, y, x, mod3, bgl, gfi, wua, wglu, wus, wout)


def _rope_constants():
    half = ROT_DIM // 2
    freq = (ROPE_THETA ** (-jnp.arange(half, dtype=_F32) * 2.0 / ROT_DIM)).reshape(half, 1)
    in_head = jnp.arange(LANES) % HEAD_DIM
    j = jnp.arange(half)[:, None]
    cos_sel = ((in_head == j) | (in_head == half + j)).astype(_F32)
    sin_sel = (in_head == half + j).astype(_F32) - (in_head == j).astype(_F32)
    zero = jnp.zeros_like(cos_sel)
    return freq, jnp.concatenate([jnp.concatenate([cos_sel, zero], axis=1),
                                  jnp.concatenate([zero, sin_sel], axis=1)], axis=0)


def _ssm_params(lam_re, lam_im, log_dt, b_re, b_im, c_re, c_im):
    n_g, n_p = lam_re.shape[1], lam_re.shape[2]
    pairs = n_g // 2

    def lanes(m):
        rows = m.shape[2]
        return m.reshape(2, pairs, 2, rows, n_p).transpose(1, 0, 3, 2, 4).reshape(pairs, 2, rows, 2 * n_p)

    lam = lanes(jnp.stack([lam_re, lam_im, jnp.broadcast_to(log_dt[..., None], lam_re.shape)],
                          axis=2).astype(_F32))
    bt = jnp.stack([lanes(jnp.swapaxes(b_re, -1, -2)), lanes(jnp.swapaxes(b_im, -1, -2))], axis=2)
    ct = jnp.stack([lanes(c_re), lanes(c_im)], axis=2)
    return lam, bt.astype(_F32), ct.astype(_F32)


def _layer(x, c, positions, layer_idx, w_ada, b_ada, g_pre, w_in, lam_qk, g_subln,
           lam_re, lam_im, log_dt, b_re, b_im, c_re, c_im, d_skip,
           w_glu, b_glu, w_up_attn, w_up_ssm, w_out, g_final):
    bsz, s, d = x.shape
    n_c = s // SSM_CHUNK
    qk_w = ATTN_HEADS * 2 * HEAD_DIM
    v_w = ATTN_HEADS * V_HEAD_DIM
    u_w = lam_re.shape[1] * SSM_GROUP
    u_off = 2 * qk_w + 2 * v_w
    lam_init = 0.8 - 0.6 * math.exp(-0.3 * layer_idx)
    mod, lam = _modulation(c, w_ada, b_ada, lam_qk, lam_init=lam_init)
    mod3 = mod.reshape(bsz, 1, 3 * d)
    w_in_bf = w_in.astype(_BF)
    rows = min(ROW_BLOCK, s)
    wvt_bf = w_in[:, 2 * qk_w:2 * qk_w + v_w].T.astype(_BF)
    pos_rows = positions.astype(_F32).reshape(bsz * (s // rows), 1, rows)
    qt, k, vt, gates, u = _inproj(x, mod3, g_pre, w_in_bf, wvt_bf, pos_rows, rows=rows,
                                  qk_w=qk_w, v_w=v_w, u_off=u_off, u_w=u_w)
    o = _attention(lam, g_subln, qt, k, vt, tq=min(Q_ROWS, s), lam_init=lam_init)
    ut = u.reshape(bsz, n_c, SSM_CHUNK, u_w).transpose(2, 3, 1, 0).reshape(SSM_CHUNK, u_w, n_c * bsz)
    dsk = jnp.broadcast_to(d_skip.reshape(u_w // (2 * SSM_GROUP), 2, 1, SSM_GROUP, 1),
                           (u_w // (2 * SSM_GROUP), 2, SSM_CHUNK, SSM_GROUP, 1)).reshape(
                               u_w // (2 * SSM_GROUP), 2, SSM_CHUNK * SSM_GROUP, 1)
    yt = _ssm(ut, *_ssm_params(lam_re, lam_im, log_dt, b_re, b_im, c_re, c_im), dsk, bsz=bsz)
    y = yt.reshape(SSM_CHUNK, u_w, n_c, bsz).transpose(3, 2, 0, 1).reshape(bsz, s, u_w)
    return _post(o, gates, y, x, mod3, b_glu, g_final,
                 w_up_attn.astype(_BF), w_glu.astype(_BF), w_up_ssm.astype(_BF), w_out.astype(_BF),
                 rows=rows)


def kernel(x, c, positions, w_ada, b_ada, g_pre, w_in, lam_qk, g_subln, ssm_lam_re, ssm_lam_im,
           ssm_log_dt, ssm_b_re, ssm_b_im, ssm_c_re, ssm_c_im, ssm_d, w_glu, b_glu, w_up_attn,
           w_up_ssm, w_out, g_final):
    depth = w_ada.shape[0]
    assert depth == 1, "the final RMSNorm is fused into the single layer's epilogue"
    assert x.shape[1] % (SSM_CHUNK * 8) == 0
    return _layer(x, c, positions, 0, w_ada[0], b_ada[0], g_pre[0], w_in[0],
                  lam_qk[0], g_subln[0], ssm_lam_re[0], ssm_lam_im[0], ssm_log_dt[0], ssm_b_re[0],
                  ssm_b_im[0], ssm_c_re[0], ssm_c_im[0], ssm_d[0], w_glu[0], b_glu[0],
                  w_up_attn[0], w_up_ssm[0], w_out[0], g_final)
```

```python
import functools
import math

import jax
import jax.numpy as jnp
from jax import lax
from jax.experimental import pallas as pl
from jax.experimental.pallas import tpu as pltpu

ATTN_HEADS = 4
HEAD_DIM = 64
V_HEAD_DIM = 2 * HEAD_DIM
ROT_DIM = HEAD_DIM // 4
ROPE_THETA = 500000.0
SSM_GROUP = 16
SSM_STATE = 64
SSM_CHUNK = 16
EPS = 1e-6
LOG2_E = math.log2(math.e)
LANES = 128
VMEM_LIMIT_BYTES = 56 * 1024 * 1024
ROW_BLOCK = 1024
Q_ROWS = 512
KEY_BLOCK = 256
VT_PAD = 16

_HI = lax.Precision.HIGHEST
_BF = jnp.bfloat16
_F32 = jnp.float32


def _nt_dot(a, b):
    return lax.dot_general(a, b, (((1,), (1,)), ((), ())), preferred_element_type=_F32)


def _dot(a, b):
    return jnp.dot(a, b, preferred_element_type=_F32)


def _sigmoid(x):
    return 0.5 * jnp.tanh(0.5 * x) + 0.5


def _prep_kernel(c_ref, w_ref, b_ref, lamqk_ref, win_ref, o_ref, lam_ref, winb_ref, wvt_ref,
                 *, lam_init, n_mod, v_tile):
    j = pl.program_id(0)

    @pl.when(j < n_mod)
    def _():
        o_ref[...] = jnp.dot(jax.nn.silu(c_ref[...]), w_ref[...], precision=_HI,
                             preferred_element_type=_F32) + b_ref[...]

    @pl.when(j == 0)
    def _():
        lf = lamqk_ref[...]
        lam = (jnp.exp(jnp.sum(lf[0:1, :] * lf[1:2, :], axis=-1, keepdims=True))
               - jnp.exp(jnp.sum(lf[2:3, :] * lf[3:4, :], axis=-1, keepdims=True)) + lam_init)
        lam_ref[...] = jnp.broadcast_to(lam, lam_ref.shape)

    w = win_ref[...]
    winb_ref[...] = w.astype(_BF)

    @pl.when(j == v_tile)
    def _():
        wvt_ref[...] = w.T.astype(_BF)


def _prepare(c, w_ada, b_ada, lam_qk, w_in, *, lam_init, v_off, v_w):
    bsz, d = c.shape
    n = w_ada.shape[1]
    n_in = w_in.shape[1]
    tn = d
    n_mod = n // tn
    assert v_off % v_w == 0 and n_in % v_w == 0 and n_in // v_w >= n_mod
    last = n_mod - 1
    return pl.pallas_call(
        functools.partial(_prep_kernel, lam_init=lam_init, n_mod=n_mod, v_tile=v_off // v_w),
        grid=(n_in // v_w,),
        in_specs=[pl.BlockSpec((bsz, d), lambda j: (0, 0)),
                  pl.BlockSpec((d, tn), lambda j: (0, jnp.minimum(j, last))),
                  pl.BlockSpec((1, tn), lambda j: (0, jnp.minimum(j, last))),
                  pl.BlockSpec(lam_qk.shape, lambda j: (0, 0)),
                  pl.BlockSpec((d, v_w), lambda j: (0, j))],
        out_specs=[pl.BlockSpec((bsz, tn), lambda j: (0, jnp.minimum(j, last))),
                   pl.BlockSpec((1, LANES), lambda j: (0, 0)),
                   pl.BlockSpec((d, v_w), lambda j: (0, j)),
                   pl.BlockSpec((v_w, d), lambda j: (0, 0))],
        out_shape=[jax.ShapeDtypeStruct((bsz, n), _F32), jax.ShapeDtypeStruct((1, LANES), _F32),
                   jax.ShapeDtypeStruct((d, n_in), _BF), jax.ShapeDtypeStruct((v_w, d), _BF)],
        compiler_params=pltpu.CompilerParams(
            dimension_semantics=("arbitrary",), vmem_limit_bytes=VMEM_LIMIT_BYTES),
        name="adaln_mod",
    )(c, w_ada, b_ada.reshape(1, n), lam_qk, w_in)


def _inproj_kernel(x_ref, mod_ref, gpre_ref, w_ref, wvt_ref, pos_ref, freq_ref, spread_ref,
                   qt_ref, k_ref, vt_ref, gates_ref, u_ref, *, d, qk_w, v_w, u_off, u_w):
    shift = mod_ref[0, :, 0:d]
    scale = mod_ref[0, :, d:2 * d]
    x = x_ref[0]
    inv = lax.rsqrt(jnp.mean(x * x, axis=-1, keepdims=True) + EPS)
    h = ((x * inv) * (gpre_ref[...] * (1.0 + scale)) + shift).astype(_BF)

    rows = x.shape[0]
    half = ROT_DIM // 2
    ang = freq_ref[...] * pos_ref[0]
    cs = jnp.concatenate([jnp.cos(ang), jnp.sin(ang)], axis=0)
    tab = jnp.dot(cs.T, spread_ref[...], precision=_HI, preferred_element_type=_F32)
    lane = lax.broadcasted_iota(jnp.int32, (1, LANES), 1)
    reps = qk_w // LANES
    rot_c = jnp.concatenate([jnp.where(lane % HEAD_DIM < ROT_DIM, tab[:, 0:LANES], 1.0)] * reps,
                            axis=1)
    rot_s = jnp.concatenate([tab[:, LANES:2 * LANES]] * reps, axis=1)
    low = lax.broadcasted_iota(jnp.int32, (1, qk_w), 1) % HEAD_DIM < half

    def rope(t):
        up = pltpu.roll(t, qk_w - half, 1)
        dn = pltpu.roll(t, half, 1)
        return t * rot_c + jnp.where(low, up, dn) * rot_s

    v_off = 2 * qk_w
    vt = _nt_dot(wvt_ref[...], h).astype(_BF)
    pad = (lax.broadcasted_iota(jnp.int32, (VT_PAD, KEY_BLOCK), 0) == 0).astype(_BF)
    for hd in range(v_w // V_HEAD_DIM):
        for c in range(rows // KEY_BLOCK):
            vt_ref[0, hd, c, 0:V_HEAD_DIM, :] = vt[hd * V_HEAD_DIM:(hd + 1) * V_HEAD_DIM,
                                                   c * KEY_BLOCK:(c + 1) * KEY_BLOCK]
            vt_ref[0, hd, c, V_HEAD_DIM:V_HEAD_DIM + VT_PAD, :] = pad
    u_ref[0] = _dot(h, w_ref[:, u_off:u_off + u_w]).astype(_BF)
    qt = rope(_dot(h, w_ref[:, 0:qk_w]) * (HEAD_DIM ** -0.5 * LOG2_E)).T.astype(_BF)
    hw = 2 * HEAD_DIM
    for hd in range(qk_w // hw):
        qt_ref[0, hd] = qt[hd * hw:(hd + 1) * hw, :]
    k_ref[0] = rope(_dot(h, w_ref[:, qk_w:2 * qk_w])).astype(_BF)
    za_off = v_off + v_w
    z_a = _dot(h, w_ref[:, za_off:za_off + v_w])
    gates_ref[0, :, 0:v_w] = (z_a * _sigmoid(z_a)).astype(_BF)
    zs_off = u_off + u_w
    z_s = _dot(h, w_ref[:, zs_off:zs_off + u_w])
    gates_ref[0, :, v_w:v_w + u_w] = (z_s * _sigmoid(z_s)).astype(_BF)
    g_off = zs_off + u_w
    rest = w_ref.shape[1] - g_off
    piece = 512
    for p in range(rest // piece):
        gates_ref[0, :, v_w + u_w + p * piece:v_w + u_w + (p + 1) * piece] = _sigmoid(_dot(
            h, w_ref[:, g_off + p * piece:g_off + (p + 1) * piece])).astype(_BF)


def _inproj(x, mod3, g_pre, w_in_bf, wvt_bf, pos_rows, *, rows, qk_w, v_w, u_off, u_w):
    bsz, s, d = x.shape
    n_in = w_in_bf.shape[1]
    gates_w = n_in - 2 * qk_w - v_w - u_w
    n_heads = v_w // V_HEAD_DIM
    vt_rows = V_HEAD_DIM + VT_PAD
    kern = functools.partial(_inproj_kernel, d=d, qk_w=qk_w, v_w=v_w, u_off=u_off, u_w=u_w)
    freq, spread = _rope_constants()
    seq_spec = lambda w: pl.BlockSpec((1, rows, w), lambda b, t: (b, t, 0))
    return pl.pallas_call(
        kern,
        grid=(bsz, s // rows),
        in_specs=[seq_spec(d),
                  pl.BlockSpec((1, 1, 3 * d), lambda b, t: (b, 0, 0)),
                  pl.BlockSpec((1, d), lambda b, t: (0, 0)),
                  pl.BlockSpec((d, n_in), lambda b, t: (0, 0), pipeline_mode=pl.Buffered(1)),
                  pl.BlockSpec((v_w, d), lambda b, t: (0, 0), pipeline_mode=pl.Buffered(1)),
                  pl.BlockSpec((1, 1, rows), lambda b, t: (b * (s // rows) + t, 0, 0)),
                  pl.BlockSpec(freq.shape, lambda b, t: (0, 0)),
                  pl.BlockSpec(spread.shape, lambda b, t: (0, 0))],
        out_specs=[pl.BlockSpec((1, n_heads, 2 * HEAD_DIM, rows), lambda b, t: (b, 0, 0, t)),
                   seq_spec(qk_w),
                   pl.BlockSpec((1, n_heads, rows // KEY_BLOCK, vt_rows, KEY_BLOCK),
                                lambda b, t: (b, 0, t, 0, 0)),
                   seq_spec(gates_w), seq_spec(u_w)],
        out_shape=[jax.ShapeDtypeStruct((bsz, n_heads, 2 * HEAD_DIM, s), _BF),
                   jax.ShapeDtypeStruct((bsz, s, qk_w), _BF),
                   jax.ShapeDtypeStruct((bsz, n_heads, s // KEY_BLOCK, vt_rows, KEY_BLOCK), _BF),
                   jax.ShapeDtypeStruct((bsz, s, gates_w), _BF),
                   jax.ShapeDtypeStruct((bsz, s, u_w), _BF)],
        compiler_params=pltpu.CompilerParams(
            dimension_semantics=("arbitrary", "arbitrary"), vmem_limit_bytes=VMEM_LIMIT_BYTES),
        name="norm_inproj",
    )(x, mod3, g_pre.reshape(1, d), w_in_bf, wvt_bf, pos_rows, freq, spread)


def _attn_kernel(lam_ref, gsub_ref, qt_ref, k_ref, vt_ref, o_ref, *scratch, lam_init):
    s_ref = (scratch[0:2], scratch[2:4])
    e_ref = (scratch[4:6], scratch[6:8])
    m_ref = (scratch[8:10], scratch[10:12])
    ev_ref = (scratch[12:14], scratch[14:16])
    n_kb, tk, tq = s_ref[0][0].shape
    step = pl.program_id(0)

    @pl.when(step == 0)
    def _():
        for ref in scratch:
            ref[...] = jnp.zeros_like(ref)

    lam = lam_ref[0:1, 0:1]

    def tick(new, old):
        ot = None
        for mp in range(2):
            ev = ev_ref[old][mp][...]
            nrm = ev[0:V_HEAD_DIM, :] * (1.0 / ev[V_HEAD_DIM:V_HEAD_DIM + 1, :])
            ot = nrm if ot is None else ot - lam * nrm
        inv = lax.rsqrt(jnp.mean(ot * ot, axis=0, keepdims=True) + EPS)
        o_ref[0] = (((ot * inv) * gsub_ref[...]) * (1.0 - lam_init)).T.astype(_BF)
        qt = qt_ref[0, 0]
        first = lax.broadcasted_iota(jnp.int32, (2 * HEAD_DIM, 1), 0) < HEAD_DIM
        zero = jnp.zeros_like(qt)
        qmt = (jnp.where(first, qt, zero), jnp.where(first, zero, qt))
        m_rows = [jnp.max(m_ref[old][mp][...], axis=0, keepdims=True) for mp in range(2)]
        for mp in range(2):
            ev = None
            m = jnp.full((8, tq), -jnp.inf, _F32)
            for kb in range(n_kb):
                st = _dot(k_ref[0, kb * tk:(kb + 1) * tk, :], qmt[mp])
                s_ref[new][mp][kb] = st
                m = jnp.maximum(m, jnp.max(st.reshape(tk // 8, 8, tq), axis=0))
                part = _dot(vt_ref[0, 0, kb], e_ref[new][mp][kb])
                ev = part if ev is None else ev + part
                e_ref[old][mp][kb] = jnp.exp2(s_ref[old][mp][kb] - m_rows[mp]).astype(_BF)
            m_ref[new][mp][...] = m
            ev_ref[new][mp][...] = ev

    @pl.when(step % 2 == 0)
    def _():
        tick(0, 1)

    @pl.when(step % 2 == 1)
    def _():
        tick(1, 0)


def _attention(lam, g_subln, qt, k, vt, *, tq, lam_init):
    bsz, n_heads, hw, s = qt.shape
    _, _, n_kb, vt_rows, tk = vt.shape
    n_q = s // tq
    n_items = bsz * n_heads * n_q
    kern = functools.partial(_attn_kernel, lam_init=lam_init)

    def item(j):
        j = jnp.clip(j, 0, n_items - 1)
        return j // (n_heads * n_q), (j // n_q) % n_heads, j % n_q

    def qt_map(g):
        b, h, i = item(g)
        return b, h, 0, i

    def k_map(g):
        b, h, _ = item(g)
        return b, 0, h

    def vt_map(g):
        b, h, _ = item(g - 2)
        return b, h, 0, 0, 0

    def o_map(g):
        b, h, i = item(g - 3)
        return b, i, h

    return pl.pallas_call(
        kern,
        grid=(n_items + 3,),
        in_specs=[pl.BlockSpec(lam.shape, lambda g: (0, 0)),
                  pl.BlockSpec((V_HEAD_DIM, 1), lambda g: (0, 0)),
                  pl.BlockSpec((1, 1, hw, tq), qt_map),
                  pl.BlockSpec((1, s, hw), k_map),
                  pl.BlockSpec((1, 1, n_kb, vt_rows, tk), vt_map)],
        out_specs=pl.BlockSpec((1, tq, V_HEAD_DIM), o_map),
        out_shape=jax.ShapeDtypeStruct((bsz, s, n_heads * V_HEAD_DIM), _BF),
        scratch_shapes=([pltpu.VMEM((n_kb, tk, tq), _F32)] * 4 + [pltpu.VMEM((n_kb, tk, tq), _BF)] * 4
                        + [pltpu.VMEM((8, tq), _F32)] * 4 + [pltpu.VMEM((vt_rows, tq), _F32)] * 4),
        compiler_params=pltpu.CompilerParams(
            dimension_semantics=("arbitrary",), vmem_limit_bytes=VMEM_LIMIT_BYTES),
        name="diff_attention",
    )(lam, g_subln.reshape(V_HEAD_DIM, 1), qt, k, vt)


def _ssm_kernel(ut_ref, lam_ref, bt_ref, ct_ref, dsk_ref, yt_ref, at_ref, z_ref, xp_ref,
                *, bsz, n_c):
    el = SSM_CHUNK
    cw = el * SSM_GROUP
    sw = 2 * SSM_STATE
    n_cols = n_c * bsz
    in_group = [lax.broadcasted_iota(jnp.int32, (1, sw), 1) // SSM_STATE == gl for gl in range(2)]

    def group_rows(ref, gl):
        return ref[:, gl * SSM_GROUP:(gl + 1) * SSM_GROUP, :].reshape(cw, n_cols)

    def per_row(rows, cols):
        return (jnp.broadcast_to(rows[:, None, :], (el, SSM_GROUP, sw)),
                jnp.broadcast_to(cols[None, :, :], (el, SSM_GROUP, sw)))

    def cmul_rows(p_r, p_i, m_r, m_i):
        pr3, mr3 = per_row(p_r, m_r)
        pi3, mi3 = per_row(p_i, m_i)
        return (pr3 * mr3 - pi3 * mi3).reshape(cw, sw), (pr3 * mi3 + pi3 * mr3).reshape(cw, sw)

    idx = lax.broadcasted_iota(jnp.int32, (el, 1), 0).astype(_F32)
    ops = []
    for e in range(2):
        lr = jnp.minimum(lam_ref[0, e, 0:1, :], -1e-4)
        li = lam_ref[0, e, 1:2, :]
        dt = jnp.exp(lam_ref[0, e, 2:3, :])

        def power(k, lr=lr, li=li, dt=dt):
            mag = jnp.exp(k * (lr * dt))
            return mag * jnp.cos(k * (li * dt)), mag * jnp.sin(k * (li * dt))

        one_r, one_i = power(1.0)
        den = lr * lr + li * li
        nr, ni = one_r - 1.0, one_i
        coef_r = (nr * lr + ni * li) / den
        coef_i = (ni * lr - nr * li) / den
        b_r, b_i = bt_ref[0, e, 0], bt_ref[0, e, 1]
        bb_r = coef_r * b_r - coef_i * b_i
        bb_i = coef_r * b_i + coef_i * b_r
        c_r, c_i = ct_ref[0, e, 0], ct_ref[0, e, 1]
        ops.append(dict(power=power, bb=(bb_r, bb_i), c=(c_r, c_i)))
    fwd, bwd = ops

    zf_r, zf_i = cmul_rows(*fwd["power"](el - 1.0 - idx), *fwd["bb"])
    zb_r, zb_i = cmul_rows(*bwd["power"](idx), *bwd["bb"])
    yf_r, yf_i = cmul_rows(*fwd["power"](idx + 1.0), *fwd["c"])
    yb_r, yb_i = cmul_rows(*bwd["power"](el - idx), *bwd["c"])
    kf_r, kf_i = cmul_rows(*fwd["power"](idx), *fwd["c"])
    kb_r, kb_i = cmul_rows(*bwd["power"](idx), *bwd["c"])

    def pair_slot(parts, gl):
        return jnp.concatenate([jnp.where(in_group[gl], p, 0.0) for p in parts], axis=1).astype(_BF)

    for gl in range(2):
        at_ref[gl] = group_rows(ut_ref, gl).T
    z = (_dot(at_ref[0], pair_slot([zf_r, zf_i, zb_r, zb_i], 0))
         + _dot(at_ref[1], pair_slot([zf_r, zf_i, zb_r, zb_i], 1)))
    for comp in range(4):
        z_ref[comp] = z[:, comp * sw:(comp + 1) * sw]

    a_fr, a_fi = fwd["power"](float(el))
    a_br, a_bi = bwd["power"](float(el))

    def step(c, carry):
        fr, fi, br, bi = carry
        rows_f = pl.ds(pl.multiple_of(c * bsz, bsz), bsz)
        rows_b = pl.ds(pl.multiple_of((n_c - 1 - c) * bsz, bsz), bsz)
        xp_ref[0, rows_f, :] = fr
        xp_ref[1, rows_f, :] = fi
        xp_ref[2, rows_b, :] = br
        xp_ref[3, rows_b, :] = bi
        zfr = z_ref[0, rows_f, :]
        zfi = z_ref[1, rows_f, :]
        zbr = z_ref[2, rows_b, :]
        zbi = z_ref[3, rows_b, :]
        return (a_fr * fr - a_fi * fi + zfr, a_fr * fi + a_fi * fr + zfi,
                a_br * br - a_bi * bi + zbr, a_br * bi + a_bi * br + zbi)

    zero = jnp.zeros((bsz, sw), _F32)
    lax.fori_loop(0, n_c, step, (zero, zero, zero, zero), unroll=4)
    xp = jnp.concatenate([xp_ref[comp] for comp in range(4)], axis=1).astype(_BF)

    spread = (lax.broadcasted_iota(jnp.int32, (SSM_GROUP, cw), 1) % SSM_GROUP
              == lax.broadcasted_iota(jnp.int32, (SSM_GROUP, cw), 0)).astype(_BF)
    lag = (lax.broadcasted_iota(jnp.int32, (cw, cw), 0) // SSM_GROUP
           - lax.broadcasted_iota(jnp.int32, (cw, cw), 1) // SSM_GROUP)

    def lag_kernels(k_r, k_i, bb, gl):
        k_r, k_i = jnp.where(in_group[gl], k_r, 0.0), jnp.where(in_group[gl], k_i, 0.0)
        hi_dot = functools.partial(lax.dot_general, dimension_numbers=(((1,), (1,)), ((), ())),
                                   precision=_HI, preferred_element_type=_F32)
        return hi_dot(k_r, bb[0]) - hi_dot(k_i, bb[1])

    for gl in range(2):
        ktf = _dot(lag_kernels(kf_r, kf_i, fwd["bb"], gl).astype(_BF), spread)
        ktb = _dot(lag_kernels(kb_r, kb_i, bwd["bb"], gl).astype(_BF), spread)

        def lag_rows(kt, k):
            blk = kt[k * SSM_GROUP:(k + 1) * SSM_GROUP, :]
            return jnp.broadcast_to(blk[None], (el, SSM_GROUP, cw)).reshape(cw, cw)

        tm = jnp.where(lag == 0, lag_rows(ktf, 0) + lag_rows(ktb, 0), 0.0)
        for k in range(1, el):
            tm = jnp.where(lag == k, lag_rows(ktf, k), tm)
            tm = jnp.where(lag == -k, lag_rows(ktb, k), tm)
        u_g = group_rows(ut_ref, gl)
        yt = _dot(tm.astype(_BF), u_g) + _nt_dot(pair_slot([yf_r, -yf_i, yb_r, -yb_i], gl), xp)
        yt = yt + dsk_ref[0, gl] * u_g.astype(_F32)
        yt_ref[:, gl * SSM_GROUP:(gl + 1) * SSM_GROUP, :] = yt.reshape(
            el, SSM_GROUP, n_cols).astype(_BF)


def _ssm(ut, lam, bt, ct, dsk, *, bsz):
    _, u_w, n_cols = ut.shape
    n_c = n_cols // bsz
    n_pairs = u_w // (2 * SSM_GROUP)
    cw = SSM_CHUNK * SSM_GROUP
    sw = 2 * SSM_STATE
    kern = functools.partial(_ssm_kernel, bsz=bsz, n_c=n_c)
    blk = pl.BlockSpec((SSM_CHUNK, 2 * SSM_GROUP, n_cols), lambda g: (0, g, 0))
    par = pl.BlockSpec((1, 2, 2, SSM_GROUP, sw), lambda g: (g, 0, 0, 0, 0))
    return pl.pallas_call(
        kern,
        grid=(n_pairs,),
        in_specs=[blk,
                  pl.BlockSpec((1, 2, 3, sw), lambda g: (g, 0, 0, 0)),
                  par, par,
                  pl.BlockSpec((1, 2, cw, 1), lambda g: (g, 0, 0, 0))],
        out_specs=blk,
        out_shape=jax.ShapeDtypeStruct(ut.shape, _BF),
        scratch_shapes=[pltpu.VMEM((2, n_cols, cw), _BF),
                        pltpu.VMEM((4, n_cols, sw), _F32),
                        pltpu.VMEM((4, n_cols, sw), _F32)],
        compiler_params=pltpu.CompilerParams(
            dimension_semantics=("arbitrary",), vmem_limit_bytes=VMEM_LIMIT_BYTES),
        name="s5_chunked",
    )(ut, lam, bt, ct, dsk)


def _post_kernel(o_ref, gates_ref, y_ref, x_ref, mod_ref, bglu_ref, gfin_ref,
                 wua_ref, wglu_ref, wus_ref, wout_ref, out_ref, *, d, aw, sw):
    y = jax.nn.gelu(y_ref[0].astype(_F32))
    y = y * _sigmoid(_dot(y.astype(_BF), wglu_ref[...]) + bglu_ref[...])
    s_br = _dot((y * gates_ref[0, :, aw:aw + sw].astype(_F32)).astype(_BF), wus_ref[...])
    a_br = _dot((o_ref[0].astype(_F32) * gates_ref[0, :, 0:aw].astype(_F32)).astype(_BF),
                wua_ref[...])
    g_off = aw + sw
    merged = (gates_ref[0, :, g_off:g_off + d].astype(_F32) * a_br
              + gates_ref[0, :, g_off + d:g_off + 2 * d].astype(_F32) * s_br)
    r = _dot(merged.astype(_BF), wout_ref[...])
    xo = x_ref[0] + mod_ref[0, :, 2 * d:3 * d] * r
    inv = lax.rsqrt(jnp.mean(xo * xo, axis=-1, keepdims=True) + EPS)
    out_ref[0] = (xo * inv) * gfin_ref[...]


def _post(o, gates, y, x, mod3, b_glu, g_final, wua, wglu, wus, wout, *, rows):
    bsz, s, d = x.shape
    aw = o.shape[2]
    sw = y.shape[2]
    kern = functools.partial(_post_kernel, d=d, aw=aw, sw=sw)
    full = lambda a: pl.BlockSpec(a.shape, lambda b, t: (0,) * a.ndim)
    seq_spec = lambda w: pl.BlockSpec((1, rows, w), lambda b, t: (b, t, 0))
    bgl = b_glu.reshape(1, sw)
    gfi = g_final.reshape(1, d)
    return pl.pallas_call(
        kern,
        grid=(bsz, s // rows),
        in_specs=[seq_spec(aw), seq_spec(gates.shape[2]), seq_spec(sw), seq_spec(d),
                  pl.BlockSpec((1, 1, 3 * d), lambda b, t: (b, 0, 0)),
                  full(bgl), full(gfi), full(wua), full(wglu), full(wus), full(wout)],
        out_specs=seq_spec(d),
        out_shape=jax.ShapeDtypeStruct(x.shape, _F32),
        compiler_params=pltpu.CompilerParams(
            dimension_semantics=("arbitrary", "arbitrary"), vmem_limit_bytes=VMEM_LIMIT_BYTES),
        name="merge_out",
    )(o, gates, y, x, mod3, bgl, gfi, wua, wglu, wus, wout)


def _rope_constants():
    half = ROT_DIM // 2
    freq = (ROPE_THETA ** (-jnp.arange(half, dtype=_F32) * 2.0 / ROT_DIM)).reshape(half, 1)
    in_head = jnp.arange(LANES) % HEAD_DIM
    j = jnp.arange(half)[:, None]
    cos_sel = ((in_head == j) | (in_head == half + j)).astype(_F32)
    sin_sel = (in_head == half + j).astype(_F32) - (in_head == j).astype(_F32)
    zero = jnp.zeros_like(cos_sel)
    return freq, jnp.concatenate([jnp.concatenate([cos_sel, zero], axis=1),
                                  jnp.concatenate([zero, sin_sel], axis=1)], axis=0)


def _ssm_params(lam_re, lam_im, log_dt, b_re, b_im, c_re, c_im):
    n_g, n_p = lam_re.shape[1], lam_re.shape[2]
    pairs = n_g // 2

    def lanes(m):
        rows = m.shape[2]
        return m.reshape(2, pairs, 2, rows, n_p).transpose(1, 0, 3, 2, 4).reshape(pairs, 2, rows, 2 * n_p)

    lam = lanes(jnp.stack([lam_re, lam_im, jnp.broadcast_to(log_dt[..., None], lam_re.shape)],
                          axis=2).astype(_F32))
    bt = jnp.stack([lanes(jnp.swapaxes(b_re, -1, -2)), lanes(jnp.swapaxes(b_im, -1, -2))], axis=2)
    ct = jnp.stack([lanes(c_re), lanes(c_im)], axis=2)
    return lam, bt.astype(_F32), ct.astype(_F32)


def _layer(x, c, positions, layer_idx, w_ada, b_ada, g_pre, w_in, lam_qk, g_subln,
           lam_re, lam_im, log_dt, b_re, b_im, c_re, c_im, d_skip,
           w_glu, b_glu, w_up_attn, w_up_ssm, w_out, g_final):
    bsz, s, d = x.shape
    n_c = s // SSM_CHUNK
    qk_w = ATTN_HEADS * 2 * HEAD_DIM
    v_w = ATTN_HEADS * V_HEAD_DIM
    u_w = lam_re.shape[1] * SSM_GROUP
    u_off = 2 * qk_w + 2 * v_w
    lam_init = 0.8 - 0.6 * math.exp(-0.3 * layer_idx)
    mod, lam, w_in_bf, wvt_bf = _prepare(c, w_ada, b_ada, lam_qk, w_in, lam_init=lam_init,
                                         v_off=2 * qk_w, v_w=v_w)
    mod3 = mod.reshape(bsz, 1, 3 * d)
    rows = min(ROW_BLOCK, s)
    pos_rows = positions.astype(_F32).reshape(bsz * (s // rows), 1, rows)
    qt, k, vt, gates, u = _inproj(x, mod3, g_pre, w_in_bf, wvt_bf, pos_rows, rows=rows,
                                  qk_w=qk_w, v_w=v_w, u_off=u_off, u_w=u_w)
    o = _attention(lam, g_subln, qt, k, vt, tq=min(Q_ROWS, s), lam_init=lam_init)
    ut = u.reshape(bsz, n_c, SSM_CHUNK, u_w).transpose(2, 3, 1, 0).reshape(SSM_CHUNK, u_w, n_c * bsz)
    dsk = jnp.broadcast_to(d_skip.reshape(u_w // (2 * SSM_GROUP), 2, 1, SSM_GROUP, 1),
                           (u_w // (2 * SSM_GROUP), 2, SSM_CHUNK, SSM_GROUP, 1)).reshape(
                               u_w // (2 * SSM_GROUP), 2, SSM_CHUNK * SSM_GROUP, 1)
    yt = _ssm(ut, *_ssm_params(lam_re, lam_im, log_dt, b_re, b_im, c_re, c_im), dsk, bsz=bsz)
    y = yt.reshape(SSM_CHUNK, u_w, n_c, bsz).transpose(3, 2, 0, 1).reshape(bsz, s, u_w)
    return _post(o, gates, y, x, mod3, b_glu, g_final,
                 w_up_attn.astype(_BF), w_glu.astype(_BF), w_up_ssm.astype(_BF), w_out.astype(_BF),
                 rows=rows)


def kernel(x, c, positions, w_ada, b_ada, g_pre, w_in, lam_qk, g_subln, ssm_lam_re, ssm_lam_im,
           ssm_log_dt, ssm_b_re, ssm_b_im, ssm_c_re, ssm_c_im, ssm_d, w_glu, b_glu, w_up_attn,
           w_up_ssm, w_out, g_final):
    depth = w_ada.shape[0]
    assert depth == 1, "the final RMSNorm is fused into the single layer's epilogue"
    assert x.shape[1] % (SSM_CHUNK * 8) == 0
    return _layer(x, c, positions, 0, w_ada[0], b_ada[0], g_pre[0], w_in[0],
                  lam_qk[0], g_subln[0], ssm_lam_re[0], ssm_lam_im[0], ssm_log_dt[0], ssm_b_re[0],
                  ssm_b_im[0], ssm_c_re[0], ssm_c_im[0], ssm_d[0], w_glu[0], b_glu[0],
                  w_up_attn[0], w_up_ssm[0], w_out[0], g_final)
```

```python
import functools
import math

import jax
import jax.numpy as jnp
import numpy as np
from jax import lax
from jax.experimental import pallas as pl
from jax.experimental.pallas import tpu as pltpu

ATTN_HEADS = 4
HEAD_DIM = 64
V_HEAD_DIM = 2 * HEAD_DIM
ROT_DIM = HEAD_DIM // 4
ROPE_THETA = 500000.0
SSM_GROUP = 16
SSM_STATE = 64
SSM_CHUNK = 16
EPS = 1e-6
LOG2_E = math.log2(math.e)
LANES = 128
VMEM_LIMIT_BYTES = 56 * 1024 * 1024
ROW_BLOCK = 1024
Q_ROWS = 512
KEY_BLOCK = 256
VT_PAD = 16

_HI = lax.Precision.HIGHEST
_BF = jnp.bfloat16
_F32 = jnp.float32


def _nt_dot(a, b):
    return lax.dot_general(a, b, (((1,), (1,)), ((), ())), preferred_element_type=_F32)


def _dot(a, b):
    return jnp.dot(a, b, preferred_element_type=_F32)


def _sigmoid(x):
    return 0.5 * jnp.tanh(0.5 * x) + 0.5


def _prep_kernel(c_ref, w_ref, b_ref, lamqk_ref, win_ref, o_ref, lam_ref, winb_ref, wvt_ref,
                 *, lam_init, n_mod, v_tile):
    j = pl.program_id(0)

    @pl.when(j < n_mod)
    def _():
        o_ref[...] = jnp.dot(jax.nn.silu(c_ref[...]), w_ref[...], precision=_HI,
                             preferred_element_type=_F32) + b_ref[...]

    @pl.when(j == 0)
    def _():
        lf = lamqk_ref[...]
        lam = (jnp.exp(jnp.sum(lf[0:1, :] * lf[1:2, :], axis=-1, keepdims=True))
               - jnp.exp(jnp.sum(lf[2:3, :] * lf[3:4, :], axis=-1, keepdims=True)) + lam_init)
        lam_ref[...] = jnp.broadcast_to(lam, lam_ref.shape)

    w = win_ref[...]
    winb_ref[...] = w.astype(_BF)

    @pl.when(j == v_tile)
    def _():
        wvt_ref[...] = w.T.astype(_BF)


def _prepare(c, w_ada, b_ada, lam_qk, w_in, *, lam_init, v_off, v_w):
    bsz, d = c.shape
    n = w_ada.shape[1]
    n_in = w_in.shape[1]
    tn = d
    n_mod = n // tn
    assert v_off % v_w == 0 and n_in % v_w == 0 and n_in // v_w >= n_mod
    last = n_mod - 1
    return pl.pallas_call(
        functools.partial(_prep_kernel, lam_init=lam_init, n_mod=n_mod, v_tile=v_off // v_w),
        grid=(n_in // v_w,),
        in_specs=[pl.BlockSpec((bsz, d), lambda j: (0, 0)),
                  pl.BlockSpec((d, tn), lambda j: (0, jnp.minimum(j, last))),
                  pl.BlockSpec((1, tn), lambda j: (0, jnp.minimum(j, last))),
                  pl.BlockSpec(lam_qk.shape, lambda j: (0, 0)),
                  pl.BlockSpec((d, v_w), lambda j: (0, j))],
        out_specs=[pl.BlockSpec((bsz, tn), lambda j: (0, jnp.minimum(j, last))),
                   pl.BlockSpec((1, LANES), lambda j: (0, 0)),
                   pl.BlockSpec((d, v_w), lambda j: (0, j)),
                   pl.BlockSpec((v_w, d), lambda j: (0, 0))],
        out_shape=[jax.ShapeDtypeStruct((bsz, n), _F32), jax.ShapeDtypeStruct((1, LANES), _F32),
                   jax.ShapeDtypeStruct((d, n_in), _BF), jax.ShapeDtypeStruct((v_w, d), _BF)],
        compiler_params=pltpu.CompilerParams(
            dimension_semantics=("arbitrary",), vmem_limit_bytes=VMEM_LIMIT_BYTES),
        name="adaln_mod",
    )(c, w_ada, b_ada.reshape(1, n), lam_qk, w_in)


def _inproj_kernel(x_ref, mod_ref, gpre_ref, w_ref, wvt_ref, pos_ref, freq_ref, spread_ref,
                   qt_ref, k_ref, vt_ref, gates_ref, u_ref, *, d, qk_w, v_w, u_off, u_w):
    shift = mod_ref[0, :, 0:d]
    scale = mod_ref[0, :, d:2 * d]
    x = x_ref[0]
    inv = lax.rsqrt(jnp.mean(x * x, axis=-1, keepdims=True) + EPS)
    h = ((x * inv) * (gpre_ref[...] * (1.0 + scale)) + shift).astype(_BF)

    rows = x.shape[0]
    half = ROT_DIM // 2
    ang = freq_ref[...] * pos_ref[0]
    cs = jnp.concatenate([jnp.cos(ang), jnp.sin(ang)], axis=0)
    tab = jnp.dot(cs.T, spread_ref[...], precision=_HI, preferred_element_type=_F32)
    lane = lax.broadcasted_iota(jnp.int32, (1, LANES), 1)
    reps = qk_w // LANES
    rot_c = jnp.concatenate([jnp.where(lane % HEAD_DIM < ROT_DIM, tab[:, 0:LANES], 1.0)] * reps,
                            axis=1)
    rot_s = jnp.concatenate([tab[:, LANES:2 * LANES]] * reps, axis=1)
    low = lax.broadcasted_iota(jnp.int32, (1, qk_w), 1) % HEAD_DIM < half

    def rope(t):
        up = pltpu.roll(t, qk_w - half, 1)
        dn = pltpu.roll(t, half, 1)
        return t * rot_c + jnp.where(low, up, dn) * rot_s

    v_off = 2 * qk_w
    vt = _nt_dot(wvt_ref[...], h).astype(_BF)
    pad = (lax.broadcasted_iota(jnp.int32, (VT_PAD, rows), 0) == 0).astype(_BF)
    for hd in range(v_w // V_HEAD_DIM):
        vt_ref[0, hd, 0:V_HEAD_DIM, :] = vt[hd * V_HEAD_DIM:(hd + 1) * V_HEAD_DIM, :]
        vt_ref[0, hd, V_HEAD_DIM:V_HEAD_DIM + VT_PAD, :] = pad
    u_ref[0] = _dot(h, w_ref[:, u_off:u_off + u_w]).astype(_BF)
    qt = rope(_dot(h, w_ref[:, 0:qk_w]) * (HEAD_DIM ** -0.5 * LOG2_E)).T.astype(_BF)
    hw = 2 * HEAD_DIM
    for hd in range(qk_w // hw):
        qt_ref[0, hd] = qt[hd * hw:(hd + 1) * hw, :]
    k_ref[0] = rope(_dot(h, w_ref[:, qk_w:2 * qk_w])).astype(_BF)
    za_off = v_off + v_w
    z_a = _dot(h, w_ref[:, za_off:za_off + v_w])
    gates_ref[0, :, 0:v_w] = (z_a * _sigmoid(z_a)).astype(_BF)
    zs_off = u_off + u_w
    z_s = _dot(h, w_ref[:, zs_off:zs_off + u_w])
    gates_ref[0, :, v_w:v_w + u_w] = (z_s * _sigmoid(z_s)).astype(_BF)
    g_off = zs_off + u_w
    rest = w_ref.shape[1] - g_off
    piece = 512
    for p in range(rest // piece):
        gates_ref[0, :, v_w + u_w + p * piece:v_w + u_w + (p + 1) * piece] = _sigmoid(_dot(
            h, w_ref[:, g_off + p * piece:g_off + (p + 1) * piece])).astype(_BF)


def _inproj(x, mod3, g_pre, w_in_bf, wvt_bf, pos_rows, *, rows, qk_w, v_w, u_off, u_w):
    bsz, s, d = x.shape
    n_in = w_in_bf.shape[1]
    gates_w = n_in - 2 * qk_w - v_w - u_w
    n_heads = v_w // V_HEAD_DIM
    vt_rows = V_HEAD_DIM + VT_PAD
    kern = functools.partial(_inproj_kernel, d=d, qk_w=qk_w, v_w=v_w, u_off=u_off, u_w=u_w)
    freq, spread = _rope_constants()
    seq_spec = lambda w: pl.BlockSpec((1, rows, w), lambda b, t: (b, t, 0))
    return pl.pallas_call(
        kern,
        grid=(bsz, s // rows),
        in_specs=[seq_spec(d),
                  pl.BlockSpec((1, 1, 3 * d), lambda b, t: (b, 0, 0)),
                  pl.BlockSpec((1, d), lambda b, t: (0, 0)),
                  pl.BlockSpec((d, n_in), lambda b, t: (0, 0), pipeline_mode=pl.Buffered(1)),
                  pl.BlockSpec((v_w, d), lambda b, t: (0, 0), pipeline_mode=pl.Buffered(1)),
                  pl.BlockSpec((1, 1, rows), lambda b, t: (b * (s // rows) + t, 0, 0)),
                  pl.BlockSpec(freq.shape, lambda b, t: (0, 0)),
                  pl.BlockSpec(spread.shape, lambda b, t: (0, 0))],
        out_specs=[pl.BlockSpec((1, n_heads, 2 * HEAD_DIM, rows), lambda b, t: (b, 0, 0, t)),
                   seq_spec(qk_w),
                   pl.BlockSpec((1, n_heads, vt_rows, rows), lambda b, t: (b, 0, 0, t)),
                   seq_spec(gates_w), seq_spec(u_w)],
        out_shape=[jax.ShapeDtypeStruct((bsz, n_heads, 2 * HEAD_DIM, s), _BF),
                   jax.ShapeDtypeStruct((bsz, s, qk_w), _BF),
                   jax.ShapeDtypeStruct((bsz, n_heads, vt_rows, s), _BF),
                   jax.ShapeDtypeStruct((bsz, s, gates_w), _BF),
                   jax.ShapeDtypeStruct((bsz, s, u_w), _BF)],
        compiler_params=pltpu.CompilerParams(
            dimension_semantics=("arbitrary", "arbitrary"), vmem_limit_bytes=VMEM_LIMIT_BYTES),
        name="norm_inproj",
    )(x, mod3, g_pre.reshape(1, d), w_in_bf, wvt_bf, pos_rows, freq, spread)


def _attn_kernel(tab_ref, lam_ref, gsub_ref, qt_ref, k_ref, vt_ref, o_ref, *scratch, lam_init):
    s_ref = (scratch[0:2], scratch[2:4])
    e_ref = (scratch[4:6], scratch[6:8])
    m_ref = (scratch[8:10], scratch[10:12])
    ev_ref = (scratch[12:14], scratch[14:16])
    n_keys, tq = s_ref[0][0].shape
    tk = KEY_BLOCK
    n_kb = n_keys // tk
    step = pl.program_id(0)

    @pl.when(step == 0)
    def _():
        for ref in scratch:
            ref[...] = jnp.zeros_like(ref)

    lam = lam_ref[0:1, 0:1]

    def tick(new, old):
        ot = None
        for mp in range(2):
            ev = ev_ref[old][mp][...]
            nrm = ev[0:V_HEAD_DIM, :] * (1.0 / ev[V_HEAD_DIM:V_HEAD_DIM + 1, :])
            ot = nrm if ot is None else ot - lam * nrm
        inv = lax.rsqrt(jnp.mean(ot * ot, axis=0, keepdims=True) + EPS)
        o_ref[0] = (((ot * inv) * gsub_ref[...]) * (1.0 - lam_init)).T.astype(_BF)
        qt = qt_ref[0, 0]
        first = lax.broadcasted_iota(jnp.int32, (2 * HEAD_DIM, 1), 0) < HEAD_DIM
        zero = jnp.zeros_like(qt)
        qmt = (jnp.where(first, qt, zero), jnp.where(first, zero, qt))
        m_rows = [jnp.max(m_ref[old][mp][...], axis=0, keepdims=True) for mp in range(2)]
        n_part = 4 if n_kb % 4 == 0 else 1
        for mp in range(2):
            ev = None
            m = jnp.full((8, tq), -jnp.inf, _F32)
            for kb in range(n_kb):
                keys = slice(kb * tk, (kb + 1) * tk)
                if kb % (n_kb // n_part) == 0:
                    part_keys = slice(kb * tk, (kb + n_kb // n_part) * tk)
                    part = _dot(vt_ref[0, 0, :, part_keys], e_ref[new][mp][part_keys, :])
                    ev = part if ev is None else ev + part
                st = _dot(k_ref[0, keys, :], qmt[mp])
                s_ref[new][mp][keys, :] = st
                m = jnp.maximum(m, jnp.max(st.reshape(tk // 8, 8, tq), axis=0))
                e_ref[old][mp][keys, :] = jnp.exp2(s_ref[old][mp][keys, :] - m_rows[mp]).astype(_BF)
            m_ref[new][mp][...] = m
            ev_ref[new][mp][...] = ev

    @pl.when(step % 2 == 0)
    def _():
        tick(0, 1)

    @pl.when(step % 2 == 1)
    def _():
        tick(1, 0)


def _attention(lam, g_subln, qt, k, vt, *, tq, lam_init):
    bsz, n_heads, hw, s = qt.shape
    _, _, vt_rows, _ = vt.shape
    n_q = s // tq
    n_items = bsz * n_heads * n_q
    kern = functools.partial(_attn_kernel, lam_init=lam_init)
    n_steps = n_items + 3

    def item(j):
        j = np.clip(j, 0, n_items - 1)
        return j // (n_heads * n_q), (j // n_q) % n_heads, j % n_q

    g = np.arange(n_steps)
    tab = jnp.asarray(np.stack([*item(g), *item(g - 2), *item(g - 3)]).astype(np.int32))

    return pl.pallas_call(
        kern,
        grid_spec=pltpu.PrefetchScalarGridSpec(
            num_scalar_prefetch=1,
            grid=(n_steps,),
            in_specs=[pl.BlockSpec(lam.shape, lambda g, t: (0, 0)),
                      pl.BlockSpec((V_HEAD_DIM, 1), lambda g, t: (0, 0)),
                      pl.BlockSpec((1, 1, hw, tq), lambda g, t: (t[0, g], t[1, g], 0, t[2, g])),
                      pl.BlockSpec((1, s, hw), lambda g, t: (t[0, g], 0, t[1, g])),
                      pl.BlockSpec((1, 1, vt_rows, s), lambda g, t: (t[3, g], t[4, g], 0, 0))],
            out_specs=pl.BlockSpec((1, tq, V_HEAD_DIM), lambda g, t: (t[6, g], t[8, g], t[7, g])),
            scratch_shapes=([pltpu.VMEM((s, tq), _F32)] * 4 + [pltpu.VMEM((s, tq), _BF)] * 4
                            + [pltpu.VMEM((8, tq), _F32)] * 4 + [pltpu.VMEM((vt_rows, tq), _F32)] * 4)),
        out_shape=jax.ShapeDtypeStruct((bsz, s, n_heads * V_HEAD_DIM), _BF),
        compiler_params=pltpu.CompilerParams(
            dimension_semantics=("arbitrary",), vmem_limit_bytes=VMEM_LIMIT_BYTES),
        name="diff_attention",
    )(tab, lam, g_subln.reshape(V_HEAD_DIM, 1), qt, k, vt)


def _ssm_kernel(ut_ref, lam_ref, bt_ref, ct_ref, dsk_ref, yt_ref, at_ref, z_ref, xp_ref,
                *, bsz, n_c):
    el = SSM_CHUNK
    cw = el * SSM_GROUP
    sw = 2 * SSM_STATE
    n_cols = n_c * bsz
    in_group = [lax.broadcasted_iota(jnp.int32, (1, sw), 1) // SSM_STATE == gl for gl in range(2)]

    def group_rows(ref, gl):
        return ref[:, gl * SSM_GROUP:(gl + 1) * SSM_GROUP, :].reshape(cw, n_cols)

    def per_row(rows, cols):
        return (jnp.broadcast_to(rows[:, None, :], (el, SSM_GROUP, sw)),
                jnp.broadcast_to(cols[None, :, :], (el, SSM_GROUP, sw)))

    def cmul_rows(p_r, p_i, m_r, m_i):
        pr3, mr3 = per_row(p_r, m_r)
        pi3, mi3 = per_row(p_i, m_i)
        return (pr3 * mr3 - pi3 * mi3).reshape(cw, sw), (pr3 * mi3 + pi3 * mr3).reshape(cw, sw)

    idx = lax.broadcasted_iota(jnp.int32, (el, 1), 0).astype(_F32)
    ops = []
    for e in range(2):
        lr = jnp.minimum(lam_ref[0, e, 0:1, :], -1e-4)
        li = lam_ref[0, e, 1:2, :]
        dt = jnp.exp(lam_ref[0, e, 2:3, :])

        def power(k, lr=lr, li=li, dt=dt):
            mag = jnp.exp(k * (lr * dt))
            return mag * jnp.cos(k * (li * dt)), mag * jnp.sin(k * (li * dt))

        one_r, one_i = power(1.0)
        den = lr * lr + li * li
        nr, ni = one_r - 1.0, one_i
        coef_r = (nr * lr + ni * li) / den
        coef_i = (ni * lr - nr * li) / den
        b_r, b_i = bt_ref[0, e, 0], bt_ref[0, e, 1]
        bb_r = coef_r * b_r - coef_i * b_i
        bb_i = coef_r * b_i + coef_i * b_r
        c_r, c_i = ct_ref[0, e, 0], ct_ref[0, e, 1]
        ops.append(dict(power=power, bb=(bb_r, bb_i), c=(c_r, c_i)))
    fwd, bwd = ops

    zf_r, zf_i = cmul_rows(*fwd["power"](el - 1.0 - idx), *fwd["bb"])
    zb_r, zb_i = cmul_rows(*bwd["power"](idx), *bwd["bb"])
    yf_r, yf_i = cmul_rows(*fwd["power"](idx + 1.0), *fwd["c"])
    yb_r, yb_i = cmul_rows(*bwd["power"](el - idx), *bwd["c"])
    kf_r, kf_i = cmul_rows(*fwd["power"](idx), *fwd["c"])
    kb_r, kb_i = cmul_rows(*bwd["power"](idx), *bwd["c"])

    def pair_slot(parts, gl):
        return jnp.concatenate([jnp.where(in_group[gl], p, 0.0) for p in parts], axis=1).astype(_BF)

    for gl in range(2):
        at_ref[gl] = group_rows(ut_ref, gl).T
    z = (_dot(at_ref[0], pair_slot([zf_r, zf_i, zb_r, zb_i], 0))
         + _dot(at_ref[1], pair_slot([zf_r, zf_i, zb_r, zb_i], 1)))
    for comp in range(4):
        z_ref[comp] = z[:, comp * sw:(comp + 1) * sw]

    a_fr, a_fi = fwd["power"](float(el))
    a_br, a_bi = bwd["power"](float(el))

    def step(c, carry):
        fr, fi, br, bi = carry
        rows_f = pl.ds(pl.multiple_of(c * bsz, bsz), bsz)
        rows_b = pl.ds(pl.multiple_of((n_c - 1 - c) * bsz, bsz), bsz)
        xp_ref[0, rows_f, :] = fr
        xp_ref[1, rows_f, :] = fi
        xp_ref[2, rows_b, :] = br
        xp_ref[3, rows_b, :] = bi
        zfr = z_ref[0, rows_f, :]
        zfi = z_ref[1, rows_f, :]
        zbr = z_ref[2, rows_b, :]
        zbi = z_ref[3, rows_b, :]
        return (a_fr * fr - a_fi * fi + zfr, a_fr * fi + a_fi * fr + zfi,
                a_br * br - a_bi * bi + zbr, a_br * bi + a_bi * br + zbi)

    zero = jnp.zeros((bsz, sw), _F32)
    lax.fori_loop(0, n_c, step, (zero, zero, zero, zero), unroll=4)
    xp = jnp.concatenate([xp_ref[comp] for comp in range(4)], axis=1).astype(_BF)

    spread = (lax.broadcasted_iota(jnp.int32, (SSM_GROUP, cw), 1) % SSM_GROUP
              == lax.broadcasted_iota(jnp.int32, (SSM_GROUP, cw), 0)).astype(_BF)
    lag = (lax.broadcasted_iota(jnp.int32, (cw, cw), 0) // SSM_GROUP
           - lax.broadcasted_iota(jnp.int32, (cw, cw), 1) // SSM_GROUP)

    def lag_kernels(k_r, k_i, bb, gl):
        k_r, k_i = jnp.where(in_group[gl], k_r, 0.0), jnp.where(in_group[gl], k_i, 0.0)
        hi_dot = functools.partial(lax.dot_general, dimension_numbers=(((1,), (1,)), ((), ())),
                                   precision=_HI, preferred_element_type=_F32)
        return hi_dot(k_r, bb[0]) - hi_dot(k_i, bb[1])

    for gl in range(2):
        ktf = _dot(lag_kernels(kf_r, kf_i, fwd["bb"], gl).astype(_BF), spread)
        ktb = _dot(lag_kernels(kb_r, kb_i, bwd["bb"], gl).astype(_BF), spread)

        def lag_rows(kt, k):
            blk = kt[k * SSM_GROUP:(k + 1) * SSM_GROUP, :]
            return jnp.broadcast_to(blk[None], (el, SSM_GROUP, cw)).reshape(cw, cw)

        tm = jnp.where(lag == 0, lag_rows(ktf, 0) + lag_rows(ktb, 0), 0.0)
        for k in range(1, el):
            tm = jnp.where(lag == k, lag_rows(ktf, k), tm)
            tm = jnp.where(lag == -k, lag_rows(ktb, k), tm)
        u_g = group_rows(ut_ref, gl)
        yt = _dot(tm.astype(_BF), u_g) + _nt_dot(pair_slot([yf_r, -yf_i, yb_r, -yb_i], gl), xp)
        d_rows = jnp.broadcast_to(dsk_ref[0, gl][None], (el, SSM_GROUP, 1)).reshape(cw, 1)
        yt = yt + d_rows * u_g.astype(_F32)
        yt_ref[:, gl * SSM_GROUP:(gl + 1) * SSM_GROUP, :] = yt.reshape(
            el, SSM_GROUP, n_cols).astype(_BF)


def _ssm(ut, lam, bt, ct, dsk, *, bsz):
    _, u_w, n_cols = ut.shape
    n_c = n_cols // bsz
    n_pairs = u_w // (2 * SSM_GROUP)
    cw = SSM_CHUNK * SSM_GROUP
    sw = 2 * SSM_STATE
    kern = functools.partial(_ssm_kernel, bsz=bsz, n_c=n_c)
    blk = pl.BlockSpec((SSM_CHUNK, 2 * SSM_GROUP, n_cols), lambda g: (0, g, 0))
    par = pl.BlockSpec((1, 2, 2, SSM_GROUP, sw), lambda g: (g, 0, 0, 0, 0))
    return pl.pallas_call(
        kern,
        grid=(n_pairs,),
        in_specs=[blk,
                  pl.BlockSpec((1, 2, 3, sw), lambda g: (g, 0, 0, 0)),
                  par, par,
                  pl.BlockSpec((1, 2, SSM_GROUP, 1), lambda g: (g, 0, 0, 0))],
        out_specs=blk,
        out_shape=jax.ShapeDtypeStruct(ut.shape, _BF),
        scratch_shapes=[pltpu.VMEM((2, n_cols, cw), _BF),
                        pltpu.VMEM((4, n_cols, sw), _F32),
                        pltpu.VMEM((4, n_cols, sw), _F32)],
        compiler_params=pltpu.CompilerParams(
            dimension_semantics=("arbitrary",), vmem_limit_bytes=VMEM_LIMIT_BYTES),
        name="s5_chunked",
    )(ut, lam, bt, ct, dsk)


def _post_kernel(o_ref, gates_ref, y_ref, x_ref, mod_ref, bglu_ref, gfin_ref,
                 wua_ref, wglu_ref, wus_ref, wout_ref, out_ref, *, d, aw, sw):
    y = jax.nn.gelu(y_ref[0].astype(_F32))
    y = y * _sigmoid(_dot(y.astype(_BF), wglu_ref[...]) + bglu_ref[...])
    s_br = _dot((y * gates_ref[0, :, aw:aw + sw].astype(_F32)).astype(_BF), wus_ref[...])
    a_br = _dot((o_ref[0].astype(_F32) * gates_ref[0, :, 0:aw].astype(_F32)).astype(_BF),
                wua_ref[...])
    g_off = aw + sw
    merged = (gates_ref[0, :, g_off:g_off + d].astype(_F32) * a_br
              + gates_ref[0, :, g_off + d:g_off + 2 * d].astype(_F32) * s_br)
    r = _dot(merged.astype(_BF), wout_ref[...])
    xo = x_ref[0] + mod_ref[0, :, 2 * d:3 * d] * r
    inv = lax.rsqrt(jnp.mean(xo * xo, axis=-1, keepdims=True) + EPS)
    out_ref[0] = (xo * inv) * gfin_ref[...]


def _post(o, gates, y, x, mod3, b_glu, g_final, wua, wglu, wus, wout, *, rows):
    bsz, s, d = x.shape
    aw = o.shape[2]
    sw = y.shape[2]
    kern = functools.partial(_post_kernel, d=d, aw=aw, sw=sw)
    full = lambda a: pl.BlockSpec(a.shape, lambda b, t: (0,) * a.ndim)
    seq_spec = lambda w: pl.BlockSpec((1, rows, w), lambda b, t: (b, t, 0))
    bgl = b_glu.reshape(1, sw)
    gfi = g_final.reshape(1, d)
    return pl.pallas_call(
        kern,
        grid=(bsz, s // rows),
        in_specs=[seq_spec(aw), seq_spec(gates.shape[2]), seq_spec(sw), seq_spec(d),
                  pl.BlockSpec((1, 1, 3 * d), lambda b, t: (b, 0, 0)),
                  full(bgl), full(gfi), full(wua), full(wglu), full(wus), full(wout)],
        out_specs=seq_spec(d),
        out_shape=jax.ShapeDtypeStruct(x.shape, _F32),
        compiler_params=pltpu.CompilerParams(
            dimension_semantics=("arbitrary", "arbitrary"), vmem_limit_bytes=VMEM_LIMIT_BYTES),
        name="merge_out",
    )(o, gates, y, x, mod3, bgl, gfi, wua, wglu, wus, wout)


def _rope_constants():
    half = ROT_DIM // 2
    freq = (ROPE_THETA ** (-jnp.arange(half, dtype=_F32) * 2.0 / ROT_DIM)).reshape(half, 1)
    in_head = jnp.arange(LANES) % HEAD_DIM
    j = jnp.arange(half)[:, None]
    cos_sel = ((in_head == j) | (in_head == half + j)).astype(_F32)
    sin_sel = (in_head == half + j).astype(_F32) - (in_head == j).astype(_F32)
    zero = jnp.zeros_like(cos_sel)
    return freq, jnp.concatenate([jnp.concatenate([cos_sel, zero], axis=1),
                                  jnp.concatenate([zero, sin_sel], axis=1)], axis=0)


def _ssm_params(lam_re, lam_im, log_dt, b_re, b_im, c_re, c_im):
    n_g, n_p = lam_re.shape[1], lam_re.shape[2]
    pairs = n_g // 2

    def lanes(m):
        rows = m.shape[2]
        return m.reshape(2, pairs, 2, rows, n_p).transpose(1, 0, 3, 2, 4).reshape(pairs, 2, rows, 2 * n_p)

    lam = lanes(jnp.stack([lam_re, lam_im, jnp.broadcast_to(log_dt[..., None], lam_re.shape)],
                          axis=2).astype(_F32))
    bt = jnp.stack([lanes(jnp.swapaxes(b_re, -1, -2)), lanes(jnp.swapaxes(b_im, -1, -2))], axis=2)
    ct = jnp.stack([lanes(c_re), lanes(c_im)], axis=2)
    return lam, bt.astype(_F32), ct.astype(_F32)


def _layer(x, c, positions, layer_idx, w_ada, b_ada, g_pre, w_in, lam_qk, g_subln,
           lam_re, lam_im, log_dt, b_re, b_im, c_re, c_im, d_skip,
           w_glu, b_glu, w_up_attn, w_up_ssm, w_out, g_final):
    bsz, s, d = x.shape
    n_c = s // SSM_CHUNK
    qk_w = ATTN_HEADS * 2 * HEAD_DIM
    v_w = ATTN_HEADS * V_HEAD_DIM
    u_w = lam_re.shape[1] * SSM_GROUP
    u_off = 2 * qk_w + 2 * v_w
    lam_init = 0.8 - 0.6 * math.exp(-0.3 * layer_idx)
    mod, lam, w_in_bf, wvt_bf = _prepare(c, w_ada, b_ada, lam_qk, w_in, lam_init=lam_init,
                                         v_off=2 * qk_w, v_w=v_w)
    mod3 = mod.reshape(bsz, 1, 3 * d)
    rows = min(ROW_BLOCK, s)
    pos_rows = positions.astype(_F32).reshape(bsz * (s // rows), 1, rows)
    qt, k, vt, gates, u = _inproj(x, mod3, g_pre, w_in_bf, wvt_bf, pos_rows, rows=rows,
                                  qk_w=qk_w, v_w=v_w, u_off=u_off, u_w=u_w)
    o = _attention(lam, g_subln, qt, k, vt, tq=min(Q_ROWS, s), lam_init=lam_init)
    ut = u.reshape(bsz, n_c, SSM_CHUNK, u_w).transpose(2, 3, 1, 0).reshape(SSM_CHUNK, u_w, n_c * bsz)
    dsk = d_skip.astype(_F32).reshape(u_w // (2 * SSM_GROUP), 2, SSM_GROUP, 1)
    yt = _ssm(ut, *_ssm_params(lam_re, lam_im, log_dt, b_re, b_im, c_re, c_im), dsk, bsz=bsz)
    y = yt.reshape(SSM_CHUNK, u_w, n_c, bsz).transpose(3, 2, 0, 1).reshape(bsz, s, u_w)
    return _post(o, gates, y, x, mod3, b_glu, g_final,
                 w_up_attn.astype(_BF), w_glu.astype(_BF), w_up_ssm.astype(_BF), w_out.astype(_BF),
                 rows=rows)


def kernel(x, c, positions, w_ada, b_ada, g_pre, w_in, lam_qk, g_subln, ssm_lam_re, ssm_lam_im,
           ssm_log_dt, ssm_b_re, ssm_b_im, ssm_c_re, ssm_c_im, ssm_d, w_glu, b_glu, w_up_attn,
           w_up_ssm, w_out, g_final):
    depth = w_ada.shape[0]
    assert depth == 1, "the final RMSNorm is fused into the single layer's epilogue"
    assert x.shape[1] % (SSM_CHUNK * 8) == 0
    return _layer(x, c, positions, 0, w_ada[0], b_ada[0], g_pre[0], w_in[0],
                  lam_qk[0], g_subln[0], ssm_lam_re[0], ssm_lam_im[0], ssm_log_dt[0], ssm_b_re[0],
                  ssm_b_im[0], ssm_c_re[0], ssm_c_im[0], ssm_d[0], w_glu[0], b_glu[0],
                  w_up_attn[0], w_up_ssm[0], w_out[0], g_final)
```

```python
import functools
import math

import jax
import jax.numpy as jnp
import numpy as np
from jax import lax
from jax.experimental import pallas as pl
from jax.experimental.pallas import tpu as pltpu

ATTN_HEADS = 4
HEAD_DIM = 64
V_HEAD_DIM = 2 * HEAD_DIM
ROT_DIM = HEAD_DIM // 4
ROPE_THETA = 500000.0
SSM_GROUP = 16
SSM_STATE = 64
SSM_CHUNK = 16
EPS = 1e-6
LOG2_E = math.log2(math.e)
LANES = 128
VMEM_LIMIT_BYTES = 56 * 1024 * 1024
ROW_BLOCK = 1024
Q_ROWS = 512
KEY_BLOCK = 256
POST_SUB_ROWS = 256
VT_PAD = 16

_HI = lax.Precision.HIGHEST
_BF = jnp.bfloat16
_F32 = jnp.float32


def _nt_dot(a, b):
    return lax.dot_general(a, b, (((1,), (1,)), ((), ())), preferred_element_type=_F32)


def _dot(a, b):
    return jnp.dot(a, b, preferred_element_type=_F32)


def _sigmoid(x):
    return 0.5 * jnp.tanh(0.5 * x) + 0.5


def _prep_kernel(c_ref, w_ref, b_ref, lamqk_ref, win_ref, o_ref, lam_ref, winb_ref, wvt_ref,
                 *, lam_init, n_mod, v_tile):
    j = pl.program_id(0)

    @pl.when(j < n_mod)
    def _():
        o_ref[...] = jnp.dot(jax.nn.silu(c_ref[...]), w_ref[...], precision=_HI,
                             preferred_element_type=_F32) + b_ref[...]

    @pl.when(j == 0)
    def _():
        lf = lamqk_ref[...]
        lam = (jnp.exp(jnp.sum(lf[0:1, :] * lf[1:2, :], axis=-1, keepdims=True))
               - jnp.exp(jnp.sum(lf[2:3, :] * lf[3:4, :], axis=-1, keepdims=True)) + lam_init)
        lam_ref[...] = jnp.broadcast_to(lam, lam_ref.shape)

    w = win_ref[...]
    winb_ref[...] = w.astype(_BF)

    @pl.when(j == v_tile)
    def _():
        wvt_ref[...] = w.T.astype(_BF)


def _prepare(c, w_ada, b_ada, lam_qk, w_in, *, lam_init, v_off, v_w):
    bsz, d = c.shape
    n = w_ada.shape[1]
    n_in = w_in.shape[1]
    tn = d
    n_mod = n // tn
    assert v_off % v_w == 0 and n_in % v_w == 0 and n_in // v_w >= n_mod
    last = n_mod - 1
    return pl.pallas_call(
        functools.partial(_prep_kernel, lam_init=lam_init, n_mod=n_mod, v_tile=v_off // v_w),
        grid=(n_in // v_w,),
        in_specs=[pl.BlockSpec((bsz, d), lambda j: (0, 0)),
                  pl.BlockSpec((d, tn), lambda j: (0, jnp.minimum(j, last))),
                  pl.BlockSpec((1, tn), lambda j: (0, jnp.minimum(j, last))),
                  pl.BlockSpec(lam_qk.shape, lambda j: (0, 0)),
                  pl.BlockSpec((d, v_w), lambda j: (0, j))],
        out_specs=[pl.BlockSpec((bsz, tn), lambda j: (0, jnp.minimum(j, last))),
                   pl.BlockSpec((1, LANES), lambda j: (0, 0)),
                   pl.BlockSpec((d, v_w), lambda j: (0, j)),
                   pl.BlockSpec((v_w, d), lambda j: (0, 0))],
        out_shape=[jax.ShapeDtypeStruct((bsz, n), _F32), jax.ShapeDtypeStruct((1, LANES), _F32),
                   jax.ShapeDtypeStruct((d, n_in), _BF), jax.ShapeDtypeStruct((v_w, d), _BF)],
        compiler_params=pltpu.CompilerParams(
            dimension_semantics=("arbitrary",), vmem_limit_bytes=VMEM_LIMIT_BYTES),
        name="adaln_mod",
    )(c, w_ada, b_ada.reshape(1, n), lam_qk, w_in)


def _inproj_kernel(x_ref, mod_ref, gpre_ref, w_ref, wvt_ref, pos_ref, freq_ref, spread_ref,
                   qt_ref, k_ref, vt_ref, gates_ref, u_ref, *, d, qk_w, v_w, u_off, u_w):
    shift = mod_ref[0, :, 0:d]
    scale = mod_ref[0, :, d:2 * d]
    x = x_ref[0]
    inv = lax.rsqrt(jnp.mean(x * x, axis=-1, keepdims=True) + EPS)
    h = ((x * inv) * (gpre_ref[...] * (1.0 + scale)) + shift).astype(_BF)

    rows = x.shape[0]
    half = ROT_DIM // 2
    ang = freq_ref[...] * pos_ref[0]
    cs = jnp.concatenate([jnp.cos(ang), jnp.sin(ang)], axis=0)
    tab = jnp.dot(cs.T, spread_ref[...], precision=_HI, preferred_element_type=_F32)
    lane = lax.broadcasted_iota(jnp.int32, (1, LANES), 1)
    reps = qk_w // LANES
    rot_c = jnp.concatenate([jnp.where(lane % HEAD_DIM < ROT_DIM, tab[:, 0:LANES], 1.0)] * reps,
                            axis=1)
    rot_s = jnp.concatenate([tab[:, LANES:2 * LANES]] * reps, axis=1)
    low = lax.broadcasted_iota(jnp.int32, (1, qk_w), 1) % HEAD_DIM < half

    def rope(t):
        up = pltpu.roll(t, qk_w - half, 1)
        dn = pltpu.roll(t, half, 1)
        return t * rot_c + jnp.where(low, up, dn) * rot_s

    v_off = 2 * qk_w
    vt = _nt_dot(wvt_ref[...], h).astype(_BF)
    pad = (lax.broadcasted_iota(jnp.int32, (VT_PAD, rows), 0) == 0).astype(_BF)
    for hd in range(v_w // V_HEAD_DIM):
        vt_ref[0, hd, 0:V_HEAD_DIM, :] = vt[hd * V_HEAD_DIM:(hd + 1) * V_HEAD_DIM, :]
        vt_ref[0, hd, V_HEAD_DIM:V_HEAD_DIM + VT_PAD, :] = pad
    u_ref[0] = _dot(h, w_ref[:, u_off:u_off + u_w]).astype(_BF)
    qt = rope(_dot(h, w_ref[:, 0:qk_w]) * (HEAD_DIM ** -0.5 * LOG2_E)).T.astype(_BF)
    hw = 2 * HEAD_DIM
    for hd in range(qk_w // hw):
        qt_ref[0, hd] = qt[hd * hw:(hd + 1) * hw, :]
    k_ref[0] = rope(_dot(h, w_ref[:, qk_w:2 * qk_w])).astype(_BF)
    za_off = v_off + v_w
    z_a = _dot(h, w_ref[:, za_off:za_off + v_w])
    gates_ref[0, :, 0:v_w] = (z_a * _sigmoid(z_a)).astype(_BF)
    zs_off = u_off + u_w
    z_s = _dot(h, w_ref[:, zs_off:zs_off + u_w])
    gates_ref[0, :, v_w:v_w + u_w] = (z_s * _sigmoid(z_s)).astype(_BF)
    g_off = zs_off + u_w
    rest = w_ref.shape[1] - g_off
    piece = 512
    for p in range(rest // piece):
        gates_ref[0, :, v_w + u_w + p * piece:v_w + u_w + (p + 1) * piece] = _sigmoid(_dot(
            h, w_ref[:, g_off + p * piece:g_off + (p + 1) * piece])).astype(_BF)


def _inproj(x, mod3, g_pre, w_in_bf, wvt_bf, pos_rows, *, rows, qk_w, v_w, u_off, u_w):
    bsz, s, d = x.shape
    n_in = w_in_bf.shape[1]
    gates_w = n_in - 2 * qk_w - v_w - u_w
    n_heads = v_w // V_HEAD_DIM
    vt_rows = V_HEAD_DIM + VT_PAD
    kern = functools.partial(_inproj_kernel, d=d, qk_w=qk_w, v_w=v_w, u_off=u_off, u_w=u_w)
    freq, spread = _rope_constants()
    seq_spec = lambda w: pl.BlockSpec((1, rows, w), lambda b, t: (b, t, 0))
    return pl.pallas_call(
        kern,
        grid=(bsz, s // rows),
        in_specs=[seq_spec(d),
                  pl.BlockSpec((1, 1, 3 * d), lambda b, t: (b, 0, 0)),
                  pl.BlockSpec((1, d), lambda b, t: (0, 0)),
                  pl.BlockSpec((d, n_in), lambda b, t: (0, 0), pipeline_mode=pl.Buffered(1)),
                  pl.BlockSpec((v_w, d), lambda b, t: (0, 0), pipeline_mode=pl.Buffered(1)),
                  pl.BlockSpec((1, 1, rows), lambda b, t: (b * (s // rows) + t, 0, 0)),
                  pl.BlockSpec(freq.shape, lambda b, t: (0, 0)),
                  pl.BlockSpec(spread.shape, lambda b, t: (0, 0))],
        out_specs=[pl.BlockSpec((1, n_heads, 2 * HEAD_DIM, rows), lambda b, t: (b, 0, 0, t)),
                   seq_spec(qk_w),
                   pl.BlockSpec((1, n_heads, vt_rows, rows), lambda b, t: (b, 0, 0, t)),
                   seq_spec(gates_w), seq_spec(u_w)],
        out_shape=[jax.ShapeDtypeStruct((bsz, n_heads, 2 * HEAD_DIM, s), _BF),
                   jax.ShapeDtypeStruct((bsz, s, qk_w), _BF),
                   jax.ShapeDtypeStruct((bsz, n_heads, vt_rows, s), _BF),
                   jax.ShapeDtypeStruct((bsz, s, gates_w), _BF),
                   jax.ShapeDtypeStruct((bsz, s, u_w), _BF)],
        compiler_params=pltpu.CompilerParams(
            dimension_semantics=("arbitrary", "arbitrary"), vmem_limit_bytes=VMEM_LIMIT_BYTES),
        name="norm_inproj",
    )(x, mod3, g_pre.reshape(1, d), w_in_bf, wvt_bf, pos_rows, freq, spread)


def _attn_kernel(tab_ref, lam_ref, gsub_ref, qt_ref, k_ref, vt_ref, o_ref, *scratch, lam_init):
    s_ref = (scratch[0:2], scratch[2:4])
    e_ref = (scratch[4:6], scratch[6:8])
    m_ref = (scratch[8:10], scratch[10:12])
    ev_ref = (scratch[12:14], scratch[14:16])
    n_keys, tq = s_ref[0][0].shape
    tk = KEY_BLOCK
    n_kb = n_keys // tk
    step = pl.program_id(0)

    @pl.when(step == 0)
    def _():
        for ref in scratch:
            ref[...] = jnp.zeros_like(ref)

    lam = lam_ref[0:1, 0:1]

    def tick(new, old):
        ot = None
        for mp in range(2):
            ev = ev_ref[old][mp][...]
            nrm = ev[0:V_HEAD_DIM, :] * (1.0 / ev[V_HEAD_DIM:V_HEAD_DIM + 1, :])
            ot = nrm if ot is None else ot - lam * nrm
        inv = lax.rsqrt(jnp.mean(ot * ot, axis=0, keepdims=True) + EPS)
        o_ref[0] = (((ot * inv) * gsub_ref[...]) * (1.0 - lam_init)).T.astype(_BF)
        qt = qt_ref[0, 0]
        first = lax.broadcasted_iota(jnp.int32, (2 * HEAD_DIM, 1), 0) < HEAD_DIM
        zero = jnp.zeros_like(qt)
        qmt = (jnp.where(first, qt, zero), jnp.where(first, zero, qt))
        m_rows = [jnp.max(m_ref[old][mp][...], axis=0, keepdims=True) for mp in range(2)]
        n_part = 4 if n_kb % 4 == 0 else 1
        for mp in range(2):
            ev = None
            m = jnp.full((8, tq), -jnp.inf, _F32)
            for kb in range(n_kb):
                keys = slice(kb * tk, (kb + 1) * tk)
                if kb % (n_kb // n_part) == 0:
                    part_keys = slice(kb * tk, (kb + n_kb // n_part) * tk)
                    part = _dot(vt_ref[0, 0, :, part_keys], e_ref[new][mp][part_keys, :])
                    ev = part if ev is None else ev + part
                st = _dot(k_ref[0, keys, :], qmt[mp])
                s_ref[new][mp][keys, :] = st
                m = jnp.maximum(m, jnp.max(st.reshape(tk // 8, 8, tq), axis=0))
                e_ref[old][mp][keys, :] = jnp.exp2(s_ref[old][mp][keys, :] - m_rows[mp]).astype(_BF)
            m_ref[new][mp][...] = m
            ev_ref[new][mp][...] = ev

    @pl.when(step % 2 == 0)
    def _():
        tick(0, 1)

    @pl.when(step % 2 == 1)
    def _():
        tick(1, 0)


def _attention(lam, g_subln, qt, k, vt, *, tq, lam_init):
    bsz, n_heads, hw, s = qt.shape
    _, _, vt_rows, _ = vt.shape
    n_q = s // tq
    n_items = bsz * n_heads * n_q
    kern = functools.partial(_attn_kernel, lam_init=lam_init)
    n_steps = n_items + 3

    def item(j):
        j = np.clip(j, 0, n_items - 1)
        return j // (n_heads * n_q), (j // n_q) % n_heads, j % n_q

    g = np.arange(n_steps)
    tab = jnp.asarray(np.stack([*item(g), *item(g - 2), *item(g - 3)]).astype(np.int32))

    return pl.pallas_call(
        kern,
        grid_spec=pltpu.PrefetchScalarGridSpec(
            num_scalar_prefetch=1,
            grid=(n_steps,),
            in_specs=[pl.BlockSpec(lam.shape, lambda g, t: (0, 0)),
                      pl.BlockSpec((V_HEAD_DIM, 1), lambda g, t: (0, 0)),
                      pl.BlockSpec((1, 1, hw, tq), lambda g, t: (t[0, g], t[1, g], 0, t[2, g])),
                      pl.BlockSpec((1, s, hw), lambda g, t: (t[0, g], 0, t[1, g])),
                      pl.BlockSpec((1, 1, vt_rows, s), lambda g, t: (t[3, g], t[4, g], 0, 0))],
            out_specs=pl.BlockSpec((1, tq, V_HEAD_DIM), lambda g, t: (t[6, g], t[8, g], t[7, g])),
            scratch_shapes=([pltpu.VMEM((s, tq), _F32)] * 4 + [pltpu.VMEM((s, tq), _BF)] * 4
                            + [pltpu.VMEM((8, tq), _F32)] * 4 + [pltpu.VMEM((vt_rows, tq), _F32)] * 4)),
        out_shape=jax.ShapeDtypeStruct((bsz, s, n_heads * V_HEAD_DIM), _BF),
        compiler_params=pltpu.CompilerParams(
            dimension_semantics=("arbitrary",), vmem_limit_bytes=VMEM_LIMIT_BYTES),
        name="diff_attention",
    )(tab, lam, g_subln.reshape(V_HEAD_DIM, 1), qt, k, vt)


def _ssm_kernel(ut_ref, lam_ref, bt_ref, ct_ref, dsk_ref, yt_ref, at_ref, z_ref, xp_ref,
                *, bsz, n_c):
    el = SSM_CHUNK
    cw = el * SSM_GROUP
    sw = 2 * SSM_STATE
    n_cols = n_c * bsz
    in_group = [lax.broadcasted_iota(jnp.int32, (1, sw), 1) // SSM_STATE == gl for gl in range(2)]

    def group_rows(ref, gl):
        return ref[:, gl * SSM_GROUP:(gl + 1) * SSM_GROUP, :].reshape(cw, n_cols)

    def per_row(rows, cols):
        return (jnp.broadcast_to(rows[:, None, :], (el, SSM_GROUP, sw)),
                jnp.broadcast_to(cols[None, :, :], (el, SSM_GROUP, sw)))

    def cmul_rows(p_r, p_i, m_r, m_i):
        pr3, mr3 = per_row(p_r, m_r)
        pi3, mi3 = per_row(p_i, m_i)
        return (pr3 * mr3 - pi3 * mi3).reshape(cw, sw), (pr3 * mi3 + pi3 * mr3).reshape(cw, sw)

    idx = lax.broadcasted_iota(jnp.int32, (el, 1), 0).astype(_F32)
    ops = []
    for e in range(2):
        lr = jnp.minimum(lam_ref[0, e, 0:1, :], -1e-4)
        li = lam_ref[0, e, 1:2, :]
        dt = jnp.exp(lam_ref[0, e, 2:3, :])

        def power(k, lr=lr, li=li, dt=dt):
            mag = jnp.exp(k * (lr * dt))
            return mag * jnp.cos(k * (li * dt)), mag * jnp.sin(k * (li * dt))

        one_r, one_i = power(1.0)
        den = lr * lr + li * li
        nr, ni = one_r - 1.0, one_i
        coef_r = (nr * lr + ni * li) / den
        coef_i = (ni * lr - nr * li) / den
        b_r, b_i = bt_ref[0, e, 0], bt_ref[0, e, 1]
        bb_r = coef_r * b_r - coef_i * b_i
        bb_i = coef_r * b_i + coef_i * b_r
        c_r, c_i = ct_ref[0, e, 0], ct_ref[0, e, 1]
        ops.append(dict(power=power, bb=(bb_r, bb_i), c=(c_r, c_i)))
    fwd, bwd = ops

    zf_r, zf_i = cmul_rows(*fwd["power"](el - 1.0 - idx), *fwd["bb"])
    zb_r, zb_i = cmul_rows(*bwd["power"](idx), *bwd["bb"])
    yf_r, yf_i = cmul_rows(*fwd["power"](idx + 1.0), *fwd["c"])
    yb_r, yb_i = cmul_rows(*bwd["power"](el - idx), *bwd["c"])
    kf_r, kf_i = cmul_rows(*fwd["power"](idx), *fwd["c"])
    kb_r, kb_i = cmul_rows(*bwd["power"](idx), *bwd["c"])

    def pair_slot(parts, gl):
        return jnp.concatenate([jnp.where(in_group[gl], p, 0.0) for p in parts], axis=1).astype(_BF)

    for gl in range(2):
        at_ref[gl] = group_rows(ut_ref, gl).T
    z = (_dot(at_ref[0], pair_slot([zf_r, zf_i, zb_r, zb_i], 0))
         + _dot(at_ref[1], pair_slot([zf_r, zf_i, zb_r, zb_i], 1)))
    for comp in range(4):
        z_ref[comp] = z[:, comp * sw:(comp + 1) * sw]

    a_fr, a_fi = fwd["power"](float(el))
    a_br, a_bi = bwd["power"](float(el))

    def step(c, carry):
        fr, fi, br, bi = carry
        rows_f = pl.ds(pl.multiple_of(c * bsz, bsz), bsz)
        rows_b = pl.ds(pl.multiple_of((n_c - 1 - c) * bsz, bsz), bsz)
        xp_ref[0, rows_f, :] = fr
        xp_ref[1, rows_f, :] = fi
        xp_ref[2, rows_b, :] = br
        xp_ref[3, rows_b, :] = bi
        zfr = z_ref[0, rows_f, :]
        zfi = z_ref[1, rows_f, :]
        zbr = z_ref[2, rows_b, :]
        zbi = z_ref[3, rows_b, :]
        return (a_fr * fr - a_fi * fi + zfr, a_fr * fi + a_fi * fr + zfi,
                a_br * br - a_bi * bi + zbr, a_br * bi + a_bi * br + zbi)

    zero = jnp.zeros((bsz, sw), _F32)
    lax.fori_loop(0, n_c, step, (zero, zero, zero, zero), unroll=4)
    xp = jnp.concatenate([xp_ref[comp] for comp in range(4)], axis=1).astype(_BF)

    spread = (lax.broadcasted_iota(jnp.int32, (SSM_GROUP, cw), 1) % SSM_GROUP
              == lax.broadcasted_iota(jnp.int32, (SSM_GROUP, cw), 0)).astype(_BF)
    lag = (lax.broadcasted_iota(jnp.int32, (cw, cw), 0) // SSM_GROUP
           - lax.broadcasted_iota(jnp.int32, (cw, cw), 1) // SSM_GROUP)

    def lag_kernels(k_r, k_i, bb, gl):
        k_r, k_i = jnp.where(in_group[gl], k_r, 0.0), jnp.where(in_group[gl], k_i, 0.0)
        hi_dot = functools.partial(lax.dot_general, dimension_numbers=(((1,), (1,)), ((), ())),
                                   precision=_HI, preferred_element_type=_F32)
        return hi_dot(k_r, bb[0]) - hi_dot(k_i, bb[1])

    for gl in range(2):
        ktf = _dot(lag_kernels(kf_r, kf_i, fwd["bb"], gl).astype(_BF), spread)
        ktb = _dot(lag_kernels(kb_r, kb_i, bwd["bb"], gl).astype(_BF), spread)

        def lag_rows(kt, k):
            blk = kt[k * SSM_GROUP:(k + 1) * SSM_GROUP, :]
            return jnp.broadcast_to(blk[None], (el, SSM_GROUP, cw)).reshape(cw, cw)

        tm = jnp.where(lag == 0, lag_rows(ktf, 0) + lag_rows(ktb, 0), 0.0)
        for k in range(1, el):
            tm = jnp.where(lag == k, lag_rows(ktf, k), tm)
            tm = jnp.where(lag == -k, lag_rows(ktb, k), tm)
        u_g = group_rows(ut_ref, gl)
        yt = _dot(tm.astype(_BF), u_g) + _nt_dot(pair_slot([yf_r, -yf_i, yb_r, -yb_i], gl), xp)
        d_rows = jnp.broadcast_to(dsk_ref[0, gl][None], (el, SSM_GROUP, 1)).reshape(cw, 1)
        yt = yt + d_rows * u_g.astype(_F32)
        yt_ref[:, gl * SSM_GROUP:(gl + 1) * SSM_GROUP, :] = yt.reshape(
            el, SSM_GROUP, n_cols).astype(_BF)


def _ssm(ut, lam, bt, ct, dsk, *, bsz):
    _, u_w, n_cols = ut.shape
    n_c = n_cols // bsz
    n_pairs = u_w // (2 * SSM_GROUP)
    cw = SSM_CHUNK * SSM_GROUP
    sw = 2 * SSM_STATE
    kern = functools.partial(_ssm_kernel, bsz=bsz, n_c=n_c)
    blk = pl.BlockSpec((SSM_CHUNK, 2 * SSM_GROUP, n_cols), lambda g: (0, g, 0))
    par = pl.BlockSpec((1, 2, 2, SSM_GROUP, sw), lambda g: (g, 0, 0, 0, 0))
    return pl.pallas_call(
        kern,
        grid=(n_pairs,),
        in_specs=[blk,
                  pl.BlockSpec((1, 2, 3, sw), lambda g: (g, 0, 0, 0)),
                  par, par,
                  pl.BlockSpec((1, 2, SSM_GROUP, 1), lambda g: (g, 0, 0, 0))],
        out_specs=blk,
        out_shape=jax.ShapeDtypeStruct(ut.shape, _BF),
        scratch_shapes=[pltpu.VMEM((2, n_cols, cw), _BF),
                        pltpu.VMEM((4, n_cols, sw), _F32),
                        pltpu.VMEM((4, n_cols, sw), _F32)],
        compiler_params=pltpu.CompilerParams(
            dimension_semantics=("arbitrary",), vmem_limit_bytes=VMEM_LIMIT_BYTES),
        name="s5_chunked",
    )(ut, lam, bt, ct, dsk)


def _post_kernel(o_ref, gates_ref, y_ref, x_ref, mod_ref, bglu_ref, gfin_ref,
                 wua_ref, wglu_ref, wus_ref, wout_ref, out_ref,
                 ys_ref, lin_ref, abr_ref, sbr_ref, r_ref, *, d, aw, sw):
    g_off = aw + sw
    gate = mod_ref[0, :, 2 * d:3 * d]
    rows = x_ref.shape[1]
    subs = [slice(r0, r0 + POST_SUB_ROWS) for r0 in range(0, rows, POST_SUB_ROWS)]

    def stage_glu(rs):
        y = jax.nn.gelu(y_ref[0, rs, :].astype(_F32))
        ys_ref[rs, :] = y
        lin_ref[rs, :] = _dot(y.astype(_BF), wglu_ref[...])

    def stage_branches(rs):
        y = ys_ref[rs, :] * _sigmoid(lin_ref[rs, :] + bglu_ref[...])
        sbr_ref[rs, :] = _dot(y.astype(_BF) * gates_ref[0, rs, aw:aw + sw], wus_ref[...])
        abr_ref[rs, :] = _dot(o_ref[0, rs, :] * gates_ref[0, rs, 0:aw], wua_ref[...])

    def stage_out(rs):
        merged = (gates_ref[0, rs, g_off:g_off + d] * abr_ref[rs, :].astype(_BF)
                  + gates_ref[0, rs, g_off + d:g_off + 2 * d] * sbr_ref[rs, :].astype(_BF))
        r_ref[rs, :] = _dot(merged, wout_ref[...])

    def stage_norm(rs):
        xo = x_ref[0, rs, :] + gate * r_ref[rs, :]
        inv = lax.rsqrt(jnp.mean(xo * xo, axis=-1, keepdims=True) + EPS)
        out_ref[0, rs, :] = (xo * inv) * gfin_ref[...]

    stages = (stage_glu, stage_branches, stage_out, stage_norm)
    for wave in range(len(subs) + len(stages) - 1):
        for k in reversed(range(len(stages))):
            if 0 <= wave - k < len(subs):
                stages[k](subs[wave - k])


def _post(o, gates, y, x, mod3, b_glu, g_final, wua, wglu, wus, wout, *, rows):
    bsz, s, d = x.shape
    aw = o.shape[2]
    sw = y.shape[2]
    kern = functools.partial(_post_kernel, d=d, aw=aw, sw=sw)
    full = lambda a: pl.BlockSpec(a.shape, lambda b, t: (0,) * a.ndim)
    seq_spec = lambda w: pl.BlockSpec((1, rows, w), lambda b, t: (b, t, 0))
    bgl = b_glu.reshape(1, sw)
    gfi = g_final.reshape(1, d)
    return pl.pallas_call(
        kern,
        grid=(bsz, s // rows),
        in_specs=[seq_spec(aw), seq_spec(gates.shape[2]), seq_spec(sw), seq_spec(d),
                  pl.BlockSpec((1, 1, 3 * d), lambda b, t: (b, 0, 0)),
                  full(bgl), full(gfi), full(wua), full(wglu), full(wus), full(wout)],
        out_specs=seq_spec(d),
        out_shape=jax.ShapeDtypeStruct(x.shape, _F32),
        scratch_shapes=[pltpu.VMEM((rows, sw), _F32), pltpu.VMEM((rows, sw), _F32),
                        pltpu.VMEM((rows, d), _F32), pltpu.VMEM((rows, d), _F32),
                        pltpu.VMEM((rows, d), _F32)],
        compiler_params=pltpu.CompilerParams(
            dimension_semantics=("arbitrary", "arbitrary"), vmem_limit_bytes=VMEM_LIMIT_BYTES),
        name="merge_out",
    )(o, gates, y, x, mod3, bgl, gfi, wua, wglu, wus, wout)


def _rope_constants():
    half = ROT_DIM // 2
    freq = (ROPE_THETA ** (-jnp.arange(half, dtype=_F32) * 2.0 / ROT_DIM)).reshape(half, 1)
    in_head = jnp.arange(LANES) % HEAD_DIM
    j = jnp.arange(half)[:, None]
    cos_sel = ((in_head == j) | (in_head == half + j)).astype(_F32)
    sin_sel = (in_head == half + j).astype(_F32) - (in_head == j).astype(_F32)
    zero = jnp.zeros_like(cos_sel)
    return freq, jnp.concatenate([jnp.concatenate([cos_sel, zero], axis=1),
                                  jnp.concatenate([zero, sin_sel], axis=1)], axis=0)


def _ssm_params(lam_re, lam_im, log_dt, b_re, b_im, c_re, c_im):
    n_g, n_p = lam_re.shape[1], lam_re.shape[2]
    pairs = n_g // 2

    def lanes(m):
        rows = m.shape[2]
        return m.reshape(2, pairs, 2, rows, n_p).transpose(1, 0, 3, 2, 4).reshape(pairs, 2, rows, 2 * n_p)

    lam = lanes(jnp.stack([lam_re, lam_im, jnp.broadcast_to(log_dt[..., None], lam_re.shape)],
                          axis=2).astype(_F32))
    bt = jnp.stack([lanes(jnp.swapaxes(b_re, -1, -2)), lanes(jnp.swapaxes(b_im, -1, -2))], axis=2)
    ct = jnp.stack([lanes(c_re), lanes(c_im)], axis=2)
    return lam, bt.astype(_F32), ct.astype(_F32)


def _layer(x, c, positions, layer_idx, w_ada, b_ada, g_pre, w_in, lam_qk, g_subln,
           lam_re, lam_im, log_dt, b_re, b_im, c_re, c_im, d_skip,
           w_glu, b_glu, w_up_attn, w_up_ssm, w_out, g_final):
    bsz, s, d = x.shape
    n_c = s // SSM_CHUNK
    qk_w = ATTN_HEADS * 2 * HEAD_DIM
    v_w = ATTN_HEADS * V_HEAD_DIM
    u_w = lam_re.shape[1] * SSM_GROUP
    u_off = 2 * qk_w + 2 * v_w
    lam_init = 0.8 - 0.6 * math.exp(-0.3 * layer_idx)
    mod, lam, w_in_bf, wvt_bf = _prepare(c, w_ada, b_ada, lam_qk, w_in, lam_init=lam_init,
                                         v_off=2 * qk_w, v_w=v_w)
    mod3 = mod.reshape(bsz, 1, 3 * d)
    rows = min(ROW_BLOCK, s)
    pos_rows = positions.astype(_F32).reshape(bsz * (s // rows), 1, rows)
    qt, k, vt, gates, u = _inproj(x, mod3, g_pre, w_in_bf, wvt_bf, pos_rows, rows=rows,
                                  qk_w=qk_w, v_w=v_w, u_off=u_off, u_w=u_w)
    o = _attention(lam, g_subln, qt, k, vt, tq=min(Q_ROWS, s), lam_init=lam_init)
    ut = u.reshape(bsz, n_c, SSM_CHUNK, u_w).transpose(2, 3, 1, 0).reshape(SSM_CHUNK, u_w, n_c * bsz)
    dsk = d_skip.astype(_F32).reshape(u_w // (2 * SSM_GROUP), 2, SSM_GROUP, 1)
    yt = _ssm(ut, *_ssm_params(lam_re, lam_im, log_dt, b_re, b_im, c_re, c_im), dsk, bsz=bsz)
    y = yt.reshape(SSM_CHUNK, u_w, n_c, bsz).transpose(3, 2, 0, 1).reshape(bsz, s, u_w)
    return _post(o, gates, y, x, mod3, b_glu, g_final,
                 w_up_attn.astype(_BF), w_glu.astype(_BF), w_up_ssm.astype(_BF), w_out.astype(_BF),
                 rows=rows)


def kernel(x, c, positions, w_ada, b_ada, g_pre, w_in, lam_qk, g_subln, ssm_lam_re, ssm_lam_im,
           ssm_log_dt, ssm_b_re, ssm_b_im, ssm_c_re, ssm_c_im, ssm_d, w_glu, b_glu, w_up_attn,
           w_up_ssm, w_out, g_final):
    depth = w_ada.shape[0]
    assert depth == 1, "the final RMSNorm is fused into the single layer's epilogue"
    assert x.shape[1] % (SSM_CHUNK * 8) == 0
    return _layer(x, c, positions, 0, w_ada[0], b_ada[0], g_pre[0], w_in[0],
                  lam_qk[0], g_subln[0], ssm_lam_re[0], ssm_lam_im[0], ssm_log_dt[0], ssm_b_re[0],
                  ssm_b_im[0], ssm_c_re[0], ssm_c_im[0], ssm_d[0], w_glu[0], b_glu[0],
                  w_up_attn[0], w_up_ssm[0], w_out[0], g_final)
```

```python
import functools
import math

import jax
import jax.numpy as jnp
import numpy as np
from jax import lax
from jax.experimental import pallas as pl
from jax.experimental.pallas import tpu as pltpu

ATTN_HEADS = 4
HEAD_DIM = 64
V_HEAD_DIM = 2 * HEAD_DIM
ROT_DIM = HEAD_DIM // 4
ROPE_THETA = 500000.0
SSM_GROUP = 16
SSM_STATE = 64
SSM_CHUNK = 16
EPS = 1e-6
LOG2_E = math.log2(math.e)
LANES = 128
VMEM_LIMIT_BYTES = 56 * 1024 * 1024
ROW_BLOCK = 1024
Q_ROWS = 512
KEY_BLOCK = 256
VT_PAD = 16

_HI = lax.Precision.HIGHEST
_BF = jnp.bfloat16
_F32 = jnp.float32


def _nt_dot(a, b):
    return lax.dot_general(a, b, (((1,), (1,)), ((), ())), preferred_element_type=_F32)


def _dot(a, b):
    return jnp.dot(a, b, preferred_element_type=_F32)


def _sigmoid(x):
    return 0.5 * jnp.tanh(0.5 * x) + 0.5


def _prep_kernel(c_ref, w_ref, b_ref, lamqk_ref, win_ref, o_ref, lam_ref, winb_ref, wvt_ref,
                 *, lam_init, n_mod, v_tile):
    j = pl.program_id(0)

    @pl.when(j < n_mod)
    def _():
        o_ref[...] = jnp.dot(jax.nn.silu(c_ref[...]), w_ref[...], precision=_HI,
                             preferred_element_type=_F32) + b_ref[...]

    @pl.when(j == 0)
    def _():
        lf = lamqk_ref[...]
        lam = (jnp.exp(jnp.sum(lf[0:1, :] * lf[1:2, :], axis=-1, keepdims=True))
               - jnp.exp(jnp.sum(lf[2:3, :] * lf[3:4, :], axis=-1, keepdims=True)) + lam_init)
        lam_ref[...] = jnp.broadcast_to(lam, lam_ref.shape)

    w = win_ref[...]
    winb_ref[...] = w.astype(_BF)

    @pl.when(j == v_tile)
    def _():
        wvt_ref[...] = w.T.astype(_BF)


def _prepare(c, w_ada, b_ada, lam_qk, w_in, *, lam_init, v_off, v_w):
    bsz, d = c.shape
    n = w_ada.shape[1]
    n_in = w_in.shape[1]
    tn = d
    n_mod = n // tn
    assert v_off % v_w == 0 and n_in % v_w == 0 and n_in // v_w >= n_mod
    last = n_mod - 1
    return pl.pallas_call(
        functools.partial(_prep_kernel, lam_init=lam_init, n_mod=n_mod, v_tile=v_off // v_w),
        grid=(n_in // v_w,),
        in_specs=[pl.BlockSpec((bsz, d), lambda j: (0, 0)),
                  pl.BlockSpec((d, tn), lambda j: (0, jnp.minimum(j, last))),
                  pl.BlockSpec((1, tn), lambda j: (0, jnp.minimum(j, last))),
                  pl.BlockSpec(lam_qk.shape, lambda j: (0, 0)),
                  pl.BlockSpec((d, v_w), lambda j: (0, j))],
        out_specs=[pl.BlockSpec((bsz, tn), lambda j: (0, jnp.minimum(j, last))),
                   pl.BlockSpec((1, LANES), lambda j: (0, 0)),
                   pl.BlockSpec((d, v_w), lambda j: (0, j)),
                   pl.BlockSpec((v_w, d), lambda j: (0, 0))],
        out_shape=[jax.ShapeDtypeStruct((bsz, n), _F32), jax.ShapeDtypeStruct((1, LANES), _F32),
                   jax.ShapeDtypeStruct((d, n_in), _BF), jax.ShapeDtypeStruct((v_w, d), _BF)],
        compiler_params=pltpu.CompilerParams(
            dimension_semantics=("arbitrary",), vmem_limit_bytes=VMEM_LIMIT_BYTES),
        name="adaln_mod",
    )(c, w_ada, b_ada.reshape(1, n), lam_qk, w_in)


def _inproj_kernel(x_ref, mod_ref, gpre_ref, w_ref, wvt_ref, pos_ref, freq_ref, spread_ref,
                   qt_ref, k_ref, vt_ref, gates_ref, u_ref, *, d, qk_w, v_w, u_off, u_w):
    shift = mod_ref[0, :, 0:d]
    scale = mod_ref[0, :, d:2 * d]
    x = x_ref[0]
    inv = lax.rsqrt(jnp.mean(x * x, axis=-1, keepdims=True) + EPS)
    h = ((x * inv) * (gpre_ref[...] * (1.0 + scale)) + shift).astype(_BF)

    rows = x.shape[0]
    half = ROT_DIM // 2
    ang = freq_ref[...] * pos_ref[0]
    cs = jnp.concatenate([jnp.cos(ang), jnp.sin(ang)], axis=0)
    sel = spread_ref[...].astype(_BF)
    tab = None
    rest = cs
    for _ in range(3):
        piece = rest.astype(_BF)
        rest = rest - piece.astype(_F32)
        part = _dot(piece.astype(_F32).T.astype(_BF), sel)
        tab = part if tab is None else tab + part
    lane = lax.broadcasted_iota(jnp.int32, (1, LANES), 1)
    reps = qk_w // LANES
    rot_c = jnp.concatenate([jnp.where(lane % HEAD_DIM < ROT_DIM, tab[:, 0:LANES], 1.0)] * reps,
                            axis=1)
    rot_s = jnp.concatenate([tab[:, LANES:2 * LANES]] * reps, axis=1)
    low = lax.broadcasted_iota(jnp.int32, (1, qk_w), 1) % HEAD_DIM < half

    def rope(t):
        up = pltpu.roll(t, qk_w - half, 1)
        dn = pltpu.roll(t, half, 1)
        return t * rot_c + jnp.where(low, up, dn) * rot_s

    v_off = 2 * qk_w
    vt = _nt_dot(wvt_ref[...], h).astype(_BF)
    pad = (lax.broadcasted_iota(jnp.int32, (VT_PAD, rows), 0) == 0).astype(_BF)
    for hd in range(v_w // V_HEAD_DIM):
        vt_ref[0, hd, 0:V_HEAD_DIM, :] = vt[hd * V_HEAD_DIM:(hd + 1) * V_HEAD_DIM, :]
        vt_ref[0, hd, V_HEAD_DIM:V_HEAD_DIM + VT_PAD, :] = pad
    u_ref[0] = _dot(h, w_ref[:, u_off:u_off + u_w]).astype(_BF)
    qt = rope(_dot(h, w_ref[:, 0:qk_w]) * (HEAD_DIM ** -0.5 * LOG2_E)).T.astype(_BF)
    hw = 2 * HEAD_DIM
    for hd in range(qk_w // hw):
        qt_ref[0, hd] = qt[hd * hw:(hd + 1) * hw, :]
    k_ref[0] = rope(_dot(h, w_ref[:, qk_w:2 * qk_w])).astype(_BF)
    za_off = v_off + v_w
    z_a = _dot(h, w_ref[:, za_off:za_off + v_w])
    gates_ref[0, :, 0:v_w] = (z_a * _sigmoid(z_a)).astype(_BF)
    zs_off = u_off + u_w
    z_s = _dot(h, w_ref[:, zs_off:zs_off + u_w])
    gates_ref[0, :, v_w:v_w + u_w] = (z_s * _sigmoid(z_s)).astype(_BF)
    g_off = zs_off + u_w
    rest = w_ref.shape[1] - g_off
    piece = 512
    for p in range(rest // piece):
        gates_ref[0, :, v_w + u_w + p * piece:v_w + u_w + (p + 1) * piece] = _sigmoid(_dot(
            h, w_ref[:, g_off + p * piece:g_off + (p + 1) * piece])).astype(_BF)


def _inproj(x, mod3, g_pre, w_in_bf, wvt_bf, pos_rows, *, rows, qk_w, v_w, u_off, u_w):
    bsz, s, d = x.shape
    n_in = w_in_bf.shape[1]
    gates_w = n_in - 2 * qk_w - v_w - u_w
    n_heads = v_w // V_HEAD_DIM
    vt_rows = V_HEAD_DIM + VT_PAD
    kern = functools.partial(_inproj_kernel, d=d, qk_w=qk_w, v_w=v_w, u_off=u_off, u_w=u_w)
    freq, spread = _rope_constants()
    seq_spec = lambda w: pl.BlockSpec((1, rows, w), lambda b, t: (b, t, 0))
    return pl.pallas_call(
        kern,
        grid=(bsz, s // rows),
        in_specs=[seq_spec(d),
                  pl.BlockSpec((1, 1, 3 * d), lambda b, t: (b, 0, 0)),
                  pl.BlockSpec((1, d), lambda b, t: (0, 0)),
                  pl.BlockSpec((d, n_in), lambda b, t: (0, 0), pipeline_mode=pl.Buffered(1)),
                  pl.BlockSpec((v_w, d), lambda b, t: (0, 0), pipeline_mode=pl.Buffered(1)),
                  pl.BlockSpec((1, 1, rows), lambda b, t: (b * (s // rows) + t, 0, 0)),
                  pl.BlockSpec(freq.shape, lambda b, t: (0, 0)),
                  pl.BlockSpec(spread.shape, lambda b, t: (0, 0))],
        out_specs=[pl.BlockSpec((1, n_heads, 2 * HEAD_DIM, rows), lambda b, t: (b, 0, 0, t)),
                   seq_spec(qk_w),
                   pl.BlockSpec((1, n_heads, vt_rows, rows), lambda b, t: (b, 0, 0, t)),
                   seq_spec(gates_w), seq_spec(u_w)],
        out_shape=[jax.ShapeDtypeStruct((bsz, n_heads, 2 * HEAD_DIM, s), _BF),
                   jax.ShapeDtypeStruct((bsz, s, qk_w), _BF),
                   jax.ShapeDtypeStruct((bsz, n_heads, vt_rows, s), _BF),
                   jax.ShapeDtypeStruct((bsz, s, gates_w), _BF),
                   jax.ShapeDtypeStruct((bsz, s, u_w), _BF)],
        compiler_params=pltpu.CompilerParams(
            dimension_semantics=("arbitrary", "arbitrary"), vmem_limit_bytes=VMEM_LIMIT_BYTES),
        name="norm_inproj",
    )(x, mod3, g_pre.reshape(1, d), w_in_bf, wvt_bf, pos_rows, freq, spread)


def _attn_kernel(tab_ref, lam_ref, gsub_ref, qt_ref, k_ref, vt_ref, o_ref, *scratch, lam_init):
    s_ref = (scratch[0:2], scratch[2:4])
    e_ref = (scratch[4:6], scratch[6:8])
    m_ref = (scratch[8:10], scratch[10:12])
    ev_ref = (scratch[12:14], scratch[14:16])
    n_keys, tq = s_ref[0][0].shape
    tk = KEY_BLOCK
    n_kb = n_keys // tk
    step = pl.program_id(0)

    @pl.when(step == 0)
    def _():
        for ref in scratch:
            ref[...] = jnp.zeros_like(ref)

    lam = lam_ref[0:1, 0:1]

    def tick(new, old):
        ot = None
        for mp in range(2):
            ev = ev_ref[old][mp][...]
            nrm = ev[0:V_HEAD_DIM, :] * (1.0 / ev[V_HEAD_DIM:V_HEAD_DIM + 1, :])
            ot = nrm if ot is None else ot - lam * nrm
        inv = lax.rsqrt(jnp.mean(ot * ot, axis=0, keepdims=True) + EPS)
        o_ref[0] = (((ot * inv) * gsub_ref[...]) * (1.0 - lam_init)).T.astype(_BF)
        qt = qt_ref[0, 0]
        first = lax.broadcasted_iota(jnp.int32, (2 * HEAD_DIM, 1), 0) < HEAD_DIM
        zero = jnp.zeros_like(qt)
        qmt = (jnp.where(first, qt, zero), jnp.where(first, zero, qt))
        m_rows = [jnp.max(m_ref[old][mp][...], axis=0, keepdims=True) for mp in range(2)]
        n_part = 4 if n_kb % 4 == 0 else 1
        for mp in range(2):
            ev = None
            m = jnp.full((8, tq), -jnp.inf, _F32)
            for kb in range(n_kb):
                keys = slice(kb * tk, (kb + 1) * tk)
                if kb % (n_kb // n_part) == 0:
                    part_keys = slice(kb * tk, (kb + n_kb // n_part) * tk)
                    part = _dot(vt_ref[0, 0, :, part_keys], e_ref[new][mp][part_keys, :])
                    ev = part if ev is None else ev + part
                st = _dot(k_ref[0, keys, :], qmt[mp])
                s_ref[new][mp][keys, :] = st
                m = jnp.maximum(m, jnp.max(st.reshape(tk // 8, 8, tq), axis=0))
                e_ref[old][mp][keys, :] = jnp.exp2(s_ref[old][mp][keys, :] - m_rows[mp]).astype(_BF)
            m_ref[new][mp][...] = m
            ev_ref[new][mp][...] = ev

    @pl.when(step % 2 == 0)
    def _():
        tick(0, 1)

    @pl.when(step % 2 == 1)
    def _():
        tick(1, 0)


def _attention(lam, g_subln, qt, k, vt, *, tq, lam_init):
    bsz, n_heads, hw, s = qt.shape
    _, _, vt_rows, _ = vt.shape
    n_q = s // tq
    n_items = bsz * n_heads * n_q
    kern = functools.partial(_attn_kernel, lam_init=lam_init)
    n_steps = n_items + 3

    def item(j):
        j = np.clip(j, 0, n_items - 1)
        return j // (n_heads * n_q), (j // n_q) % n_heads, j % n_q

    g = np.arange(n_steps)
    tab = jnp.asarray(np.stack([*item(g), *item(g - 2), *item(g - 3)]).astype(np.int32))

    return pl.pallas_call(
        kern,
        grid_spec=pltpu.PrefetchScalarGridSpec(
            num_scalar_prefetch=1,
            grid=(n_steps,),
            in_specs=[pl.BlockSpec(lam.shape, lambda g, t: (0, 0)),
                      pl.BlockSpec((V_HEAD_DIM, 1), lambda g, t: (0, 0)),
                      pl.BlockSpec((1, 1, hw, tq), lambda g, t: (t[0, g], t[1, g], 0, t[2, g])),
                      pl.BlockSpec((1, s, hw), lambda g, t: (t[0, g], 0, t[1, g])),
                      pl.BlockSpec((1, 1, vt_rows, s), lambda g, t: (t[3, g], t[4, g], 0, 0))],
            out_specs=pl.BlockSpec((1, tq, V_HEAD_DIM), lambda g, t: (t[6, g], t[8, g], t[7, g])),
            scratch_shapes=([pltpu.VMEM((s, tq), _F32)] * 4 + [pltpu.VMEM((s, tq), _BF)] * 4
                            + [pltpu.VMEM((8, tq), _F32)] * 4 + [pltpu.VMEM((vt_rows, tq), _F32)] * 4)),
        out_shape=jax.ShapeDtypeStruct((bsz, s, n_heads * V_HEAD_DIM), _BF),
        compiler_params=pltpu.CompilerParams(
            dimension_semantics=("arbitrary",), vmem_limit_bytes=VMEM_LIMIT_BYTES),
        name="diff_attention",
    )(tab, lam, g_subln.reshape(V_HEAD_DIM, 1), qt, k, vt)


def _ssm_kernel(ut_ref, lam_ref, bt_ref, ct_ref, dsk_ref, yt_ref, at_ref, z_ref, xp_ref,
                *, bsz, n_c):
    el = SSM_CHUNK
    cw = el * SSM_GROUP
    sw = 2 * SSM_STATE
    n_cols = n_c * bsz
    in_group = [lax.broadcasted_iota(jnp.int32, (1, sw), 1) // SSM_STATE == gl for gl in range(2)]

    def group_rows(ref, gl):
        return ref[:, gl * SSM_GROUP:(gl + 1) * SSM_GROUP, :].reshape(cw, n_cols)

    def per_row(rows, cols):
        return (jnp.broadcast_to(rows[:, None, :], (el, SSM_GROUP, sw)),
                jnp.broadcast_to(cols[None, :, :], (el, SSM_GROUP, sw)))

    def cmul_rows(p_r, p_i, m_r, m_i):
        pr3, mr3 = per_row(p_r, m_r)
        pi3, mi3 = per_row(p_i, m_i)
        return (pr3 * mr3 - pi3 * mi3).reshape(cw, sw), (pr3 * mi3 + pi3 * mr3).reshape(cw, sw)

    idx = lax.broadcasted_iota(jnp.int32, (el, 1), 0).astype(_F32)
    ops = []
    for e in range(2):
        lr = jnp.minimum(lam_ref[0, e, 0:1, :], -1e-4)
        li = lam_ref[0, e, 1:2, :]
        dt = jnp.exp(lam_ref[0, e, 2:3, :])

        def power(k, lr=lr, li=li, dt=dt):
            mag = jnp.exp(k * (lr * dt))
            return mag * jnp.cos(k * (li * dt)), mag * jnp.sin(k * (li * dt))

        one_r, one_i = power(1.0)
        den = lr * lr + li * li
        nr, ni = one_r - 1.0, one_i
        coef_r = (nr * lr + ni * li) / den
        coef_i = (ni * lr - nr * li) / den
        b_r, b_i = bt_ref[0, e, 0], bt_ref[0, e, 1]
        bb_r = coef_r * b_r - coef_i * b_i
        bb_i = coef_r * b_i + coef_i * b_r
        c_r, c_i = ct_ref[0, e, 0], ct_ref[0, e, 1]
        ops.append(dict(power=power, bb=(bb_r, bb_i), c=(c_r, c_i)))
    fwd, bwd = ops

    zf_r, zf_i = cmul_rows(*fwd["power"](el - 1.0 - idx), *fwd["bb"])
    zb_r, zb_i = cmul_rows(*bwd["power"](idx), *bwd["bb"])
    yf_r, yf_i = cmul_rows(*fwd["power"](idx + 1.0), *fwd["c"])
    yb_r, yb_i = cmul_rows(*bwd["power"](el - idx), *bwd["c"])
    kf_r, kf_i = cmul_rows(*fwd["power"](idx), *fwd["c"])
    kb_r, kb_i = cmul_rows(*bwd["power"](idx), *bwd["c"])

    def pair_slot(parts, gl):
        return jnp.concatenate([jnp.where(in_group[gl], p, 0.0) for p in parts], axis=1).astype(_BF)

    for gl in range(2):
        at_ref[gl] = group_rows(ut_ref, gl).T
    z = (_dot(at_ref[0], pair_slot([zf_r, zf_i, zb_r, zb_i], 0))
         + _dot(at_ref[1], pair_slot([zf_r, zf_i, zb_r, zb_i], 1)))
    for comp in range(4):
        z_ref[comp] = z[:, comp * sw:(comp + 1) * sw]

    a_fr, a_fi = fwd["power"](float(el))
    a_br, a_bi = bwd["power"](float(el))

    def step(c, carry):
        fr, fi, br, bi = carry
        rows_f = pl.ds(pl.multiple_of(c * bsz, bsz), bsz)
        rows_b = pl.ds(pl.multiple_of((n_c - 1 - c) * bsz, bsz), bsz)
        xp_ref[0, rows_f, :] = fr
        xp_ref[1, rows_f, :] = fi
        xp_ref[2, rows_b, :] = br
        xp_ref[3, rows_b, :] = bi
        zfr = z_ref[0, rows_f, :]
        zfi = z_ref[1, rows_f, :]
        zbr = z_ref[2, rows_b, :]
        zbi = z_ref[3, rows_b, :]
        return (a_fr * fr - a_fi * fi + zfr, a_fr * fi + a_fi * fr + zfi,
                a_br * br - a_bi * bi + zbr, a_br * bi + a_bi * br + zbi)

    zero = jnp.zeros((bsz, sw), _F32)
    lax.fori_loop(0, n_c, step, (zero, zero, zero, zero), unroll=4)
    xp = jnp.concatenate([xp_ref[comp] for comp in range(4)], axis=1).astype(_BF)

    spread = (lax.broadcasted_iota(jnp.int32, (SSM_GROUP, cw), 1) % SSM_GROUP
              == lax.broadcasted_iota(jnp.int32, (SSM_GROUP, cw), 0)).astype(_BF)
    lag = (lax.broadcasted_iota(jnp.int32, (cw, cw), 0) // SSM_GROUP
           - lax.broadcasted_iota(jnp.int32, (cw, cw), 1) // SSM_GROUP)

    def lag_kernels(k_r, k_i, bb, gl):
        lhs = jnp.concatenate([jnp.where(in_group[gl], k_r, 0.0),
                               jnp.where(in_group[gl], -k_i, 0.0)], axis=1).astype(_BF)
        rhs = jnp.concatenate([bb[0], bb[1]], axis=1).astype(_BF)
        return _nt_dot(lhs, rhs)

    for gl in range(2):
        ktf = _dot(lag_kernels(kf_r, kf_i, fwd["bb"], gl).astype(_BF), spread)
        ktb = _dot(lag_kernels(kb_r, kb_i, bwd["bb"], gl).astype(_BF), spread)

        def lag_rows(kt, k):
            blk = kt[k * SSM_GROUP:(k + 1) * SSM_GROUP, :]
            return jnp.broadcast_to(blk[None], (el, SSM_GROUP, cw)).reshape(cw, cw)

        tm = jnp.where(lag == 0, lag_rows(ktf, 0) + lag_rows(ktb, 0), 0.0)
        for k in range(1, el):
            tm = jnp.where(lag == k, lag_rows(ktf, k), tm)
            tm = jnp.where(lag == -k, lag_rows(ktb, k), tm)
        u_g = group_rows(ut_ref, gl)
        yt = _dot(tm.astype(_BF), u_g) + _nt_dot(pair_slot([yf_r, -yf_i, yb_r, -yb_i], gl), xp)
        d_rows = jnp.broadcast_to(dsk_ref[0, gl][None], (el, SSM_GROUP, 1)).reshape(cw, 1)
        yt = yt + d_rows * u_g.astype(_F32)
        yt_ref[:, gl * SSM_GROUP:(gl + 1) * SSM_GROUP, :] = yt.reshape(
            el, SSM_GROUP, n_cols).astype(_BF)


def _ssm(ut, lam, bt, ct, dsk, *, bsz):
    _, u_w, n_cols = ut.shape
    n_c = n_cols // bsz
    n_pairs = u_w // (2 * SSM_GROUP)
    cw = SSM_CHUNK * SSM_GROUP
    sw = 2 * SSM_STATE
    kern = functools.partial(_ssm_kernel, bsz=bsz, n_c=n_c)
    blk = pl.BlockSpec((SSM_CHUNK, 2 * SSM_GROUP, n_cols), lambda g: (0, g, 0))
    par = pl.BlockSpec((1, 2, 2, SSM_GROUP, sw), lambda g: (g, 0, 0, 0, 0))
    return pl.pallas_call(
        kern,
        grid=(n_pairs,),
        in_specs=[blk,
                  pl.BlockSpec((1, 2, 3, sw), lambda g: (g, 0, 0, 0)),
                  par, par,
                  pl.BlockSpec((1, 2, SSM_GROUP, 1), lambda g: (g, 0, 0, 0))],
        out_specs=blk,
        out_shape=jax.ShapeDtypeStruct(ut.shape, _BF),
        scratch_shapes=[pltpu.VMEM((2, n_cols, cw), _BF),
                        pltpu.VMEM((4, n_cols, sw), _F32),
                        pltpu.VMEM((4, n_cols, sw), _F32)],
        compiler_params=pltpu.CompilerParams(
            dimension_semantics=("arbitrary",), vmem_limit_bytes=VMEM_LIMIT_BYTES),
        name="s5_chunked",
    )(ut, lam, bt, ct, dsk)


def _post_kernel(o_ref, gates_ref, y_ref, x_ref, mod_ref, bglu_ref, gfin_ref,
                 wua_ref, wglu_ref, wus_ref, wout_ref, out_ref, *, d, aw, sw):
    y = jax.nn.gelu(y_ref[0].astype(_F32))
    y = y * _sigmoid(_dot(y.astype(_BF), wglu_ref[...]) + bglu_ref[...])
    s_br = _dot((y * gates_ref[0, :, aw:aw + sw].astype(_F32)).astype(_BF), wus_ref[...])
    a_br = _dot((o_ref[0].astype(_F32) * gates_ref[0, :, 0:aw].astype(_F32)).astype(_BF),
                wua_ref[...])
    g_off = aw + sw
    merged = (gates_ref[0, :, g_off:g_off + d].astype(_F32) * a_br
              + gates_ref[0, :, g_off + d:g_off + 2 * d].astype(_F32) * s_br)
    r = _dot(merged.astype(_BF), wout_ref[...])
    xo = x_ref[0] + mod_ref[0, :, 2 * d:3 * d] * r
    inv = lax.rsqrt(jnp.mean(xo * xo, axis=-1, keepdims=True) + EPS)
    out_ref[0] = (xo * inv) * gfin_ref[...]


def _post(o, gates, y, x, mod3, b_glu, g_final, wua, wglu, wus, wout, *, rows):
    bsz, s, d = x.shape
    aw = o.shape[2]
    sw = y.shape[2]
    kern = functools.partial(_post_kernel, d=d, aw=aw, sw=sw)
    full = lambda a: pl.BlockSpec(a.shape, lambda b, t: (0,) * a.ndim)
    seq_spec = lambda w: pl.BlockSpec((1, rows, w), lambda b, t: (b, t, 0))
    bgl = b_glu.reshape(1, sw)
    gfi = g_final.reshape(1, d)
    return pl.pallas_call(
        kern,
        grid=(bsz, s // rows),
        in_specs=[seq_spec(aw), seq_spec(gates.shape[2]), seq_spec(sw), seq_spec(d),
                  pl.BlockSpec((1, 1, 3 * d), lambda b, t: (b, 0, 0)),
                  full(bgl), full(gfi), full(wua), full(wglu), full(wus), full(wout)],
        out_specs=seq_spec(d),
        out_shape=jax.ShapeDtypeStruct(x.shape, _F32),
        compiler_params=pltpu.CompilerParams(
            dimension_semantics=("arbitrary", "arbitrary"), vmem_limit_bytes=VMEM_LIMIT_BYTES),
        name="merge_out",
    )(o, gates, y, x, mod3, bgl, gfi, wua, wglu, wus, wout)


def _rope_constants():
    half = ROT_DIM // 2
    freq = (ROPE_THETA ** (-jnp.arange(half, dtype=_F32) * 2.0 / ROT_DIM)).reshape(half, 1)
    in_head = jnp.arange(LANES) % HEAD_DIM
    j = jnp.arange(half)[:, None]
    cos_sel = ((in_head == j) | (in_head == half + j)).astype(_F32)
    sin_sel = (in_head == half + j).astype(_F32) - (in_head == j).astype(_F32)
    zero = jnp.zeros_like(cos_sel)
    return freq, jnp.concatenate([jnp.concatenate([cos_sel, zero], axis=1),
                                  jnp.concatenate([zero, sin_sel], axis=1)], axis=0)


def _ssm_params(lam_re, lam_im, log_dt, b_re, b_im, c_re, c_im):
    n_g, n_p = lam_re.shape[1], lam_re.shape[2]
    pairs = n_g // 2

    def lanes(m):
        rows = m.shape[2]
        return m.reshape(2, pairs, 2, rows, n_p).transpose(1, 0, 3, 2, 4).reshape(pairs, 2, rows, 2 * n_p)

    lam = lanes(jnp.stack([lam_re, lam_im, jnp.broadcast_to(log_dt[..., None], lam_re.shape)],
                          axis=2).astype(_F32))
    bt = jnp.stack([lanes(jnp.swapaxes(b_re, -1, -2)), lanes(jnp.swapaxes(b_im, -1, -2))], axis=2)
    ct = jnp.stack([lanes(c_re), lanes(c_im)], axis=2)
    return lam, bt.astype(_F32), ct.astype(_F32)


def _layer(x, c, positions, layer_idx, w_ada, b_ada, g_pre, w_in, lam_qk, g_subln,
           lam_re, lam_im, log_dt, b_re, b_im, c_re, c_im, d_skip,
           w_glu, b_glu, w_up_attn, w_up_ssm, w_out, g_final):
    bsz, s, d = x.shape
    n_c = s // SSM_CHUNK
    qk_w = ATTN_HEADS * 2 * HEAD_DIM
    v_w = ATTN_HEADS * V_HEAD_DIM
    u_w = lam_re.shape[1] * SSM_GROUP
    u_off = 2 * qk_w + 2 * v_w
    lam_init = 0.8 - 0.6 * math.exp(-0.3 * layer_idx)
    mod, lam, w_in_bf, wvt_bf = _prepare(c, w_ada, b_ada, lam_qk, w_in, lam_init=lam_init,
                                         v_off=2 * qk_w, v_w=v_w)
    mod3 = mod.reshape(bsz, 1, 3 * d)
    rows = min(ROW_BLOCK, s)
    pos_rows = positions.astype(_F32).reshape(bsz * (s // rows), 1, rows)
    qt, k, vt, gates, u = _inproj(x, mod3, g_pre, w_in_bf, wvt_bf, pos_rows, rows=rows,
                                  qk_w=qk_w, v_w=v_w, u_off=u_off, u_w=u_w)
    o = _attention(lam, g_subln, qt, k, vt, tq=min(Q_ROWS, s), lam_init=lam_init)
    ut = u.reshape(bsz, n_c, SSM_CHUNK, u_w).transpose(2, 3, 1, 0).reshape(SSM_CHUNK, u_w, n_c * bsz)
    dsk = d_skip.astype(_F32).reshape(u_w // (2 * SSM_GROUP), 2, SSM_GROUP, 1)
    yt = _ssm(ut, *_ssm_params(lam_re, lam_im, log_dt, b_re, b_im, c_re, c_im), dsk, bsz=bsz)
    y = yt.reshape(SSM_CHUNK, u_w, n_c, bsz).transpose(3, 2, 0, 1).reshape(bsz, s, u_w)
    return _post(o, gates, y, x, mod3, b_glu, g_final,
                 w_up_attn.astype(_BF), w_glu.astype(_BF), w_up_ssm.astype(_BF), w_out.astype(_BF),
                 rows=rows)


def kernel(x, c, positions, w_ada, b_ada, g_pre, w_in, lam_qk, g_subln, ssm_lam_re, ssm_lam_im,
           ssm_log_dt, ssm_b_re, ssm_b_im, ssm_c_re, ssm_c_im, ssm_d, w_glu, b_glu, w_up_attn,
           w_up_ssm, w_out, g_final):
    depth = w_ada.shape[0]
    assert depth == 1, "the final RMSNorm is fused into the single layer's epilogue"
    assert x.shape[1] % (SSM_CHUNK * 8) == 0
    return _layer(x, c, positions, 0, w_ada[0], b_ada[0], g_pre[0], w_in[0],
                  lam_qk[0], g_subln[0], ssm_lam_re[0], ssm_lam_im[0], ssm_log_dt[0], ssm_b_re[0],
                  ssm_b_im[0], ssm_c_re[0], ssm_c_im[0], ssm_d[0], w_glu[0], b_glu[0],
                  w_up_attn[0], w_up_ssm[0], w_out[0], g_final)
```

```python
import functools
import math

import jax
import jax.numpy as jnp
import numpy as np
from jax import lax
from jax.experimental import pallas as pl
from jax.experimental.pallas import tpu as pltpu

ATTN_HEADS = 4
HEAD_DIM = 64
V_HEAD_DIM = 2 * HEAD_DIM
ROT_DIM = HEAD_DIM // 4
ROPE_THETA = 500000.0
SSM_GROUP = 16
SSM_STATE = 64
SSM_CHUNK = 16
EPS = 1e-6
LOG2_E = math.log2(math.e)
LANES = 128
VMEM_LIMIT_BYTES = 56 * 1024 * 1024
ROW_BLOCK = 1024
Q_ROWS = 512
KEY_BLOCK = 256
VT_PAD = 16

_HI = lax.Precision.HIGHEST
_BF = jnp.bfloat16
_F32 = jnp.float32


def _nt_dot(a, b):
    return lax.dot_general(a, b, (((1,), (1,)), ((), ())), preferred_element_type=_F32)


def _dot(a, b):
    return jnp.dot(a, b, preferred_element_type=_F32)


def _sigmoid(x):
    return 0.5 * jnp.tanh(0.5 * x) + 0.5


def _prep_kernel(c_ref, w_ref, b_ref, lamqk_ref, win_ref, o_ref, lam_ref, winb_ref, wvt_ref,
                 *, lam_init, n_mod, v_tile):
    j = pl.program_id(0)

    @pl.when(j < n_mod)
    def _():
        o_ref[...] = jnp.dot(jax.nn.silu(c_ref[...]), w_ref[...], precision=_HI,
                             preferred_element_type=_F32) + b_ref[...]

    @pl.when(j == 0)
    def _():
        lf = lamqk_ref[...]
        lam = (jnp.exp(jnp.sum(lf[0:1, :] * lf[1:2, :], axis=-1, keepdims=True))
               - jnp.exp(jnp.sum(lf[2:3, :] * lf[3:4, :], axis=-1, keepdims=True)) + lam_init)
        lam_ref[...] = jnp.broadcast_to(lam, lam_ref.shape)

    w = win_ref[...]
    winb_ref[...] = w.astype(_BF)

    @pl.when(j == v_tile)
    def _():
        wvt_ref[...] = w.T.astype(_BF)


def _prepare(c, w_ada, b_ada, lam_qk, w_in, *, lam_init, v_off, v_w):
    bsz, d = c.shape
    n = w_ada.shape[1]
    n_in = w_in.shape[1]
    tn = d
    n_mod = n // tn
    assert v_off % v_w == 0 and n_in % v_w == 0 and n_in // v_w >= n_mod
    last = n_mod - 1
    return pl.pallas_call(
        functools.partial(_prep_kernel, lam_init=lam_init, n_mod=n_mod, v_tile=v_off // v_w),
        grid=(n_in // v_w,),
        in_specs=[pl.BlockSpec((bsz, d), lambda j: (0, 0)),
                  pl.BlockSpec((d, tn), lambda j: (0, jnp.minimum(j, last))),
                  pl.BlockSpec((1, tn), lambda j: (0, jnp.minimum(j, last))),
                  pl.BlockSpec(lam_qk.shape, lambda j: (0, 0)),
                  pl.BlockSpec((d, v_w), lambda j: (0, j))],
        out_specs=[pl.BlockSpec((bsz, tn), lambda j: (0, jnp.minimum(j, last))),
                   pl.BlockSpec((1, LANES), lambda j: (0, 0)),
                   pl.BlockSpec((d, v_w), lambda j: (0, j)),
                   pl.BlockSpec((v_w, d), lambda j: (0, 0))],
        out_shape=[jax.ShapeDtypeStruct((bsz, n), _F32), jax.ShapeDtypeStruct((1, LANES), _F32),
                   jax.ShapeDtypeStruct((d, n_in), _BF), jax.ShapeDtypeStruct((v_w, d), _BF)],
        compiler_params=pltpu.CompilerParams(
            dimension_semantics=("arbitrary",), vmem_limit_bytes=VMEM_LIMIT_BYTES),
        name="adaln_mod",
    )(c, w_ada, b_ada.reshape(1, n), lam_qk, w_in)


def _inproj_kernel(x_ref, mod_ref, gpre_ref, w_ref, wvt_ref, pos_ref, freq_ref, spread_ref,
                   qt_ref, k_ref, vt_ref, gates_ref, u_ref, *, d, qk_w, v_w, u_off, u_w):
    shift = mod_ref[0, :, 0:d]
    scale = mod_ref[0, :, d:2 * d]
    x = x_ref[0]
    inv = lax.rsqrt(jnp.mean(x * x, axis=-1, keepdims=True) + EPS)
    h = ((x * inv) * (gpre_ref[...] * (1.0 + scale)) + shift).astype(_BF)

    rows = x.shape[0]
    half = ROT_DIM // 2
    ang = freq_ref[...] * pos_ref[0]
    cs = jnp.concatenate([jnp.cos(ang), jnp.sin(ang)], axis=0)
    sel = spread_ref[...].astype(_BF)
    tab = None
    rest = cs
    for _ in range(3):
        piece = rest.astype(_BF)
        rest = rest - piece.astype(_F32)
        part = _dot(piece.astype(_F32).T.astype(_BF), sel)
        tab = part if tab is None else tab + part
    lane = lax.broadcasted_iota(jnp.int32, (1, LANES), 1)
    reps = qk_w // LANES
    rot_c = jnp.concatenate([jnp.where(lane % HEAD_DIM < ROT_DIM, tab[:, 0:LANES], 1.0)] * reps,
                            axis=1)
    rot_s = jnp.concatenate([tab[:, LANES:2 * LANES]] * reps, axis=1)
    low = lax.broadcasted_iota(jnp.int32, (1, qk_w), 1) % HEAD_DIM < half

    def rope(t):
        up = pltpu.roll(t, qk_w - half, 1)
        dn = pltpu.roll(t, half, 1)
        return t * rot_c + jnp.where(low, up, dn) * rot_s

    v_off = 2 * qk_w
    vt = _nt_dot(wvt_ref[...], h).astype(_BF)
    pad = (lax.broadcasted_iota(jnp.int32, (VT_PAD, rows), 0) == 0).astype(_BF)
    for hd in range(v_w // V_HEAD_DIM):
        vt_ref[0, hd, 0:V_HEAD_DIM, :] = vt[hd * V_HEAD_DIM:(hd + 1) * V_HEAD_DIM, :]
        vt_ref[0, hd, V_HEAD_DIM:V_HEAD_DIM + VT_PAD, :] = pad
    u_ref[0] = _dot(h, w_ref[:, u_off:u_off + u_w]).astype(_BF)
    qt = rope(_dot(h, w_ref[:, 0:qk_w]) * (HEAD_DIM ** -0.5 * LOG2_E)).T.astype(_BF)
    hw = 2 * HEAD_DIM
    for hd in range(qk_w // hw):
        qt_ref[0, hd] = qt[hd * hw:(hd + 1) * hw, :]
    k_ref[0] = rope(_dot(h, w_ref[:, qk_w:2 * qk_w])).astype(_BF)
    za_off = v_off + v_w
    z_a = _dot(h, w_ref[:, za_off:za_off + v_w])
    gates_ref[0, :, 0:v_w] = (z_a * _sigmoid(z_a)).astype(_BF)
    zs_off = u_off + u_w
    z_s = _dot(h, w_ref[:, zs_off:zs_off + u_w])
    gates_ref[0, :, v_w:v_w + u_w] = (z_s * _sigmoid(z_s)).astype(_BF)
    g_off = zs_off + u_w
    rest = w_ref.shape[1] - g_off
    piece = 512
    for p in range(rest // piece):
        gates_ref[0, :, v_w + u_w + p * piece:v_w + u_w + (p + 1) * piece] = _sigmoid(_dot(
            h, w_ref[:, g_off + p * piece:g_off + (p + 1) * piece])).astype(_BF)


def _inproj(x, mod3, g_pre, w_in_bf, wvt_bf, pos_rows, *, rows, qk_w, v_w, u_off, u_w):
    bsz, s, d = x.shape
    n_in = w_in_bf.shape[1]
    gates_w = n_in - 2 * qk_w - v_w - u_w
    n_heads = v_w // V_HEAD_DIM
    vt_rows = V_HEAD_DIM + VT_PAD
    kern = functools.partial(_inproj_kernel, d=d, qk_w=qk_w, v_w=v_w, u_off=u_off, u_w=u_w)
    freq, spread = _rope_constants()
    seq_spec = lambda w: pl.BlockSpec((1, rows, w), lambda b, t: (b, t, 0))
    return pl.pallas_call(
        kern,
        grid=(bsz, s // rows),
        in_specs=[seq_spec(d),
                  pl.BlockSpec((1, 1, 3 * d), lambda b, t: (b, 0, 0)),
                  pl.BlockSpec((1, d), lambda b, t: (0, 0)),
                  pl.BlockSpec((d, n_in), lambda b, t: (0, 0), pipeline_mode=pl.Buffered(1)),
                  pl.BlockSpec((v_w, d), lambda b, t: (0, 0), pipeline_mode=pl.Buffered(1)),
                  pl.BlockSpec((1, 1, rows), lambda b, t: (b * (s // rows) + t, 0, 0)),
                  pl.BlockSpec(freq.shape, lambda b, t: (0, 0)),
                  pl.BlockSpec(spread.shape, lambda b, t: (0, 0))],
        out_specs=[pl.BlockSpec((1, n_heads, 2 * HEAD_DIM, rows), lambda b, t: (b, 0, 0, t)),
                   seq_spec(qk_w),
                   pl.BlockSpec((1, n_heads, vt_rows, rows), lambda b, t: (b, 0, 0, t)),
                   seq_spec(gates_w), seq_spec(u_w)],
        out_shape=[jax.ShapeDtypeStruct((bsz, n_heads, 2 * HEAD_DIM, s), _BF),
                   jax.ShapeDtypeStruct((bsz, s, qk_w), _BF),
                   jax.ShapeDtypeStruct((bsz, n_heads, vt_rows, s), _BF),
                   jax.ShapeDtypeStruct((bsz, s, gates_w), _BF),
                   jax.ShapeDtypeStruct((bsz, s, u_w), _BF)],
        compiler_params=pltpu.CompilerParams(
            dimension_semantics=("arbitrary", "arbitrary"), vmem_limit_bytes=VMEM_LIMIT_BYTES),
        name="norm_inproj",
    )(x, mod3, g_pre.reshape(1, d), w_in_bf, wvt_bf, pos_rows, freq, spread)


def _attn_kernel(tab_ref, lam_ref, gsub_ref, qt_ref, k_ref, vt_ref, o_ref, *scratch, lam_init):
    s_ref = (scratch[0:2], scratch[2:4])
    e_ref = (scratch[4:6], scratch[6:8])
    m_ref = (scratch[8:10], scratch[10:12])
    ev_ref = (scratch[12:14], scratch[14:16])
    n_keys, tq = s_ref[0][0].shape
    tk = KEY_BLOCK
    n_kb = n_keys // tk
    step = pl.program_id(0)

    @pl.when(step == 0)
    def _():
        for ref in scratch:
            ref[...] = jnp.zeros_like(ref)

    lam = lam_ref[0:1, 0:1]

    def tick(new, old):
        ot = None
        for mp in range(2):
            ev = ev_ref[old][mp][...]
            nrm = ev[0:V_HEAD_DIM, :] * (1.0 / ev[V_HEAD_DIM:V_HEAD_DIM + 1, :])
            ot = nrm if ot is None else ot - lam * nrm
        inv = lax.rsqrt(jnp.mean(ot * ot, axis=0, keepdims=True) + EPS)
        o_ref[0] = (((ot * inv) * gsub_ref[...]) * (1.0 - lam_init)).T.astype(_BF)
        qt = qt_ref[0, 0]
        first = lax.broadcasted_iota(jnp.int32, (2 * HEAD_DIM, 1), 0) < HEAD_DIM
        zero = jnp.zeros_like(qt)
        qmt = (jnp.where(first, qt, zero), jnp.where(first, zero, qt))
        m_rows = [jnp.max(m_ref[old][mp][...], axis=0, keepdims=True) for mp in range(2)]
        n_part = 4 if n_kb % 4 == 0 else 1
        for mp in range(2):
            ev = None
            m = jnp.full((8, tq), -jnp.inf, _F32)
            for kb in range(n_kb):
                keys = slice(kb * tk, (kb + 1) * tk)
                if kb % (n_kb // n_part) == 0:
                    part_keys = slice(kb * tk, (kb + n_kb // n_part) * tk)
                    part = _dot(vt_ref[0, 0, :, part_keys], e_ref[new][mp][part_keys, :])
                    ev = part if ev is None else ev + part
                e_ref[old][mp][keys, :] = jnp.exp2(s_ref[old][mp][keys, :] - m_rows[mp]).astype(_BF)
                st = _dot(k_ref[0, keys, :], qmt[mp])
                s_ref[new][mp][keys, :] = st
                m = jnp.maximum(m, jnp.max(st.reshape(tk // 8, 8, tq), axis=0))
            m_ref[new][mp][...] = m
            ev_ref[new][mp][...] = ev

    @pl.when(step % 2 == 0)
    def _():
        tick(0, 1)

    @pl.when(step % 2 == 1)
    def _():
        tick(1, 0)


def _attention(lam, g_subln, qt, k, vt, *, tq, lam_init):
    bsz, n_heads, hw, s = qt.shape
    _, _, vt_rows, _ = vt.shape
    n_q = s // tq
    n_items = bsz * n_heads * n_q
    kern = functools.partial(_attn_kernel, lam_init=lam_init)
    n_steps = n_items + 3

    def item(j):
        j = np.clip(j, 0, n_items - 1)
        return j // (n_heads * n_q), (j // n_q) % n_heads, j % n_q

    g = np.arange(n_steps)
    tab = jnp.asarray(np.stack([*item(g), *item(g - 2), *item(g - 3)]).astype(np.int32))

    return pl.pallas_call(
        kern,
        grid_spec=pltpu.PrefetchScalarGridSpec(
            num_scalar_prefetch=1,
            grid=(n_steps,),
            in_specs=[pl.BlockSpec(lam.shape, lambda g, t: (0, 0)),
                      pl.BlockSpec((V_HEAD_DIM, 1), lambda g, t: (0, 0)),
                      pl.BlockSpec((1, 1, hw, tq), lambda g, t: (t[0, g], t[1, g], 0, t[2, g])),
                      pl.BlockSpec((1, s, hw), lambda g, t: (t[0, g], 0, t[1, g])),
                      pl.BlockSpec((1, 1, vt_rows, s), lambda g, t: (t[3, g], t[4, g], 0, 0))],
            out_specs=pl.BlockSpec((1, tq, V_HEAD_DIM), lambda g, t: (t[6, g], t[8, g], t[7, g])),
            scratch_shapes=([pltpu.VMEM((s, tq), _F32)] * 4 + [pltpu.VMEM((s, tq), _BF)] * 4
                            + [pltpu.VMEM((8, tq), _F32)] * 4 + [pltpu.VMEM((vt_rows, tq), _F32)] * 4)),
        out_shape=jax.ShapeDtypeStruct((bsz, s, n_heads * V_HEAD_DIM), _BF),
        compiler_params=pltpu.CompilerParams(
            dimension_semantics=("arbitrary",), vmem_limit_bytes=VMEM_LIMIT_BYTES),
        name="diff_attention",
    )(tab, lam, g_subln.reshape(V_HEAD_DIM, 1), qt, k, vt)


def _ssm_kernel(ut_ref, lam_ref, bt_ref, ct_ref, dsk_ref, yt_ref, at_ref, z_ref, xp_ref,
                *, bsz, n_c):
    el = SSM_CHUNK
    cw = el * SSM_GROUP
    sw = 2 * SSM_STATE
    n_cols = n_c * bsz
    in_group = [lax.broadcasted_iota(jnp.int32, (1, sw), 1) // SSM_STATE == gl for gl in range(2)]

    def group_rows(ref, gl):
        return ref[:, gl * SSM_GROUP:(gl + 1) * SSM_GROUP, :].reshape(cw, n_cols)

    def per_row(rows, cols):
        return (jnp.broadcast_to(rows[:, None, :], (el, SSM_GROUP, sw)),
                jnp.broadcast_to(cols[None, :, :], (el, SSM_GROUP, sw)))

    def cmul_rows(p_r, p_i, m_r, m_i):
        pr3, mr3 = per_row(p_r, m_r)
        pi3, mi3 = per_row(p_i, m_i)
        return (pr3 * mr3 - pi3 * mi3).reshape(cw, sw), (pr3 * mi3 + pi3 * mr3).reshape(cw, sw)

    idx = lax.broadcasted_iota(jnp.int32, (el, 1), 0).astype(_F32)
    ops = []
    for e in range(2):
        lr = jnp.minimum(lam_ref[0, e, 0:1, :], -1e-4)
        li = lam_ref[0, e, 1:2, :]
        dt = jnp.exp(lam_ref[0, e, 2:3, :])

        def power(k, lr=lr, li=li, dt=dt):
            mag = jnp.exp(k * (lr * dt))
            return mag * jnp.cos(k * (li * dt)), mag * jnp.sin(k * (li * dt))

        one_r, one_i = power(1.0)
        den = lr * lr + li * li
        nr, ni = one_r - 1.0, one_i
        coef_r = (nr * lr + ni * li) / den
        coef_i = (ni * lr - nr * li) / den
        b_r, b_i = bt_ref[0, e, 0], bt_ref[0, e, 1]
        bb_r = coef_r * b_r - coef_i * b_i
        bb_i = coef_r * b_i + coef_i * b_r
        c_r, c_i = ct_ref[0, e, 0], ct_ref[0, e, 1]
        ops.append(dict(power=power, bb=(bb_r, bb_i), c=(c_r, c_i)))
    fwd, bwd = ops

    zf_r, zf_i = cmul_rows(*fwd["power"](el - 1.0 - idx), *fwd["bb"])
    zb_r, zb_i = cmul_rows(*bwd["power"](idx), *bwd["bb"])
    yf_r, yf_i = cmul_rows(*fwd["power"](idx + 1.0), *fwd["c"])
    yb_r, yb_i = cmul_rows(*bwd["power"](el - idx), *bwd["c"])
    kf_r, kf_i = cmul_rows(*fwd["power"](idx), *fwd["c"])
    kb_r, kb_i = cmul_rows(*bwd["power"](idx), *bwd["c"])

    def pair_slot(parts, gl):
        return jnp.concatenate([jnp.where(in_group[gl], p, 0.0) for p in parts], axis=1).astype(_BF)

    for gl in range(2):
        at_ref[gl] = group_rows(ut_ref, gl).T
    z = (_dot(at_ref[0], pair_slot([zf_r, zf_i, zb_r, zb_i], 0))
         + _dot(at_ref[1], pair_slot([zf_r, zf_i, zb_r, zb_i], 1)))
    for comp in range(4):
        z_ref[comp] = z[:, comp * sw:(comp + 1) * sw]

    a_fr, a_fi = fwd["power"](float(el))
    a_br, a_bi = bwd["power"](float(el))

    def step(c, carry):
        fr, fi, br, bi = carry
        rows_f = pl.ds(pl.multiple_of(c * bsz, bsz), bsz)
        rows_b = pl.ds(pl.multiple_of((n_c - 1 - c) * bsz, bsz), bsz)
        xp_ref[0, rows_f, :] = fr
        xp_ref[1, rows_f, :] = fi
        xp_ref[2, rows_b, :] = br
        xp_ref[3, rows_b, :] = bi
        zfr = z_ref[0, rows_f, :]
        zfi = z_ref[1, rows_f, :]
        zbr = z_ref[2, rows_b, :]
        zbi = z_ref[3, rows_b, :]
        return (a_fr * fr - a_fi * fi + zfr, a_fr * fi + a_fi * fr + zfi,
                a_br * br - a_bi * bi + zbr, a_br * bi + a_bi * br + zbi)

    zero = jnp.zeros((bsz, sw), _F32)
    lax.fori_loop(0, n_c, step, (zero, zero, zero, zero), unroll=4)
    xp = jnp.concatenate([xp_ref[comp] for comp in range(4)], axis=1).astype(_BF)

    spread = (lax.broadcasted_iota(jnp.int32, (SSM_GROUP, cw), 1) % SSM_GROUP
              == lax.broadcasted_iota(jnp.int32, (SSM_GROUP, cw), 0)).astype(_BF)
    lag = (lax.broadcasted_iota(jnp.int32, (cw, cw), 0) // SSM_GROUP
           - lax.broadcasted_iota(jnp.int32, (cw, cw), 1) // SSM_GROUP)

    def lag_kernels(k_r, k_i, bb, gl):
        lhs = jnp.concatenate([jnp.where(in_group[gl], k_r, 0.0),
                               jnp.where(in_group[gl], -k_i, 0.0)], axis=1).astype(_BF)
        rhs = jnp.concatenate([bb[0], bb[1]], axis=1).astype(_BF)
        return _nt_dot(lhs, rhs)

    for gl in range(2):
        ktf = _dot(lag_kernels(kf_r, kf_i, fwd["bb"], gl).astype(_BF), spread)
        ktb = _dot(lag_kernels(kb_r, kb_i, bwd["bb"], gl).astype(_BF), spread)

        def lag_rows(kt, k):
            blk = kt[k * SSM_GROUP:(k + 1) * SSM_GROUP, :]
            return jnp.broadcast_to(blk[None], (el, SSM_GROUP, cw)).reshape(cw, cw)

        tm = jnp.where(lag == 0, lag_rows(ktf, 0) + lag_rows(ktb, 0), 0.0)
        for k in range(1, el):
            tm = jnp.where(lag == k, lag_rows(ktf, k), tm)
            tm = jnp.where(lag == -k, lag_rows(ktb, k), tm)
        u_g = group_rows(ut_ref, gl)
        yt = _dot(tm.astype(_BF), u_g) + _nt_dot(pair_slot([yf_r, -yf_i, yb_r, -yb_i], gl), xp)
        d_rows = jnp.broadcast_to(dsk_ref[0, gl][None], (el, SSM_GROUP, 1)).reshape(cw, 1)
        yt = yt + d_rows * u_g.astype(_F32)
        yt_ref[:, gl * SSM_GROUP:(gl + 1) * SSM_GROUP, :] = yt.reshape(
            el, SSM_GROUP, n_cols).astype(_BF)


def _ssm(ut, lam, bt, ct, dsk, *, bsz):
    _, u_w, n_cols = ut.shape
    n_c = n_cols // bsz
    n_pairs = u_w // (2 * SSM_GROUP)
    cw = SSM_CHUNK * SSM_GROUP
    sw = 2 * SSM_STATE
    kern = functools.partial(_ssm_kernel, bsz=bsz, n_c=n_c)
    blk = pl.BlockSpec((SSM_CHUNK, 2 * SSM_GROUP, n_cols), lambda g: (0, g, 0))
    par = pl.BlockSpec((1, 2, 2, SSM_GROUP, sw), lambda g: (g, 0, 0, 0, 0))
    return pl.pallas_call(
        kern,
        grid=(n_pairs,),
        in_specs=[blk,
                  pl.BlockSpec((1, 2, 3, sw), lambda g: (g, 0, 0, 0)),
                  par, par,
                  pl.BlockSpec((1, 2, SSM_GROUP, 1), lambda g: (g, 0, 0, 0))],
        out_specs=blk,
        out_shape=jax.ShapeDtypeStruct(ut.shape, _BF),
        scratch_shapes=[pltpu.VMEM((2, n_cols, cw), _BF),
                        pltpu.VMEM((4, n_cols, sw), _F32),
                        pltpu.VMEM((4, n_cols, sw), _F32)],
        compiler_params=pltpu.CompilerParams(
            dimension_semantics=("arbitrary",), vmem_limit_bytes=VMEM_LIMIT_BYTES),
        name="s5_chunked",
    )(ut, lam, bt, ct, dsk)


def _post_kernel(o_ref, gates_ref, y_ref, x_ref, mod_ref, bglu_ref, gfin_ref,
                 wua_ref, wglu_ref, wus_ref, wout_ref, out_ref, *, d, aw, sw):
    y = jax.nn.gelu(y_ref[0].astype(_F32))
    y = y * _sigmoid(_dot(y.astype(_BF), wglu_ref[...]) + bglu_ref[...])
    s_br = _dot((y * gates_ref[0, :, aw:aw + sw].astype(_F32)).astype(_BF), wus_ref[...])
    a_br = _dot((o_ref[0].astype(_F32) * gates_ref[0, :, 0:aw].astype(_F32)).astype(_BF),
                wua_ref[...])
    g_off = aw + sw
    merged = (gates_ref[0, :, g_off:g_off + d].astype(_F32) * a_br
              + gates_ref[0, :, g_off + d:g_off + 2 * d].astype(_F32) * s_br)
    r = _dot(merged.astype(_BF), wout_ref[...])
    xo = x_ref[0] + mod_ref[0, :, 2 * d:3 * d] * r
    inv = lax.rsqrt(jnp.mean(xo * xo, axis=-1, keepdims=True) + EPS)
    out_ref[0] = (xo * inv) * gfin_ref[...]


def _post(o, gates, y, x, mod3, b_glu, g_final, wua, wglu, wus, wout, *, rows):
    bsz, s, d = x.shape
    aw = o.shape[2]
    sw = y.shape[2]
    kern = functools.partial(_post_kernel, d=d, aw=aw, sw=sw)
    full = lambda a: pl.BlockSpec(a.shape, lambda b, t: (0,) * a.ndim)
    seq_spec = lambda w: pl.BlockSpec((1, rows, w), lambda b, t: (b, t, 0))
    bgl = b_glu.reshape(1, sw)
    gfi = g_final.reshape(1, d)
    return pl.pallas_call(
        kern,
        grid=(bsz, s // rows),
        in_specs=[seq_spec(aw), seq_spec(gates.shape[2]), seq_spec(sw), seq_spec(d),
                  pl.BlockSpec((1, 1, 3 * d), lambda b, t: (b, 0, 0)),
                  full(bgl), full(gfi), full(wua), full(wglu), full(wus), full(wout)],
        out_specs=seq_spec(d),
        out_shape=jax.ShapeDtypeStruct(x.shape, _F32),
        compiler_params=pltpu.CompilerParams(
            dimension_semantics=("arbitrary", "arbitrary"), vmem_limit_bytes=VMEM_LIMIT_BYTES),
        name="merge_out",
    )(o, gates, y, x, mod3, bgl, gfi, wua, wglu, wus, wout)


def _rope_constants():
    half = ROT_DIM // 2
    freq = (ROPE_THETA ** (-jnp.arange(half, dtype=_F32) * 2.0 / ROT_DIM)).reshape(half, 1)
    in_head = jnp.arange(LANES) % HEAD_DIM
    j = jnp.arange(half)[:, None]
    cos_sel = ((in_head == j) | (in_head == half + j)).astype(_F32)
    sin_sel = (in_head == half + j).astype(_F32) - (in_head == j).astype(_F32)
    zero = jnp.zeros_like(cos_sel)
    return freq, jnp.concatenate([jnp.concatenate([cos_sel, zero], axis=1),
                                  jnp.concatenate([zero, sin_sel], axis=1)], axis=0)


def _ssm_params(lam_re, lam_im, log_dt, b_re, b_im, c_re, c_im):
    n_g, n_p = lam_re.shape[1], lam_re.shape[2]
    pairs = n_g // 2

    def lanes(m):
        rows = m.shape[2]
        return m.reshape(2, pairs, 2, rows, n_p).transpose(1, 0, 3, 2, 4).reshape(pairs, 2, rows, 2 * n_p)

    lam = lanes(jnp.stack([lam_re, lam_im, jnp.broadcast_to(log_dt[..., None], lam_re.shape)],
                          axis=2).astype(_F32))
    bt = jnp.stack([lanes(jnp.swapaxes(b_re, -1, -2)), lanes(jnp.swapaxes(b_im, -1, -2))], axis=2)
    ct = jnp.stack([lanes(c_re), lanes(c_im)], axis=2)
    return lam, bt.astype(_F32), ct.astype(_F32)


def _layer(x, c, positions, layer_idx, w_ada, b_ada, g_pre, w_in, lam_qk, g_subln,
           lam_re, lam_im, log_dt, b_re, b_im, c_re, c_im, d_skip,
           w_glu, b_glu, w_up_attn, w_up_ssm, w_out, g_final):
    bsz, s, d = x.shape
    n_c = s // SSM_CHUNK
    qk_w = ATTN_HEADS * 2 * HEAD_DIM
    v_w = ATTN_HEADS * V_HEAD_DIM
    u_w = lam_re.shape[1] * SSM_GROUP
    u_off = 2 * qk_w + 2 * v_w
    lam_init = 0.8 - 0.6 * math.exp(-0.3 * layer_idx)
    mod, lam, w_in_bf, wvt_bf = _prepare(c, w_ada, b_ada, lam_qk, w_in, lam_init=lam_init,
                                         v_off=2 * qk_w, v_w=v_w)
    mod3 = mod.reshape(bsz, 1, 3 * d)
    rows = min(ROW_BLOCK, s)
    pos_rows = positions.astype(_F32).reshape(bsz * (s // rows), 1, rows)
    qt, k, vt, gates, u = _inproj(x, mod3, g_pre, w_in_bf, wvt_bf, pos_rows, rows=rows,
                                  qk_w=qk_w, v_w=v_w, u_off=u_off, u_w=u_w)
    o = _attention(lam, g_subln, qt, k, vt, tq=min(Q_ROWS, s), lam_init=lam_init)
    ut = u.reshape(bsz, n_c, SSM_CHUNK, u_w).transpose(2, 3, 1, 0).reshape(SSM_CHUNK, u_w, n_c * bsz)
    dsk = d_skip.astype(_F32).reshape(u_w // (2 * SSM_GROUP), 2, SSM_GROUP, 1)
    yt = _ssm(ut, *_ssm_params(lam_re, lam_im, log_dt, b_re, b_im, c_re, c_im), dsk, bsz=bsz)
    y = yt.reshape(SSM_CHUNK, u_w, n_c, bsz).transpose(3, 2, 0, 1).reshape(bsz, s, u_w)
    return _post(o, gates, y, x, mod3, b_glu, g_final,
                 w_up_attn.astype(_BF), w_glu.astype(_BF), w_up_ssm.astype(_BF), w_out.astype(_BF),
                 rows=rows)


def kernel(x, c, positions, w_ada, b_ada, g_pre, w_in, lam_qk, g_subln, ssm_lam_re, ssm_lam_im,
           ssm_log_dt, ssm_b_re, ssm_b_im, ssm_c_re, ssm_c_im, ssm_d, w_glu, b_glu, w_up_attn,
           w_up_ssm, w_out, g_final):
    depth = w_ada.shape[0]
    assert depth == 1, "the final RMSNorm is fused into the single layer's epilogue"
    assert x.shape[1] % (SSM_CHUNK * 8) == 0
    return _layer(x, c, positions, 0, w_ada[0], b_ada[0], g_pre[0], w_in[0],
                  lam_qk[0], g_subln[0], ssm_lam_re[0], ssm_lam_im[0], ssm_log_dt[0], ssm_b_re[0],
                  ssm_b_im[0], ssm_c_re[0], ssm_c_im[0], ssm_d[0], w_glu[0], b_glu[0],
                  w_up_attn[0], w_up_ssm[0], w_out[0], g_final)
```

```python
import functools
import math

import jax
import jax.numpy as jnp
import numpy as np
from jax import lax
from jax.experimental import pallas as pl
from jax.experimental.pallas import tpu as pltpu

ATTN_HEADS = 4
HEAD_DIM = 64
V_HEAD_DIM = 2 * HEAD_DIM
ROT_DIM = HEAD_DIM // 4
ROPE_THETA = 500000.0
SSM_GROUP = 16
SSM_STATE = 64
SSM_CHUNK = 16
EPS = 1e-6
LOG2_E = math.log2(math.e)
LANES = 128
VMEM_LIMIT_BYTES = 56 * 1024 * 1024
ROW_BLOCK = 1024
Q_ROWS = 512
KEY_BLOCK = 256
VT_PAD = 16

_HI = lax.Precision.HIGHEST
_BF = jnp.bfloat16
_F32 = jnp.float32


def _nt_dot(a, b):
    return lax.dot_general(a, b, (((1,), (1,)), ((), ())), preferred_element_type=_F32)


def _dot(a, b):
    return jnp.dot(a, b, preferred_element_type=_F32)


def _sigmoid(x):
    return 0.5 * jnp.tanh(0.5 * x) + 0.5


def _prep_kernel(c_ref, w_ref, b_ref, lamqk_ref, win_ref, o_ref, lam_ref, winb_ref, wvt_ref,
                 *, lam_init, n_mod, v_tile):
    j = pl.program_id(0)

    @pl.when(j < n_mod)
    def _():
        o_ref[...] = jnp.dot(jax.nn.silu(c_ref[...]), w_ref[...], precision=_HI,
                             preferred_element_type=_F32) + b_ref[...]

    @pl.when(j == 0)
    def _():
        lf = lamqk_ref[...]
        lam = (jnp.exp(jnp.sum(lf[0:1, :] * lf[1:2, :], axis=-1, keepdims=True))
               - jnp.exp(jnp.sum(lf[2:3, :] * lf[3:4, :], axis=-1, keepdims=True)) + lam_init)
        lam_ref[...] = jnp.broadcast_to(lam, lam_ref.shape)

    w = win_ref[...]
    winb_ref[...] = w.astype(_BF)

    @pl.when(j == v_tile)
    def _():
        wvt_ref[...] = w.T.astype(_BF)


def _prepare(c, w_ada, b_ada, lam_qk, w_in, *, lam_init, v_off, v_w):
    bsz, d = c.shape
    n = w_ada.shape[1]
    n_in = w_in.shape[1]
    tn = d
    n_mod = n // tn
    assert v_off % v_w == 0 and n_in % v_w == 0 and n_in // v_w >= n_mod
    last = n_mod - 1
    return pl.pallas_call(
        functools.partial(_prep_kernel, lam_init=lam_init, n_mod=n_mod, v_tile=v_off // v_w),
        grid=(n_in // v_w,),
        in_specs=[pl.BlockSpec((bsz, d), lambda j: (0, 0)),
                  pl.BlockSpec((d, tn), lambda j: (0, jnp.minimum(j, last))),
                  pl.BlockSpec((1, tn), lambda j: (0, jnp.minimum(j, last))),
                  pl.BlockSpec(lam_qk.shape, lambda j: (0, 0)),
                  pl.BlockSpec((d, v_w), lambda j: (0, j))],
        out_specs=[pl.BlockSpec((bsz, tn), lambda j: (0, jnp.minimum(j, last))),
                   pl.BlockSpec((1, LANES), lambda j: (0, 0)),
                   pl.BlockSpec((d, v_w), lambda j: (0, j)),
                   pl.BlockSpec((v_w, d), lambda j: (0, 0))],
        out_shape=[jax.ShapeDtypeStruct((bsz, n), _F32), jax.ShapeDtypeStruct((1, LANES), _F32),
                   jax.ShapeDtypeStruct((d, n_in), _BF), jax.ShapeDtypeStruct((v_w, d), _BF)],
        compiler_params=pltpu.CompilerParams(
            dimension_semantics=("arbitrary",), vmem_limit_bytes=VMEM_LIMIT_BYTES),
        name="adaln_mod",
    )(c, w_ada, b_ada.reshape(1, n), lam_qk, w_in)


def _inproj_kernel(x_ref, mod_ref, gpre_ref, w_ref, wvt_ref, pos_ref, freq_ref, spread_ref,
                   qt_ref, k_ref, vt_ref, gates_ref, u_ref, *, d, qk_w, v_w, u_off, u_w):
    shift = mod_ref[0, :, 0:d]
    scale = mod_ref[0, :, d:2 * d]
    x = x_ref[0]
    inv = lax.rsqrt(jnp.mean(x * x, axis=-1, keepdims=True) + EPS)
    h = ((x * inv) * (gpre_ref[...] * (1.0 + scale)) + shift).astype(_BF)

    rows = x.shape[0]
    half = ROT_DIM // 2
    ang = freq_ref[...] * pos_ref[0]
    cs = jnp.concatenate([jnp.cos(ang), jnp.sin(ang)], axis=0)
    sel = spread_ref[...].astype(_BF)
    tab = None
    rest = cs
    for _ in range(3):
        piece = rest.astype(_BF)
        rest = rest - piece.astype(_F32)
        part = _dot(piece.astype(_F32).T.astype(_BF), sel)
        tab = part if tab is None else tab + part
    lane = lax.broadcasted_iota(jnp.int32, (1, LANES), 1)
    reps = qk_w // LANES
    rot_c = jnp.concatenate([jnp.where(lane % HEAD_DIM < ROT_DIM, tab[:, 0:LANES], 1.0)] * reps,
                            axis=1)
    rot_s = jnp.concatenate([tab[:, LANES:2 * LANES]] * reps, axis=1)
    low = lax.broadcasted_iota(jnp.int32, (1, qk_w), 1) % HEAD_DIM < half

    def rope(t):
        up = pltpu.roll(t, qk_w - half, 1)
        dn = pltpu.roll(t, half, 1)
        return t * rot_c + jnp.where(low, up, dn) * rot_s

    v_off = 2 * qk_w
    vt = _nt_dot(wvt_ref[...], h).astype(_BF)
    pad = (lax.broadcasted_iota(jnp.int32, (VT_PAD, rows), 0) == 0).astype(_BF)
    for hd in range(v_w // V_HEAD_DIM):
        vt_ref[0, hd, 0:V_HEAD_DIM, :] = vt[hd * V_HEAD_DIM:(hd + 1) * V_HEAD_DIM, :]
        vt_ref[0, hd, V_HEAD_DIM:V_HEAD_DIM + VT_PAD, :] = pad
    u_ref[0] = _dot(h, w_ref[:, u_off:u_off + u_w]).astype(_BF)
    qt = rope(_dot(h, w_ref[:, 0:qk_w]) * (HEAD_DIM ** -0.5 * LOG2_E)).T.astype(_BF)
    hw = 2 * HEAD_DIM
    for hd in range(qk_w // hw):
        qt_ref[0, hd] = qt[hd * hw:(hd + 1) * hw, :]
    k_ref[0] = rope(_dot(h, w_ref[:, qk_w:2 * qk_w])).astype(_BF)
    za_off = v_off + v_w
    z_a = _dot(h, w_ref[:, za_off:za_off + v_w])
    gates_ref[0, :, 0:v_w] = (z_a * _sigmoid(z_a)).astype(_BF)
    zs_off = u_off + u_w
    z_s = _dot(h, w_ref[:, zs_off:zs_off + u_w])
    gates_ref[0, :, v_w:v_w + u_w] = (z_s * _sigmoid(z_s)).astype(_BF)
    g_off = zs_off + u_w
    rest = w_ref.shape[1] - g_off
    piece = 512
    for p in range(rest // piece):
        gates_ref[0, :, v_w + u_w + p * piece:v_w + u_w + (p + 1) * piece] = _sigmoid(_dot(
            h, w_ref[:, g_off + p * piece:g_off + (p + 1) * piece])).astype(_BF)


def _inproj(x, mod3, g_pre, w_in_bf, wvt_bf, pos_rows, *, rows, qk_w, v_w, u_off, u_w):
    bsz, s, d = x.shape
    n_in = w_in_bf.shape[1]
    gates_w = n_in - 2 * qk_w - v_w - u_w
    n_heads = v_w // V_HEAD_DIM
    vt_rows = V_HEAD_DIM + VT_PAD
    kern = functools.partial(_inproj_kernel, d=d, qk_w=qk_w, v_w=v_w, u_off=u_off, u_w=u_w)
    freq, spread = _rope_constants()
    seq_spec = lambda w: pl.BlockSpec((1, rows, w), lambda b, t: (b, t, 0))
    return pl.pallas_call(
        kern,
        grid=(bsz, s // rows),
        in_specs=[seq_spec(d),
                  pl.BlockSpec((1, 1, 3 * d), lambda b, t: (b, 0, 0)),
                  pl.BlockSpec((1, d), lambda b, t: (0, 0)),
                  pl.BlockSpec((d, n_in), lambda b, t: (0, 0), pipeline_mode=pl.Buffered(1)),
                  pl.BlockSpec((v_w, d), lambda b, t: (0, 0), pipeline_mode=pl.Buffered(1)),
                  pl.BlockSpec((1, 1, rows), lambda b, t: (b * (s // rows) + t, 0, 0)),
                  pl.BlockSpec(freq.shape, lambda b, t: (0, 0)),
                  pl.BlockSpec(spread.shape, lambda b, t: (0, 0))],
        out_specs=[pl.BlockSpec((1, n_heads, 2 * HEAD_DIM, rows), lambda b, t: (b, 0, 0, t)),
                   seq_spec(qk_w),
                   pl.BlockSpec((1, n_heads, vt_rows, rows), lambda b, t: (b, 0, 0, t)),
                   seq_spec(gates_w), seq_spec(u_w)],
        out_shape=[jax.ShapeDtypeStruct((bsz, n_heads, 2 * HEAD_DIM, s), _BF),
                   jax.ShapeDtypeStruct((bsz, s, qk_w), _BF),
                   jax.ShapeDtypeStruct((bsz, n_heads, vt_rows, s), _BF),
                   jax.ShapeDtypeStruct((bsz, s, gates_w), _BF),
                   jax.ShapeDtypeStruct((bsz, s, u_w), _BF)],
        compiler_params=pltpu.CompilerParams(
            dimension_semantics=("arbitrary", "arbitrary"), vmem_limit_bytes=VMEM_LIMIT_BYTES),
        name="norm_inproj",
    )(x, mod3, g_pre.reshape(1, d), w_in_bf, wvt_bf, pos_rows, freq, spread)


def _attn_kernel(tab_ref, lam_ref, gsub_ref, qt_ref, k_ref, vt_ref, o_ref, *scratch, lam_init):
    s_ref = (scratch[0:2], scratch[2:4])
    e_ref = (scratch[4:6], scratch[6:8])
    m_ref = (scratch[8:10], scratch[10:12])
    ev_ref = (scratch[12:14], scratch[14:16])
    n_keys, tq = s_ref[0][0].shape
    tk = KEY_BLOCK
    n_kb = n_keys // tk
    step = pl.program_id(0)

    @pl.when(step == 0)
    def _():
        for ref in scratch:
            ref[...] = jnp.zeros_like(ref)

    lam = lam_ref[0:1, 0:1]

    def tick(new, old):
        ot = None
        for mp in range(2):
            ev = ev_ref[old][mp][...]
            nrm = ev[0:V_HEAD_DIM, :] * (1.0 / ev[V_HEAD_DIM:V_HEAD_DIM + 1, :])
            ot = nrm if ot is None else ot - lam * nrm
        inv = lax.rsqrt(jnp.mean(ot * ot, axis=0, keepdims=True) + EPS)
        o_ref[0] = (((ot * inv) * gsub_ref[...]) * (1.0 - lam_init)).T.astype(_BF)
        qt = qt_ref[0, 0]
        first = lax.broadcasted_iota(jnp.int32, (2 * HEAD_DIM, 1), 0) < HEAD_DIM
        zero = jnp.zeros_like(qt)
        qmt = (jnp.where(first, qt, zero), jnp.where(first, zero, qt))
        m_rows = [jnp.max(m_ref[old][mp][...], axis=0, keepdims=True) for mp in range(2)]
        n_part = 4 if n_kb % 4 == 0 else 1
        for mp in range(2):
            ev = None
            m = jnp.full((8, tq), -jnp.inf, _F32)
            for kb in range(n_kb):
                keys = slice(kb * tk, (kb + 1) * tk)
                if kb % (n_kb // n_part) == 0:
                    part_keys = slice(kb * tk, (kb + n_kb // n_part) * tk)
                    part = _dot(vt_ref[0, 0, :, part_keys], e_ref[new][mp][part_keys, :])
                    ev = part if ev is None else ev + part
                e_ref[old][mp][keys, :] = jnp.exp2(s_ref[old][mp][keys, :] - m_rows[mp]).astype(_BF)
                st = _dot(k_ref[0, keys, :], qmt[mp])
                s_ref[new][mp][keys, :] = st
                m = jnp.maximum(m, jnp.max(st.reshape(tk // 8, 8, tq), axis=0))
            m_ref[new][mp][...] = m
            ev_ref[new][mp][...] = ev

    @pl.when(step % 2 == 0)
    def _():
        tick(0, 1)

    @pl.when(step % 2 == 1)
    def _():
        tick(1, 0)


def _attention(lam, g_subln, qt, k, vt, *, tq, lam_init):
    bsz, n_heads, hw, s = qt.shape
    _, _, vt_rows, _ = vt.shape
    n_q = s // tq
    n_items = bsz * n_heads * n_q
    kern = functools.partial(_attn_kernel, lam_init=lam_init)
    n_steps = n_items + 3

    def item(j):
        j = np.clip(j, 0, n_items - 1)
        return j // (n_heads * n_q), (j // n_q) % n_heads, j % n_q

    g = np.arange(n_steps)
    tab = jnp.asarray(np.stack([*item(g), *item(g - 2), *item(g - 3)]).astype(np.int32))

    return pl.pallas_call(
        kern,
        grid_spec=pltpu.PrefetchScalarGridSpec(
            num_scalar_prefetch=1,
            grid=(n_steps,),
            in_specs=[pl.BlockSpec(lam.shape, lambda g, t: (0, 0)),
                      pl.BlockSpec((V_HEAD_DIM, 1), lambda g, t: (0, 0)),
                      pl.BlockSpec((1, 1, hw, tq), lambda g, t: (t[0, g], t[1, g], 0, t[2, g])),
                      pl.BlockSpec((1, s, hw), lambda g, t: (t[0, g], 0, t[1, g])),
                      pl.BlockSpec((1, 1, vt_rows, s), lambda g, t: (t[3, g], t[4, g], 0, 0))],
            out_specs=pl.BlockSpec((1, tq, V_HEAD_DIM), lambda g, t: (t[6, g], t[8, g], t[7, g])),
            scratch_shapes=([pltpu.VMEM((s, tq), _F32)] * 4 + [pltpu.VMEM((s, tq), _BF)] * 4
                            + [pltpu.VMEM((8, tq), _F32)] * 4 + [pltpu.VMEM((vt_rows, tq), _F32)] * 4)),
        out_shape=jax.ShapeDtypeStruct((bsz, s, n_heads * V_HEAD_DIM), _BF),
        compiler_params=pltpu.CompilerParams(
            dimension_semantics=("arbitrary",), vmem_limit_bytes=VMEM_LIMIT_BYTES),
        name="diff_attention",
    )(tab, lam, g_subln.reshape(V_HEAD_DIM, 1), qt, k, vt)


def _ssm_kernel(ut_ref, lam_ref, bt_ref, ct_ref, dsk_ref, yt_ref, at_ref, z_ref, xp_ref,
                *, bsz, n_c):
    el = SSM_CHUNK
    cw = el * SSM_GROUP
    sw = 2 * SSM_STATE
    n_cols = n_c * bsz
    in_group = [lax.broadcasted_iota(jnp.int32, (1, sw), 1) // SSM_STATE == gl for gl in range(2)]

    def group_rows(ref, gl):
        return ref[:, gl * SSM_GROUP:(gl + 1) * SSM_GROUP, :].reshape(cw, n_cols)

    def per_row(rows, cols):
        return (jnp.broadcast_to(rows[:, None, :], (el, SSM_GROUP, sw)),
                jnp.broadcast_to(cols[None, :, :], (el, SSM_GROUP, sw)))

    def cmul_rows(p_r, p_i, m_r, m_i):
        pr3, mr3 = per_row(p_r, m_r)
        pi3, mi3 = per_row(p_i, m_i)
        return (pr3 * mr3 - pi3 * mi3).reshape(cw, sw), (pr3 * mi3 + pi3 * mr3).reshape(cw, sw)

    idx = lax.broadcasted_iota(jnp.int32, (el, 1), 0).astype(_F32)
    ops = []
    for e in range(2):
        lr = jnp.minimum(lam_ref[0, e, 0:1, :], -1e-4)
        li = lam_ref[0, e, 1:2, :]
        dt = jnp.exp(lam_ref[0, e, 2:3, :])

        def power(k, lr=lr, li=li, dt=dt):
            mag = jnp.exp(k * (lr * dt))
            return mag * jnp.cos(k * (li * dt)), mag * jnp.sin(k * (li * dt))

        one_r, one_i = power(1.0)
        den = lr * lr + li * li
        nr, ni = one_r - 1.0, one_i
        coef_r = (nr * lr + ni * li) / den
        coef_i = (ni * lr - nr * li) / den
        b_r, b_i = bt_ref[0, e, 0], bt_ref[0, e, 1]
        bb_r = coef_r * b_r - coef_i * b_i
        bb_i = coef_r * b_i + coef_i * b_r
        c_r, c_i = ct_ref[0, e, 0], ct_ref[0, e, 1]
        ops.append(dict(power=power, bb=(bb_r, bb_i), c=(c_r, c_i)))
    fwd, bwd = ops

    zf_r, zf_i = cmul_rows(*fwd["power"](el - 1.0 - idx), *fwd["bb"])
    zb_r, zb_i = cmul_rows(*bwd["power"](idx), *bwd["bb"])
    yf_r, yf_i = cmul_rows(*fwd["power"](idx + 1.0), *fwd["c"])
    yb_r, yb_i = cmul_rows(*bwd["power"](el - idx), *bwd["c"])
    kf_r, kf_i = cmul_rows(*fwd["power"](idx), *fwd["c"])
    kb_r, kb_i = cmul_rows(*bwd["power"](idx), *bwd["c"])

    def pair_slot(parts, gl):
        return jnp.concatenate([jnp.where(in_group[gl], p, 0.0) for p in parts], axis=1).astype(_BF)

    for gl in range(2):
        at_ref[gl] = group_rows(ut_ref, gl).T
    z = (_dot(at_ref[0], pair_slot([zf_r, zf_i, zb_r, zb_i], 0))
         + _dot(at_ref[1], pair_slot([zf_r, zf_i, zb_r, zb_i], 1)))
    for comp in range(4):
        z_ref[comp] = z[:, comp * sw:(comp + 1) * sw]

    a_fr, a_fi = fwd["power"](float(el))
    a_br, a_bi = bwd["power"](float(el))

    def step(c, carry):
        fr, fi, br, bi = carry
        rows_f = pl.ds(pl.multiple_of(c * bsz, bsz), bsz)
        rows_b = pl.ds(pl.multiple_of((n_c - 1 - c) * bsz, bsz), bsz)
        xp_ref[0, rows_f, :] = fr
        xp_ref[1, rows_f, :] = fi
        xp_ref[2, rows_b, :] = br
        xp_ref[3, rows_b, :] = bi
        zfr = z_ref[0, rows_f, :]
        zfi = z_ref[1, rows_f, :]
        zbr = z_ref[2, rows_b, :]
        zbi = z_ref[3, rows_b, :]
        return (a_fr * fr - a_fi * fi + zfr, a_fr * fi + a_fi * fr + zfi,
                a_br * br - a_bi * bi + zbr, a_br * bi + a_bi * br + zbi)

    zero = jnp.zeros((bsz, sw), _F32)
    lax.fori_loop(0, n_c, step, (zero, zero, zero, zero), unroll=4)
    xp = jnp.concatenate([xp_ref[comp] for comp in range(4)], axis=1).astype(_BF)

    spread = (lax.broadcasted_iota(jnp.int32, (SSM_GROUP, cw), 1) % SSM_GROUP
              == lax.broadcasted_iota(jnp.int32, (SSM_GROUP, cw), 0)).astype(_BF)
    per_tile = LANES // SSM_GROUP
    t_in_tile = lax.broadcasted_iota(jnp.int32, (1, LANES), 1) // SSM_GROUP

    def lag_kernels(k_r, k_i, bb, gl):
        lhs = jnp.concatenate([jnp.where(in_group[gl], k_r, 0.0),
                               jnp.where(in_group[gl], -k_i, 0.0)], axis=1).astype(_BF)
        rhs = jnp.concatenate([bb[0], bb[1]], axis=1).astype(_BF)
        return _nt_dot(lhs, rhs)

    for gl in range(2):
        ktf = _dot(lag_kernels(kf_r, kf_i, fwd["bb"], gl).astype(_BF), spread)
        ktb = _dot(lag_kernels(kb_r, kb_i, bwd["bb"], gl).astype(_BF), spread)

        def column_block(tile, t):
            lanes = slice(tile * LANES, (tile + 1) * LANES)

            def blk(kt, k):
                return kt[k * SSM_GROUP:(k + 1) * SSM_GROUP, lanes]

            rows = [blk(ktb, t - s_) if s_ < t else blk(ktf, s_ - t) if s_ > t
                    else blk(ktf, 0) + blk(ktb, 0) for s_ in range(el)]
            return jnp.concatenate(rows, axis=0)

        tiles = []
        for tile in range(cw // LANES):
            acc = column_block(tile, tile * per_tile)
            for j in range(1, per_tile):
                acc = jnp.where(t_in_tile == j, column_block(tile, tile * per_tile + j), acc)
            tiles.append(acc)
        tm = jnp.concatenate(tiles, axis=1)
        u_g = group_rows(ut_ref, gl)
        yt = _dot(tm.astype(_BF), u_g) + _nt_dot(pair_slot([yf_r, -yf_i, yb_r, -yb_i], gl), xp)
        d_rows = jnp.broadcast_to(dsk_ref[0, gl][None], (el, SSM_GROUP, 1)).reshape(cw, 1)
        yt = yt + d_rows * u_g.astype(_F32)
        yt_ref[:, gl * SSM_GROUP:(gl + 1) * SSM_GROUP, :] = yt.reshape(
            el, SSM_GROUP, n_cols).astype(_BF)


def _ssm(ut, lam, bt, ct, dsk, *, bsz):
    _, u_w, n_cols = ut.shape
    n_c = n_cols // bsz
    n_pairs = u_w // (2 * SSM_GROUP)
    cw = SSM_CHUNK * SSM_GROUP
    sw = 2 * SSM_STATE
    kern = functools.partial(_ssm_kernel, bsz=bsz, n_c=n_c)
    blk = pl.BlockSpec((SSM_CHUNK, 2 * SSM_GROUP, n_cols), lambda g: (0, g, 0))
    par = pl.BlockSpec((1, 2, 2, SSM_GROUP, sw), lambda g: (g, 0, 0, 0, 0))
    return pl.pallas_call(
        kern,
        grid=(n_pairs,),
        in_specs=[blk,
                  pl.BlockSpec((1, 2, 3, sw), lambda g: (g, 0, 0, 0)),
                  par, par,
                  pl.BlockSpec((1, 2, SSM_GROUP, 1), lambda g: (g, 0, 0, 0))],
        out_specs=blk,
        out_shape=jax.ShapeDtypeStruct(ut.shape, _BF),
        scratch_shapes=[pltpu.VMEM((2, n_cols, cw), _BF),
                        pltpu.VMEM((4, n_cols, sw), _F32),
                        pltpu.VMEM((4, n_cols, sw), _F32)],
        compiler_params=pltpu.CompilerParams(
            dimension_semantics=("arbitrary",), vmem_limit_bytes=VMEM_LIMIT_BYTES),
        name="s5_chunked",
    )(ut, lam, bt, ct, dsk)


def _post_kernel(o_ref, gates_ref, y_ref, x_ref, mod_ref, bglu_ref, gfin_ref,
                 wua_ref, wglu_ref, wus_ref, wout_ref, out_ref, *, d, aw, sw):
    y = jax.nn.gelu(y_ref[0].astype(_F32))
    y = y * _sigmoid(_dot(y.astype(_BF), wglu_ref[...]) + bglu_ref[...])
    s_br = _dot((y * gates_ref[0, :, aw:aw + sw].astype(_F32)).astype(_BF), wus_ref[...])
    a_br = _dot((o_ref[0].astype(_F32) * gates_ref[0, :, 0:aw].astype(_F32)).astype(_BF),
                wua_ref[...])
    g_off = aw + sw
    merged = (gates_ref[0, :, g_off:g_off + d].astype(_F32) * a_br
              + gates_ref[0, :, g_off + d:g_off + 2 * d].astype(_F32) * s_br)
    r = _dot(merged.astype(_BF), wout_ref[...])
    xo = x_ref[0] + mod_ref[0, :, 2 * d:3 * d] * r
    inv = lax.rsqrt(jnp.mean(xo * xo, axis=-1, keepdims=True) + EPS)
    out_ref[0] = (xo * inv) * gfin_ref[...]


def _post(o, gates, y, x, mod3, b_glu, g_final, wua, wglu, wus, wout, *, rows):
    bsz, s, d = x.shape
    aw = o.shape[2]
    sw = y.shape[2]
    kern = functools.partial(_post_kernel, d=d, aw=aw, sw=sw)
    full = lambda a: pl.BlockSpec(a.shape, lambda b, t: (0,) * a.ndim)
    seq_spec = lambda w: pl.BlockSpec((1, rows, w), lambda b, t: (b, t, 0))
    bgl = b_glu.reshape(1, sw)
    gfi = g_final.reshape(1, d)
    return pl.pallas_call(
        kern,
        grid=(bsz, s // rows),
        in_specs=[seq_spec(aw), seq_spec(gates.shape[2]), seq_spec(sw), seq_spec(d),
                  pl.BlockSpec((1, 1, 3 * d), lambda b, t: (b, 0, 0)),
                  full(bgl), full(gfi), full(wua), full(wglu), full(wus), full(wout)],
        out_specs=seq_spec(d),
        out_shape=jax.ShapeDtypeStruct(x.shape, _F32),
        compiler_params=pltpu.CompilerParams(
            dimension_semantics=("arbitrary", "arbitrary"), vmem_limit_bytes=VMEM_LIMIT_BYTES),
        name="merge_out",
    )(o, gates, y, x, mod3, bgl, gfi, wua, wglu, wus, wout)


def _rope_constants():
    half = ROT_DIM // 2
    freq = (ROPE_THETA ** (-jnp.arange(half, dtype=_F32) * 2.0 / ROT_DIM)).reshape(half, 1)
    in_head = jnp.arange(LANES) % HEAD_DIM
    j = jnp.arange(half)[:, None]
    cos_sel = ((in_head == j) | (in_head == half + j)).astype(_F32)
    sin_sel = (in_head == half + j).astype(_F32) - (in_head == j).astype(_F32)
    zero = jnp.zeros_like(cos_sel)
    return freq, jnp.concatenate([jnp.concatenate([cos_sel, zero], axis=1),
                                  jnp.concatenate([zero, sin_sel], axis=1)], axis=0)


def _ssm_params(lam_re, lam_im, log_dt, b_re, b_im, c_re, c_im):
    n_g, n_p = lam_re.shape[1], lam_re.shape[2]
    pairs = n_g // 2

    def lanes(m):
        rows = m.shape[2]
        return m.reshape(2, pairs, 2, rows, n_p).transpose(1, 0, 3, 2, 4).reshape(pairs, 2, rows, 2 * n_p)

    lam = lanes(jnp.stack([lam_re, lam_im, jnp.broadcast_to(log_dt[..., None], lam_re.shape)],
                          axis=2).astype(_F32))
    bt = jnp.stack([lanes(jnp.swapaxes(b_re, -1, -2)), lanes(jnp.swapaxes(b_im, -1, -2))], axis=2)
    ct = jnp.stack([lanes(c_re), lanes(c_im)], axis=2)
    return lam, bt.astype(_F32), ct.astype(_F32)


def _layer(x, c, positions, layer_idx, w_ada, b_ada, g_pre, w_in, lam_qk, g_subln,
           lam_re, lam_im, log_dt, b_re, b_im, c_re, c_im, d_skip,
           w_glu, b_glu, w_up_attn, w_up_ssm, w_out, g_final):
    bsz, s, d = x.shape
    n_c = s // SSM_CHUNK
    qk_w = ATTN_HEADS * 2 * HEAD_DIM
    v_w = ATTN_HEADS * V_HEAD_DIM
    u_w = lam_re.shape[1] * SSM_GROUP
    u_off = 2 * qk_w + 2 * v_w
    lam_init = 0.8 - 0.6 * math.exp(-0.3 * layer_idx)
    mod, lam, w_in_bf, wvt_bf = _prepare(c, w_ada, b_ada, lam_qk, w_in, lam_init=lam_init,
                                         v_off=2 * qk_w, v_w=v_w)
    mod3 = mod.reshape(bsz, 1, 3 * d)
    rows = min(ROW_BLOCK, s)
    pos_rows = positions.astype(_F32).reshape(bsz * (s // rows), 1, rows)
    qt, k, vt, gates, u = _inproj(x, mod3, g_pre, w_in_bf, wvt_bf, pos_rows, rows=rows,
                                  qk_w=qk_w, v_w=v_w, u_off=u_off, u_w=u_w)
    o = _attention(lam, g_subln, qt, k, vt, tq=min(Q_ROWS, s), lam_init=lam_init)
    ut = u.reshape(bsz, n_c, SSM_CHUNK, u_w).transpose(2, 3, 1, 0).reshape(SSM_CHUNK, u_w, n_c * bsz)
    dsk = d_skip.astype(_F32).reshape(u_w // (2 * SSM_GROUP), 2, SSM_GROUP, 1)
    yt = _ssm(ut, *_ssm_params(lam_re, lam_im, log_dt, b_re, b_im, c_re, c_im), dsk, bsz=bsz)
    y = yt.reshape(SSM_CHUNK, u_w, n_c, bsz).transpose(3, 2, 0, 1).reshape(bsz, s, u_w)
    return _post(o, gates, y, x, mod3, b_glu, g_final,
                 w_up_attn.astype(_BF), w_glu.astype(_BF), w_up_ssm.astype(_BF), w_out.astype(_BF),
                 rows=rows)


def kernel(x, c, positions, w_ada, b_ada, g_pre, w_in, lam_qk, g_subln, ssm_lam_re, ssm_lam_im,
           ssm_log_dt, ssm_b_re, ssm_b_im, ssm_c_re, ssm_c_im, ssm_d, w_glu, b_glu, w_up_attn,
           w_up_ssm, w_out, g_final):
    depth = w_ada.shape[0]
    assert depth == 1, "the final RMSNorm is fused into the single layer's epilogue"
    assert x.shape[1] % (SSM_CHUNK * 8) == 0
    return _layer(x, c, positions, 0, w_ada[0], b_ada[0], g_pre[0], w_in[0],
                  lam_qk[0], g_subln[0], ssm_lam_re[0], ssm_lam_im[0], ssm_log_dt[0], ssm_b_re[0],
                  ssm_b_im[0], ssm_c_re[0], ssm_c_im[0], ssm_d[0], w_glu[0], b_glu[0],
                  w_up_attn[0], w_up_ssm[0], w_out[0], g_final)
```

```python
import functools
import math

import jax
import jax.numpy as jnp
import numpy as np
from jax import lax
from jax.experimental import pallas as pl
from jax.experimental.pallas import tpu as pltpu

ATTN_HEADS = 4
HEAD_DIM = 64
V_HEAD_DIM = 2 * HEAD_DIM
ROT_DIM = HEAD_DIM // 4
ROPE_THETA = 500000.0
SSM_GROUP = 16
SSM_STATE = 64
SSM_CHUNK = 16
EPS = 1e-6
LOG2_E = math.log2(math.e)
LANES = 128
VMEM_LIMIT_BYTES = 56 * 1024 * 1024
ROW_BLOCK = 1024
Q_ROWS = 512
KEY_BLOCK = 256
VT_PAD = 16

_HI = lax.Precision.HIGHEST
_BF = jnp.bfloat16
_F32 = jnp.float32


def _nt_dot(a, b):
    return lax.dot_general(a, b, (((1,), (1,)), ((), ())), preferred_element_type=_F32)


def _dot(a, b):
    return jnp.dot(a, b, preferred_element_type=_F32)


def _sigmoid(x):
    return 0.5 * jnp.tanh(0.5 * x) + 0.5


def _prep_kernel(c_ref, w_ref, b_ref, lamqk_ref, win_ref, o_ref, lam_ref, winb_ref, wvt_ref,
                 *, lam_init, n_mod, v_tile):
    j = pl.program_id(0)

    @pl.when(j < n_mod)
    def _():
        o_ref[...] = jnp.dot(jax.nn.silu(c_ref[...]), w_ref[...], precision=_HI,
                             preferred_element_type=_F32) + b_ref[...]

    @pl.when(j == 0)
    def _():
        lf = lamqk_ref[...]
        lam = (jnp.exp(jnp.sum(lf[0:1, :] * lf[1:2, :], axis=-1, keepdims=True))
               - jnp.exp(jnp.sum(lf[2:3, :] * lf[3:4, :], axis=-1, keepdims=True)) + lam_init)
        lam_ref[...] = jnp.broadcast_to(lam, lam_ref.shape)

    w = win_ref[...]
    winb_ref[...] = w.astype(_BF)

    @pl.when(j == v_tile)
    def _():
        wvt_ref[...] = w.T.astype(_BF)


def _prepare(c, w_ada, b_ada, lam_qk, w_in, *, lam_init, v_off, v_w):
    bsz, d = c.shape
    n = w_ada.shape[1]
    n_in = w_in.shape[1]
    tn = d
    n_mod = n // tn
    assert v_off % v_w == 0 and n_in % v_w == 0 and n_in // v_w >= n_mod
    last = n_mod - 1
    return pl.pallas_call(
        functools.partial(_prep_kernel, lam_init=lam_init, n_mod=n_mod, v_tile=v_off // v_w),
        grid=(n_in // v_w,),
        in_specs=[pl.BlockSpec((bsz, d), lambda j: (0, 0)),
                  pl.BlockSpec((d, tn), lambda j: (0, jnp.minimum(j, last))),
                  pl.BlockSpec((1, tn), lambda j: (0, jnp.minimum(j, last))),
                  pl.BlockSpec(lam_qk.shape, lambda j: (0, 0)),
                  pl.BlockSpec((d, v_w), lambda j: (0, j))],
        out_specs=[pl.BlockSpec((bsz, tn), lambda j: (0, jnp.minimum(j, last))),
                   pl.BlockSpec((1, LANES), lambda j: (0, 0)),
                   pl.BlockSpec((d, v_w), lambda j: (0, j)),
                   pl.BlockSpec((v_w, d), lambda j: (0, 0))],
        out_shape=[jax.ShapeDtypeStruct((bsz, n), _F32), jax.ShapeDtypeStruct((1, LANES), _F32),
                   jax.ShapeDtypeStruct((d, n_in), _BF), jax.ShapeDtypeStruct((v_w, d), _BF)],
        compiler_params=pltpu.CompilerParams(
            dimension_semantics=("arbitrary",), vmem_limit_bytes=VMEM_LIMIT_BYTES),
        name="adaln_mod",
    )(c, w_ada, b_ada.reshape(1, n), lam_qk, w_in)


def _inproj_kernel(x_ref, mod_ref, gpre_ref, w_ref, wvt_ref, pos_ref, freq_ref, spread_ref,
                   qt_ref, k_ref, vt_ref, gates_ref, u_ref, *, d, qk_w, v_w, u_off, u_w):
    shift = mod_ref[0, :, 0:d]
    scale = mod_ref[0, :, d:2 * d]
    x = x_ref[0]
    inv = lax.rsqrt(jnp.mean(x * x, axis=-1, keepdims=True) + EPS)
    h = ((x * inv) * (gpre_ref[...] * (1.0 + scale)) + shift).astype(_BF)

    rows = x.shape[0]
    half = ROT_DIM // 2
    ang = freq_ref[...] * pos_ref[0]
    cs = jnp.concatenate([jnp.cos(ang), jnp.sin(ang)], axis=0)
    pieces = []
    rest = cs
    for _ in range(3):
        piece = rest.astype(_BF).astype(_F32)
        rest = rest - piece
        pieces.append(piece)
    sel = spread_ref[...].astype(_BF)
    tab = _dot(jnp.concatenate(pieces, axis=0).T.astype(_BF), jnp.concatenate([sel] * 3, axis=0))
    lane = lax.broadcasted_iota(jnp.int32, (1, LANES), 1)
    reps = qk_w // LANES
    rot_c = jnp.concatenate([jnp.where(lane % HEAD_DIM < ROT_DIM, tab[:, 0:LANES], 1.0)] * reps,
                            axis=1)
    rot_s = jnp.concatenate([tab[:, LANES:2 * LANES]] * reps, axis=1)
    low = lax.broadcasted_iota(jnp.int32, (1, qk_w), 1) % HEAD_DIM < half

    def rope(t):
        up = pltpu.roll(t, qk_w - half, 1)
        dn = pltpu.roll(t, half, 1)
        return t * rot_c + jnp.where(low, up, dn) * rot_s

    v_off = 2 * qk_w
    vt = _nt_dot(wvt_ref[...], h).astype(_BF)
    pad = (lax.broadcasted_iota(jnp.int32, (VT_PAD, rows), 0) == 0).astype(_BF)
    for hd in range(v_w // V_HEAD_DIM):
        vt_ref[0, hd, 0:V_HEAD_DIM, :] = vt[hd * V_HEAD_DIM:(hd + 1) * V_HEAD_DIM, :]
        vt_ref[0, hd, V_HEAD_DIM:V_HEAD_DIM + VT_PAD, :] = pad
    u_ref[0] = _dot(h, w_ref[:, u_off:u_off + u_w]).astype(_BF)
    qt = rope(_dot(h, w_ref[:, 0:qk_w]) * (HEAD_DIM ** -0.5 * LOG2_E)).T.astype(_BF)
    hw = 2 * HEAD_DIM
    for hd in range(qk_w // hw):
        qt_ref[0, hd] = qt[hd * hw:(hd + 1) * hw, :]
    k_ref[0] = rope(_dot(h, w_ref[:, qk_w:2 * qk_w])).astype(_BF)
    za_off = v_off + v_w
    z_a = _dot(h, w_ref[:, za_off:za_off + v_w])
    gates_ref[0, :, 0:v_w] = (z_a * _sigmoid(z_a)).astype(_BF)
    zs_off = u_off + u_w
    z_s = _dot(h, w_ref[:, zs_off:zs_off + u_w])
    gates_ref[0, :, v_w:v_w + u_w] = (z_s * _sigmoid(z_s)).astype(_BF)
    g_off = zs_off + u_w
    rest = w_ref.shape[1] - g_off
    piece = 512
    for p in range(rest // piece):
        gates_ref[0, :, v_w + u_w + p * piece:v_w + u_w + (p + 1) * piece] = _sigmoid(_dot(
            h, w_ref[:, g_off + p * piece:g_off + (p + 1) * piece])).astype(_BF)


def _inproj(x, mod3, g_pre, w_in_bf, wvt_bf, pos_rows, *, rows, qk_w, v_w, u_off, u_w):
    bsz, s, d = x.shape
    n_in = w_in_bf.shape[1]
    gates_w = n_in - 2 * qk_w - v_w - u_w
    n_heads = v_w // V_HEAD_DIM
    vt_rows = V_HEAD_DIM + VT_PAD
    kern = functools.partial(_inproj_kernel, d=d, qk_w=qk_w, v_w=v_w, u_off=u_off, u_w=u_w)
    freq, spread = _rope_constants()
    seq_spec = lambda w: pl.BlockSpec((1, rows, w), lambda b, t: (b, t, 0))
    return pl.pallas_call(
        kern,
        grid=(bsz, s // rows),
        in_specs=[seq_spec(d),
                  pl.BlockSpec((1, 1, 3 * d), lambda b, t: (b, 0, 0)),
                  pl.BlockSpec((1, d), lambda b, t: (0, 0)),
                  pl.BlockSpec((d, n_in), lambda b, t: (0, 0), pipeline_mode=pl.Buffered(1)),
                  pl.BlockSpec((v_w, d), lambda b, t: (0, 0), pipeline_mode=pl.Buffered(1)),
                  pl.BlockSpec((1, 1, rows), lambda b, t: (b * (s // rows) + t, 0, 0)),
                  pl.BlockSpec(freq.shape, lambda b, t: (0, 0)),
                  pl.BlockSpec(spread.shape, lambda b, t: (0, 0))],
        out_specs=[pl.BlockSpec((1, n_heads, 2 * HEAD_DIM, rows), lambda b, t: (b, 0, 0, t)),
                   seq_spec(qk_w),
                   pl.BlockSpec((1, n_heads, vt_rows, rows), lambda b, t: (b, 0, 0, t)),
                   seq_spec(gates_w), seq_spec(u_w)],
        out_shape=[jax.ShapeDtypeStruct((bsz, n_heads, 2 * HEAD_DIM, s), _BF),
                   jax.ShapeDtypeStruct((bsz, s, qk_w), _BF),
                   jax.ShapeDtypeStruct((bsz, n_heads, vt_rows, s), _BF),
                   jax.ShapeDtypeStruct((bsz, s, gates_w), _BF),
                   jax.ShapeDtypeStruct((bsz, s, u_w), _BF)],
        compiler_params=pltpu.CompilerParams(
            dimension_semantics=("arbitrary", "arbitrary"), vmem_limit_bytes=VMEM_LIMIT_BYTES),
        name="norm_inproj",
    )(x, mod3, g_pre.reshape(1, d), w_in_bf, wvt_bf, pos_rows, freq, spread)


def _attn_kernel(tab_ref, lam_ref, gsub_ref, qt_ref, k_ref, vt_ref, o_ref, *scratch, lam_init):
    s_ref = (scratch[0:2], scratch[2:4])
    e_ref = (scratch[4:6], scratch[6:8])
    m_ref = (scratch[8:10], scratch[10:12])
    ev_ref = (scratch[12:14], scratch[14:16])
    n_keys, tq = s_ref[0][0].shape
    tk = KEY_BLOCK
    n_kb = n_keys // tk
    step = pl.program_id(0)

    @pl.when(step == 0)
    def _():
        for ref in scratch:
            ref[...] = jnp.zeros_like(ref)

    lam = lam_ref[0:1, 0:1]

    def tick(new, old):
        def normalise_item():
            ot = None
            for mp in range(2):
                ev = ev_ref[old][mp][...]
                nrm = ev[0:V_HEAD_DIM, :] * (1.0 / ev[V_HEAD_DIM:V_HEAD_DIM + 1, :])
                ot = nrm if ot is None else ot - lam * nrm
            inv = lax.rsqrt(jnp.mean(ot * ot, axis=0, keepdims=True) + EPS)
            o_ref[0] = (((ot * inv) * gsub_ref[...]) * (1.0 - lam_init)).T.astype(_BF)

        qt = qt_ref[0, 0]
        first = lax.broadcasted_iota(jnp.int32, (2 * HEAD_DIM, 1), 0) < HEAD_DIM
        zero = jnp.zeros_like(qt)
        qmt = (jnp.where(first, qt, zero), jnp.where(first, zero, qt))
        n_part = 4 if n_kb % 4 == 0 else 1
        for mp in range(2):
            m_row = jnp.max(m_ref[old][mp][...], axis=0, keepdims=True)
            for kb in range(n_kb):
                keys = slice(kb * tk, (kb + 1) * tk)
                if mp == 1 and kb == min(1, n_kb - 1):
                    normalise_item()
                if kb % (n_kb // n_part) == 0:
                    part_keys = slice(kb * tk, (kb + n_kb // n_part) * tk)
                    part = _dot(vt_ref[0, 0, :, part_keys], e_ref[new][mp][part_keys, :])
                    if kb == 0:
                        ev_ref[new][mp][...] = part
                    else:
                        ev_ref[new][mp][...] += part
                e_ref[old][mp][keys, :] = jnp.exp2(s_ref[old][mp][keys, :] - m_row).astype(_BF)
                st = _dot(k_ref[0, keys, :], qmt[mp])
                s_ref[new][mp][keys, :] = st
                m_blk = jnp.max(st.reshape(tk // 8, 8, tq), axis=0)
                if kb == 0:
                    m_ref[new][mp][...] = m_blk
                else:
                    m_ref[new][mp][...] = jnp.maximum(m_ref[new][mp][...], m_blk)

    @pl.when(step % 2 == 0)
    def _():
        tick(0, 1)

    @pl.when(step % 2 == 1)
    def _():
        tick(1, 0)


def _attention(lam, g_subln, qt, k, vt, *, tq, lam_init):
    bsz, n_heads, hw, s = qt.shape
    _, _, vt_rows, _ = vt.shape
    n_q = s // tq
    n_items = bsz * n_heads * n_q
    kern = functools.partial(_attn_kernel, lam_init=lam_init)
    n_steps = n_items + 3

    def item(j):
        j = np.clip(j, 0, n_items - 1)
        return j // (n_heads * n_q), (j // n_q) % n_heads, j % n_q

    g = np.arange(n_steps)
    tab = jnp.asarray(np.stack([*item(g), *item(g - 2), *item(g - 3)]).astype(np.int32))

    return pl.pallas_call(
        kern,
        grid_spec=pltpu.PrefetchScalarGridSpec(
            num_scalar_prefetch=1,
            grid=(n_steps,),
            in_specs=[pl.BlockSpec(lam.shape, lambda g, t: (0, 0)),
                      pl.BlockSpec((V_HEAD_DIM, 1), lambda g, t: (0, 0)),
                      pl.BlockSpec((1, 1, hw, tq), lambda g, t: (t[0, g], t[1, g], 0, t[2, g])),
                      pl.BlockSpec((1, s, hw), lambda g, t: (t[0, g], 0, t[1, g])),
                      pl.BlockSpec((1, 1, vt_rows, s), lambda g, t: (t[3, g], t[4, g], 0, 0))],
            out_specs=pl.BlockSpec((1, tq, V_HEAD_DIM), lambda g, t: (t[6, g], t[8, g], t[7, g])),
            scratch_shapes=([pltpu.VMEM((s, tq), _F32)] * 4 + [pltpu.VMEM((s, tq), _BF)] * 4
                            + [pltpu.VMEM((8, tq), _F32)] * 4 + [pltpu.VMEM((vt_rows, tq), _F32)] * 4)),
        out_shape=jax.ShapeDtypeStruct((bsz, s, n_heads * V_HEAD_DIM), _BF),
        compiler_params=pltpu.CompilerParams(
            dimension_semantics=("arbitrary",), vmem_limit_bytes=VMEM_LIMIT_BYTES),
        name="diff_attention",
    )(tab, lam, g_subln.reshape(V_HEAD_DIM, 1), qt, k, vt)


def _ssm_kernel(ut_ref, lam_ref, bt_ref, ct_ref, dsk_ref, yt_ref, at_ref, z_ref, xp_ref,
                *, bsz, n_c):
    el = SSM_CHUNK
    cw = el * SSM_GROUP
    sw = 2 * SSM_STATE
    n_cols = n_c * bsz
    in_group = [lax.broadcasted_iota(jnp.int32, (1, sw), 1) // SSM_STATE == gl for gl in range(2)]

    def group_rows(ref, gl):
        return ref[:, gl * SSM_GROUP:(gl + 1) * SSM_GROUP, :].reshape(cw, n_cols)

    def per_row(rows, cols):
        return (jnp.broadcast_to(rows[:, None, :], (el, SSM_GROUP, sw)),
                jnp.broadcast_to(cols[None, :, :], (el, SSM_GROUP, sw)))

    def cmul_rows(p_r, p_i, m_r, m_i):
        pr3, mr3 = per_row(p_r, m_r)
        pi3, mi3 = per_row(p_i, m_i)
        return (pr3 * mr3 - pi3 * mi3).reshape(cw, sw), (pr3 * mi3 + pi3 * mr3).reshape(cw, sw)

    idx = lax.broadcasted_iota(jnp.int32, (el, 1), 0).astype(_F32)
    ops = []
    for e in range(2):
        lr = jnp.minimum(lam_ref[0, e, 0:1, :], -1e-4)
        li = lam_ref[0, e, 1:2, :]
        dt = jnp.exp(lam_ref[0, e, 2:3, :])

        def power(k, lr=lr, li=li, dt=dt):
            mag = jnp.exp(k * (lr * dt))
            return mag * jnp.cos(k * (li * dt)), mag * jnp.sin(k * (li * dt))

        one_r, one_i = power(1.0)
        den = lr * lr + li * li
        nr, ni = one_r - 1.0, one_i
        coef_r = (nr * lr + ni * li) / den
        coef_i = (ni * lr - nr * li) / den
        b_r, b_i = bt_ref[0, e, 0], bt_ref[0, e, 1]
        bb_r = coef_r * b_r - coef_i * b_i
        bb_i = coef_r * b_i + coef_i * b_r
        c_r, c_i = ct_ref[0, e, 0], ct_ref[0, e, 1]
        ops.append(dict(power=power, bb=(bb_r, bb_i), c=(c_r, c_i)))
    fwd, bwd = ops

    zf_r, zf_i = cmul_rows(*fwd["power"](el - 1.0 - idx), *fwd["bb"])
    zb_r, zb_i = cmul_rows(*bwd["power"](idx), *bwd["bb"])
    yf_r, yf_i = cmul_rows(*fwd["power"](idx + 1.0), *fwd["c"])
    yb_r, yb_i = cmul_rows(*bwd["power"](el - idx), *bwd["c"])
    kf_r, kf_i = cmul_rows(*fwd["power"](idx), *fwd["c"])
    kb_r, kb_i = cmul_rows(*bwd["power"](idx), *bwd["c"])

    def pair_slot(parts, gl):
        return jnp.concatenate([jnp.where(in_group[gl], p, 0.0) for p in parts], axis=1).astype(_BF)

    for gl in range(2):
        at_ref[gl] = group_rows(ut_ref, gl).T
    z = (_dot(at_ref[0], pair_slot([zf_r, zf_i, zb_r, zb_i], 0))
         + _dot(at_ref[1], pair_slot([zf_r, zf_i, zb_r, zb_i], 1)))
    for comp in range(4):
        z_ref[comp] = z[:, comp * sw:(comp + 1) * sw]

    a_fr, a_fi = fwd["power"](float(el))
    a_br, a_bi = bwd["power"](float(el))

    def step(c, carry):
        fr, fi, br, bi = carry
        rows_f = pl.ds(pl.multiple_of(c * bsz, bsz), bsz)
        rows_b = pl.ds(pl.multiple_of((n_c - 1 - c) * bsz, bsz), bsz)
        xp_ref[0, rows_f, :] = fr
        xp_ref[1, rows_f, :] = fi
        xp_ref[2, rows_b, :] = br
        xp_ref[3, rows_b, :] = bi
        zfr = z_ref[0, rows_f, :]
        zfi = z_ref[1, rows_f, :]
        zbr = z_ref[2, rows_b, :]
        zbi = z_ref[3, rows_b, :]
        return (a_fr * fr - a_fi * fi + zfr, a_fr * fi + a_fi * fr + zfi,
                a_br * br - a_bi * bi + zbr, a_br * bi + a_bi * br + zbi)

    zero = jnp.zeros((bsz, sw), _F32)
    lax.fori_loop(0, n_c, step, (zero, zero, zero, zero), unroll=4)
    xp = jnp.concatenate([xp_ref[comp] for comp in range(4)], axis=1).astype(_BF)

    spread = (lax.broadcasted_iota(jnp.int32, (SSM_GROUP, cw), 1) % SSM_GROUP
              == lax.broadcasted_iota(jnp.int32, (SSM_GROUP, cw), 0)).astype(_BF)
    per_tile = LANES // SSM_GROUP
    t_in_tile = lax.broadcasted_iota(jnp.int32, (1, LANES), 1) // SSM_GROUP

    def lag_kernels(k_r, k_i, bb, gl):
        lhs = jnp.concatenate([jnp.where(in_group[gl], k_r, 0.0),
                               jnp.where(in_group[gl], -k_i, 0.0)], axis=1).astype(_BF)
        rhs = jnp.concatenate([bb[0], bb[1]], axis=1).astype(_BF)
        return _nt_dot(lhs, rhs)

    for gl in range(2):
        ktf = _dot(lag_kernels(kf_r, kf_i, fwd["bb"], gl).astype(_BF), spread)
        ktb = _dot(lag_kernels(kb_r, kb_i, bwd["bb"], gl).astype(_BF), spread)

        def column_block(tile, t):
            lanes = slice(tile * LANES, (tile + 1) * LANES)

            def blk(kt, k):
                return kt[k * SSM_GROUP:(k + 1) * SSM_GROUP, lanes]

            rows = [blk(ktb, t - s_) if s_ < t else blk(ktf, s_ - t) if s_ > t
                    else blk(ktf, 0) + blk(ktb, 0) for s_ in range(el)]
            return jnp.concatenate(rows, axis=0)

        tiles = []
        for tile in range(cw // LANES):
            acc = column_block(tile, tile * per_tile)
            for j in range(1, per_tile):
                acc = jnp.where(t_in_tile == j, column_block(tile, tile * per_tile + j), acc)
            tiles.append(acc)
        tm = jnp.concatenate(tiles, axis=1)
        u_g = group_rows(ut_ref, gl)
        yt = _dot(tm.astype(_BF), u_g) + _nt_dot(pair_slot([yf_r, -yf_i, yb_r, -yb_i], gl), xp)
        d_rows = jnp.broadcast_to(dsk_ref[0, gl][None], (el, SSM_GROUP, 1)).reshape(cw, 1)
        yt = yt + d_rows * u_g.astype(_F32)
        yt_ref[:, gl * SSM_GROUP:(gl + 1) * SSM_GROUP, :] = yt.reshape(
            el, SSM_GROUP, n_cols).astype(_BF)


def _ssm(ut, lam, bt, ct, dsk, *, bsz):
    _, u_w, n_cols = ut.shape
    n_c = n_cols // bsz
    n_pairs = u_w // (2 * SSM_GROUP)
    cw = SSM_CHUNK * SSM_GROUP
    sw = 2 * SSM_STATE
    kern = functools.partial(_ssm_kernel, bsz=bsz, n_c=n_c)
    blk = pl.BlockSpec((SSM_CHUNK, 2 * SSM_GROUP, n_cols), lambda g: (0, g, 0))
    par = pl.BlockSpec((1, 2, 2, SSM_GROUP, sw), lambda g: (g, 0, 0, 0, 0))
    return pl.pallas_call(
        kern,
        grid=(n_pairs,),
        in_specs=[blk,
                  pl.BlockSpec((1, 2, 3, sw), lambda g: (g, 0, 0, 0)),
                  par, par,
                  pl.BlockSpec((1, 2, SSM_GROUP, 1), lambda g: (g, 0, 0, 0))],
        out_specs=blk,
        out_shape=jax.ShapeDtypeStruct(ut.shape, _BF),
        scratch_shapes=[pltpu.VMEM((2, n_cols, cw), _BF),
                        pltpu.VMEM((4, n_cols, sw), _F32),
                        pltpu.VMEM((4, n_cols, sw), _F32)],
        compiler_params=pltpu.CompilerParams(
            dimension_semantics=("arbitrary",), vmem_limit_bytes=VMEM_LIMIT_BYTES),
        name="s5_chunked",
    )(ut, lam, bt, ct, dsk)


def _post_kernel(o_ref, gates_ref, y_ref, x_ref, mod_ref, bglu_ref, gfin_ref,
                 wua_ref, wglu_ref, wus_ref, wout_ref, out_ref, *, d, aw, sw):
    y = jax.nn.gelu(y_ref[0].astype(_F32))
    y = y * _sigmoid(_dot(y.astype(_BF), wglu_ref[...]) + bglu_ref[...])
    s_br = _dot((y * gates_ref[0, :, aw:aw + sw].astype(_F32)).astype(_BF), wus_ref[...])
    a_br = _dot((o_ref[0].astype(_F32) * gates_ref[0, :, 0:aw].astype(_F32)).astype(_BF),
                wua_ref[...])
    g_off = aw + sw
    merged = (gates_ref[0, :, g_off:g_off + d].astype(_F32) * a_br
              + gates_ref[0, :, g_off + d:g_off + 2 * d].astype(_F32) * s_br)
    r = _dot(merged.astype(_BF), wout_ref[...])
    xo = x_ref[0] + mod_ref[0, :, 2 * d:3 * d] * r
    inv = lax.rsqrt(jnp.mean(xo * xo, axis=-1, keepdims=True) + EPS)
    out_ref[0] = (xo * inv) * gfin_ref[...]


def _post(o, gates, y, x, mod3, b_glu, g_final, wua, wglu, wus, wout, *, rows):
    bsz, s, d = x.shape
    aw = o.shape[2]
    sw = y.shape[2]
    kern = functools.partial(_post_kernel, d=d, aw=aw, sw=sw)
    full = lambda a: pl.BlockSpec(a.shape, lambda b, t: (0,) * a.ndim)
    seq_spec = lambda w: pl.BlockSpec((1, rows, w), lambda b, t: (b, t, 0))
    bgl = b_glu.reshape(1, sw)
    gfi = g_final.reshape(1, d)
    return pl.pallas_call(
        kern,
        grid=(bsz, s // rows),
        in_specs=[seq_spec(aw), seq_spec(gates.shape[2]), seq_spec(sw), seq_spec(d),
                  pl.BlockSpec((1, 1, 3 * d), lambda b, t: (b, 0, 0)),
                  full(bgl), full(gfi), full(wua), full(wglu), full(wus), full(wout)],
        out_specs=seq_spec(d),
        out_shape=jax.ShapeDtypeStruct(x.shape, _F32),
        compiler_params=pltpu.CompilerParams(
            dimension_semantics=("arbitrary", "arbitrary"), vmem_limit_bytes=VMEM_LIMIT_BYTES),
        name="merge_out",
    )(o, gates, y, x, mod3, bgl, gfi, wua, wglu, wus, wout)


def _rope_constants():
    half = ROT_DIM // 2
    freq = (ROPE_THETA ** (-jnp.arange(half, dtype=_F32) * 2.0 / ROT_DIM)).reshape(half, 1)
    in_head = jnp.arange(LANES) % HEAD_DIM
    j = jnp.arange(half)[:, None]
    cos_sel = ((in_head == j) | (in_head == half + j)).astype(_F32)
    sin_sel = (in_head == half + j).astype(_F32) - (in_head == j).astype(_F32)
    zero = jnp.zeros_like(cos_sel)
    return freq, jnp.concatenate([jnp.concatenate([cos_sel, zero], axis=1),
                                  jnp.concatenate([zero, sin_sel], axis=1)], axis=0)


def _ssm_params(lam_re, lam_im, log_dt, b_re, b_im, c_re, c_im):
    n_g, n_p = lam_re.shape[1], lam_re.shape[2]
    pairs = n_g // 2

    def lanes(m):
        rows = m.shape[2]
        return m.reshape(2, pairs, 2, rows, n_p).transpose(1, 0, 3, 2, 4).reshape(pairs, 2, rows, 2 * n_p)

    lam = lanes(jnp.stack([lam_re, lam_im, jnp.broadcast_to(log_dt[..., None], lam_re.shape)],
                          axis=2).astype(_F32))
    bt = jnp.stack([lanes(jnp.swapaxes(b_re, -1, -2)), lanes(jnp.swapaxes(b_im, -1, -2))], axis=2)
    ct = jnp.stack([lanes(c_re), lanes(c_im)], axis=2)
    return lam, bt.astype(_F32), ct.astype(_F32)


def _layer(x, c, positions, layer_idx, w_ada, b_ada, g_pre, w_in, lam_qk, g_subln,
           lam_re, lam_im, log_dt, b_re, b_im, c_re, c_im, d_skip,
           w_glu, b_glu, w_up_attn, w_up_ssm, w_out, g_final):
    bsz, s, d = x.shape
    n_c = s // SSM_CHUNK
    qk_w = ATTN_HEADS * 2 * HEAD_DIM
    v_w = ATTN_HEADS * V_HEAD_DIM
    u_w = lam_re.shape[1] * SSM_GROUP
    u_off = 2 * qk_w + 2 * v_w
    lam_init = 0.8 - 0.6 * math.exp(-0.3 * layer_idx)
    mod, lam, w_in_bf, wvt_bf = _prepare(c, w_ada, b_ada, lam_qk, w_in, lam_init=lam_init,
                                         v_off=2 * qk_w, v_w=v_w)
    mod3 = mod.reshape(bsz, 1, 3 * d)
    rows = min(ROW_BLOCK, s)
    pos_rows = positions.astype(_F32).reshape(bsz * (s // rows), 1, rows)
    qt, k, vt, gates, u = _inproj(x, mod3, g_pre, w_in_bf, wvt_bf, pos_rows, rows=rows,
                                  qk_w=qk_w, v_w=v_w, u_off=u_off, u_w=u_w)
    o = _attention(lam, g_subln, qt, k, vt, tq=min(Q_ROWS, s), lam_init=lam_init)
    ut = u.reshape(bsz, n_c, SSM_CHUNK, u_w).transpose(2, 3, 1, 0).reshape(SSM_CHUNK, u_w, n_c * bsz)
    dsk = d_skip.astype(_F32).reshape(u_w // (2 * SSM_GROUP), 2, SSM_GROUP, 1)
    yt = _ssm(ut, *_ssm_params(lam_re, lam_im, log_dt, b_re, b_im, c_re, c_im), dsk, bsz=bsz)
    y = yt.reshape(SSM_CHUNK, u_w, n_c, bsz).transpose(3, 2, 0, 1).reshape(bsz, s, u_w)
    return _post(o, gates, y, x, mod3, b_glu, g_final,
                 w_up_attn.astype(_BF), w_glu.astype(_BF), w_up_ssm.astype(_BF), w_out.astype(_BF),
                 rows=rows)


def kernel(x, c, positions, w_ada, b_ada, g_pre, w_in, lam_qk, g_subln, ssm_lam_re, ssm_lam_im,
           ssm_log_dt, ssm_b_re, ssm_b_im, ssm_c_re, ssm_c_im, ssm_d, w_glu, b_glu, w_up_attn,
           w_up_ssm, w_out, g_final):
    depth = w_ada.shape[0]
    assert depth == 1, "the final RMSNorm is fused into the single layer's epilogue"
    assert x.shape[1] % (SSM_CHUNK * 8) == 0
    return _layer(x, c, positions, 0, w_ada[0], b_ada[0], g_pre[0], w_in[0],
                  lam_qk[0], g_subln[0], ssm_lam_re[0], ssm_lam_im[0], ssm_log_dt[0], ssm_b_re[0],
                  ssm_b_im[0], ssm_c_re[0], ssm_c_im[0], ssm_d[0], w_glu[0], b_glu[0],
                  w_up_attn[0], w_up_ssm[0], w_out[0], g_final)
```

```python
import functools
import math

import jax
import jax.numpy as jnp
import numpy as np
from jax import lax
from jax.experimental import pallas as pl
from jax.experimental.pallas import tpu as pltpu

ATTN_HEADS = 4
HEAD_DIM = 64
V_HEAD_DIM = 2 * HEAD_DIM
ROT_DIM = HEAD_DIM // 4
ROPE_THETA = 500000.0
SSM_GROUP = 16
SSM_STATE = 64
SSM_CHUNK = 16
EPS = 1e-6
LOG2_E = math.log2(math.e)
LANES = 128
VMEM_LIMIT_BYTES = 56 * 1024 * 1024
ROW_BLOCK = 1024
Q_ROWS = 1024
KEY_BLOCK = 128
VT_PAD = 16

_HI = lax.Precision.HIGHEST
_BF = jnp.bfloat16
_F32 = jnp.float32


def _nt_dot(a, b):
    return lax.dot_general(a, b, (((1,), (1,)), ((), ())), preferred_element_type=_F32)


def _dot(a, b):
    return jnp.dot(a, b, preferred_element_type=_F32)


def _sigmoid(x):
    return 0.5 * jnp.tanh(0.5 * x) + 0.5


def _prep_kernel(c_ref, w_ref, b_ref, lamqk_ref, win_ref, o_ref, lam_ref, winb_ref, wvt_ref,
                 *, lam_init, n_mod, v_tile):
    j = pl.program_id(0)

    @pl.when(j < n_mod)
    def _():
        o_ref[...] = jnp.dot(jax.nn.silu(c_ref[...]), w_ref[...], precision=_HI,
                             preferred_element_type=_F32) + b_ref[...]

    @pl.when(j == 0)
    def _():
        lf = lamqk_ref[...]
        lam = (jnp.exp(jnp.sum(lf[0:1, :] * lf[1:2, :], axis=-1, keepdims=True))
               - jnp.exp(jnp.sum(lf[2:3, :] * lf[3:4, :], axis=-1, keepdims=True)) + lam_init)
        lam_ref[...] = jnp.broadcast_to(lam, lam_ref.shape)

    w = win_ref[...]
    winb_ref[...] = w.astype(_BF)

    @pl.when(j == v_tile)
    def _():
        wvt_ref[...] = w.T.astype(_BF)


def _prepare(c, w_ada, b_ada, lam_qk, w_in, *, lam_init, v_off, v_w):
    bsz, d = c.shape
    n = w_ada.shape[1]
    n_in = w_in.shape[1]
    tn = d
    n_mod = n // tn
    assert v_off % v_w == 0 and n_in % v_w == 0 and n_in // v_w >= n_mod
    last = n_mod - 1
    return pl.pallas_call(
        functools.partial(_prep_kernel, lam_init=lam_init, n_mod=n_mod, v_tile=v_off // v_w),
        grid=(n_in // v_w,),
        in_specs=[pl.BlockSpec((bsz, d), lambda j: (0, 0)),
                  pl.BlockSpec((d, tn), lambda j: (0, jnp.minimum(j, last))),
                  pl.BlockSpec((1, tn), lambda j: (0, jnp.minimum(j, last))),
                  pl.BlockSpec(lam_qk.shape, lambda j: (0, 0)),
                  pl.BlockSpec((d, v_w), lambda j: (0, j))],
        out_specs=[pl.BlockSpec((bsz, tn), lambda j: (0, jnp.minimum(j, last))),
                   pl.BlockSpec((1, LANES), lambda j: (0, 0)),
                   pl.BlockSpec((d, v_w), lambda j: (0, j)),
                   pl.BlockSpec((v_w, d), lambda j: (0, 0))],
        out_shape=[jax.ShapeDtypeStruct((bsz, n), _F32), jax.ShapeDtypeStruct((1, LANES), _F32),
                   jax.ShapeDtypeStruct((d, n_in), _BF), jax.ShapeDtypeStruct((v_w, d), _BF)],
        compiler_params=pltpu.CompilerParams(
            dimension_semantics=("arbitrary",), vmem_limit_bytes=VMEM_LIMIT_BYTES),
        name="adaln_mod",
    )(c, w_ada, b_ada.reshape(1, n), lam_qk, w_in)


def _inproj_kernel(x_ref, mod_ref, gpre_ref, w_ref, wvt_ref, pos_ref, freq_ref, spread_ref,
                   qt_ref, k_ref, vt_ref, gates_ref, u_ref, *, d, qk_w, v_w, u_off, u_w):
    shift = mod_ref[0, :, 0:d]
    scale = mod_ref[0, :, d:2 * d]
    x = x_ref[0]
    inv = lax.rsqrt(jnp.mean(x * x, axis=-1, keepdims=True) + EPS)
    h = ((x * inv) * (gpre_ref[...] * (1.0 + scale)) + shift).astype(_BF)

    rows = x.shape[0]
    half = ROT_DIM // 2
    ang = freq_ref[...] * pos_ref[0]
    cs = jnp.concatenate([jnp.cos(ang), jnp.sin(ang)], axis=0)
    pieces = []
    rest = cs
    for _ in range(3):
        piece = rest.astype(_BF).astype(_F32)
        rest = rest - piece
        pieces.append(piece)
    sel = spread_ref[...].astype(_BF)
    tab = _dot(jnp.concatenate(pieces, axis=0).T.astype(_BF), jnp.concatenate([sel] * 3, axis=0))
    lane = lax.broadcasted_iota(jnp.int32, (1, LANES), 1)
    reps = qk_w // LANES
    rot_c = jnp.concatenate([jnp.where(lane % HEAD_DIM < ROT_DIM, tab[:, 0:LANES], 1.0)] * reps,
                            axis=1)
    rot_s = jnp.concatenate([tab[:, LANES:2 * LANES]] * reps, axis=1)
    low = lax.broadcasted_iota(jnp.int32, (1, qk_w), 1) % HEAD_DIM < half

    def rope(t):
        up = pltpu.roll(t, qk_w - half, 1)
        dn = pltpu.roll(t, half, 1)
        return t * rot_c + jnp.where(low, up, dn) * rot_s

    v_off = 2 * qk_w
    vt = _nt_dot(wvt_ref[...], h).astype(_BF)
    pad = (lax.broadcasted_iota(jnp.int32, (VT_PAD, rows), 0) == 0).astype(_BF)
    for hd in range(v_w // V_HEAD_DIM):
        vt_ref[0, hd, 0:V_HEAD_DIM, :] = vt[hd * V_HEAD_DIM:(hd + 1) * V_HEAD_DIM, :]
        vt_ref[0, hd, V_HEAD_DIM:V_HEAD_DIM + VT_PAD, :] = pad
    u_ref[0] = _dot(h, w_ref[:, u_off:u_off + u_w]).astype(_BF)
    qt = rope(_dot(h, w_ref[:, 0:qk_w]) * (HEAD_DIM ** -0.5 * LOG2_E)).T.astype(_BF)
    hw = 2 * HEAD_DIM
    for hd in range(qk_w // hw):
        qt_ref[0, hd] = qt[hd * hw:(hd + 1) * hw, :]
    k_ref[0] = rope(_dot(h, w_ref[:, qk_w:2 * qk_w])).astype(_BF)
    za_off = v_off + v_w
    z_a = _dot(h, w_ref[:, za_off:za_off + v_w])
    gates_ref[0, :, 0:v_w] = (z_a * _sigmoid(z_a)).astype(_BF)
    zs_off = u_off + u_w
    z_s = _dot(h, w_ref[:, zs_off:zs_off + u_w])
    gates_ref[0, :, v_w:v_w + u_w] = (z_s * _sigmoid(z_s)).astype(_BF)
    g_off = zs_off + u_w
    rest = w_ref.shape[1] - g_off
    piece = 512
    for p in range(rest // piece):
        gates_ref[0, :, v_w + u_w + p * piece:v_w + u_w + (p + 1) * piece] = _sigmoid(_dot(
            h, w_ref[:, g_off + p * piece:g_off + (p + 1) * piece])).astype(_BF)


def _inproj(x, mod3, g_pre, w_in_bf, wvt_bf, pos_rows, *, rows, qk_w, v_w, u_off, u_w):
    bsz, s, d = x.shape
    n_in = w_in_bf.shape[1]
    gates_w = n_in - 2 * qk_w - v_w - u_w
    n_heads = v_w // V_HEAD_DIM
    vt_rows = V_HEAD_DIM + VT_PAD
    kern = functools.partial(_inproj_kernel, d=d, qk_w=qk_w, v_w=v_w, u_off=u_off, u_w=u_w)
    freq, spread = _rope_constants()
    seq_spec = lambda w: pl.BlockSpec((1, rows, w), lambda b, t: (b, t, 0))
    return pl.pallas_call(
        kern,
        grid=(bsz, s // rows),
        in_specs=[seq_spec(d),
                  pl.BlockSpec((1, 1, 3 * d), lambda b, t: (b, 0, 0)),
                  pl.BlockSpec((1, d), lambda b, t: (0, 0)),
                  pl.BlockSpec((d, n_in), lambda b, t: (0, 0), pipeline_mode=pl.Buffered(1)),
                  pl.BlockSpec((v_w, d), lambda b, t: (0, 0), pipeline_mode=pl.Buffered(1)),
                  pl.BlockSpec((1, 1, rows), lambda b, t: (b * (s // rows) + t, 0, 0)),
                  pl.BlockSpec(freq.shape, lambda b, t: (0, 0)),
                  pl.BlockSpec(spread.shape, lambda b, t: (0, 0))],
        out_specs=[pl.BlockSpec((1, n_heads, 2 * HEAD_DIM, rows), lambda b, t: (b, 0, 0, t)),
                   seq_spec(qk_w),
                   pl.BlockSpec((1, n_heads, vt_rows, rows), lambda b, t: (b, 0, 0, t)),
                   seq_spec(gates_w), seq_spec(u_w)],
        out_shape=[jax.ShapeDtypeStruct((bsz, n_heads, 2 * HEAD_DIM, s), _BF),
                   jax.ShapeDtypeStruct((bsz, s, qk_w), _BF),
                   jax.ShapeDtypeStruct((bsz, n_heads, vt_rows, s), _BF),
                   jax.ShapeDtypeStruct((bsz, s, gates_w), _BF),
                   jax.ShapeDtypeStruct((bsz, s, u_w), _BF)],
        compiler_params=pltpu.CompilerParams(
            dimension_semantics=("arbitrary", "arbitrary"), vmem_limit_bytes=VMEM_LIMIT_BYTES),
        name="norm_inproj",
    )(x, mod3, g_pre.reshape(1, d), w_in_bf, wvt_bf, pos_rows, freq, spread)


def _attn_kernel(tab_ref, lam_ref, gsub_ref, qt_ref, k_ref, vt_ref, o_ref, *scratch, lam_init):
    s_ref = (scratch[0:2], scratch[2:4])
    e_ref = (scratch[4:6], scratch[6:8])
    m_ref = (scratch[8:10], scratch[10:12])
    ev_ref = (scratch[12:14], scratch[14:16])
    n_keys, tq = s_ref[0][0].shape
    tk = KEY_BLOCK
    n_kb = n_keys // tk
    step = pl.program_id(0)

    @pl.when(step == 0)
    def _():
        for ref in scratch:
            ref[...] = jnp.zeros_like(ref)

    lam = lam_ref[0:1, 0:1]

    def tick(new, old):
        ot = None
        for mp in range(2):
            ev = ev_ref[old][mp][...]
            nrm = ev[0:V_HEAD_DIM, :] * (1.0 / ev[V_HEAD_DIM:V_HEAD_DIM + 1, :])
            ot = nrm if ot is None else ot - lam * nrm
        inv = lax.rsqrt(jnp.mean(ot * ot, axis=0, keepdims=True) + EPS)
        o_ref[0] = (((ot * inv) * gsub_ref[...]) * (1.0 - lam_init)).T.astype(_BF)
        qt = qt_ref[0, 0]
        first = lax.broadcasted_iota(jnp.int32, (2 * HEAD_DIM, 1), 0) < HEAD_DIM
        zero = jnp.zeros_like(qt)
        qmt = (jnp.where(first, qt, zero), jnp.where(first, zero, qt))
        m_rows = [jnp.max(m_ref[old][mp][...], axis=0, keepdims=True) for mp in range(2)]
        n_part = 4 if n_kb % 4 == 0 else 1
        for mp in range(2):
            ev = None
            m = jnp.full((8, tq), -jnp.inf, _F32)
            for kb in range(n_kb):
                keys = slice(kb * tk, (kb + 1) * tk)
                if kb % (n_kb // n_part) == 0:
                    part_keys = slice(kb * tk, (kb + n_kb // n_part) * tk)
                    part = _dot(vt_ref[0, 0, :, part_keys], e_ref[new][mp][part_keys, :])
                    ev = part if ev is None else ev + part
                e_ref[old][mp][keys, :] = jnp.exp2(s_ref[old][mp][keys, :] - m_rows[mp]).astype(_BF)
                st = _dot(k_ref[0, keys, :], qmt[mp])
                s_ref[new][mp][keys, :] = st
                m = jnp.maximum(m, jnp.max(st.reshape(tk // 8, 8, tq), axis=0))
            m_ref[new][mp][...] = m
            ev_ref[new][mp][...] = ev

    @pl.when(step % 2 == 0)
    def _():
        tick(0, 1)

    @pl.when(step % 2 == 1)
    def _():
        tick(1, 0)


def _attention(lam, g_subln, qt, k, vt, *, tq, lam_init):
    bsz, n_heads, hw, s = qt.shape
    _, _, vt_rows, _ = vt.shape
    n_q = s // tq
    n_items = bsz * n_heads * n_q
    kern = functools.partial(_attn_kernel, lam_init=lam_init)
    n_steps = n_items + 3

    def item(j):
        j = np.clip(j, 0, n_items - 1)
        return j // (n_heads * n_q), (j // n_q) % n_heads, j % n_q

    g = np.arange(n_steps)
    tab = jnp.asarray(np.stack([*item(g), *item(g - 2), *item(g - 3)]).astype(np.int32))

    return pl.pallas_call(
        kern,
        grid_spec=pltpu.PrefetchScalarGridSpec(
            num_scalar_prefetch=1,
            grid=(n_steps,),
            in_specs=[pl.BlockSpec(lam.shape, lambda g, t: (0, 0)),
                      pl.BlockSpec((V_HEAD_DIM, 1), lambda g, t: (0, 0)),
                      pl.BlockSpec((1, 1, hw, tq), lambda g, t: (t[0, g], t[1, g], 0, t[2, g])),
                      pl.BlockSpec((1, s, hw), lambda g, t: (t[0, g], 0, t[1, g])),
                      pl.BlockSpec((1, 1, vt_rows, s), lambda g, t: (t[3, g], t[4, g], 0, 0))],
            out_specs=pl.BlockSpec((1, tq, V_HEAD_DIM), lambda g, t: (t[6, g], t[8, g], t[7, g])),
            scratch_shapes=([pltpu.VMEM((s, tq), _F32)] * 4 + [pltpu.VMEM((s, tq), _BF)] * 4
                            + [pltpu.VMEM((8, tq), _F32)] * 4 + [pltpu.VMEM((vt_rows, tq), _F32)] * 4)),
        out_shape=jax.ShapeDtypeStruct((bsz, s, n_heads * V_HEAD_DIM), _BF),
        compiler_params=pltpu.CompilerParams(
            dimension_semantics=("arbitrary",), vmem_limit_bytes=VMEM_LIMIT_BYTES),
        name="diff_attention",
    )(tab, lam, g_subln.reshape(V_HEAD_DIM, 1), qt, k, vt)


def _ssm_kernel(ut_ref, lam_ref, bt_ref, ct_ref, dsk_ref, yt_ref, at_ref, z_ref, xp_ref,
                *, bsz, n_c):
    el = SSM_CHUNK
    cw = el * SSM_GROUP
    sw = 2 * SSM_STATE
    n_cols = n_c * bsz
    in_group = [lax.broadcasted_iota(jnp.int32, (1, sw), 1) // SSM_STATE == gl for gl in range(2)]

    def group_rows(ref, gl):
        return ref[:, gl * SSM_GROUP:(gl + 1) * SSM_GROUP, :].reshape(cw, n_cols)

    def per_row(rows, cols):
        return (jnp.broadcast_to(rows[:, None, :], (el, SSM_GROUP, sw)),
                jnp.broadcast_to(cols[None, :, :], (el, SSM_GROUP, sw)))

    def cmul_rows(p_r, p_i, m_r, m_i):
        pr3, mr3 = per_row(p_r, m_r)
        pi3, mi3 = per_row(p_i, m_i)
        return (pr3 * mr3 - pi3 * mi3).reshape(cw, sw), (pr3 * mi3 + pi3 * mr3).reshape(cw, sw)

    idx = lax.broadcasted_iota(jnp.int32, (el, 1), 0).astype(_F32)
    ops = []
    for e in range(2):
        lr = jnp.minimum(lam_ref[0, e, 0:1, :], -1e-4)
        li = lam_ref[0, e, 1:2, :]
        dt = jnp.exp(lam_ref[0, e, 2:3, :])

        def power(k, lr=lr, li=li, dt=dt):
            mag = jnp.exp(k * (lr * dt))
            return mag * jnp.cos(k * (li * dt)), mag * jnp.sin(k * (li * dt))

        one_r, one_i = power(1.0)
        den = lr * lr + li * li
        nr, ni = one_r - 1.0, one_i
        coef_r = (nr * lr + ni * li) / den
        coef_i = (ni * lr - nr * li) / den
        b_r, b_i = bt_ref[0, e, 0], bt_ref[0, e, 1]
        bb_r = coef_r * b_r - coef_i * b_i
        bb_i = coef_r * b_i + coef_i * b_r
        c_r, c_i = ct_ref[0, e, 0], ct_ref[0, e, 1]
        ops.append(dict(power=power, bb=(bb_r, bb_i), c=(c_r, c_i)))
    fwd, bwd = ops

    zf_r, zf_i = cmul_rows(*fwd["power"](el - 1.0 - idx), *fwd["bb"])
    zb_r, zb_i = cmul_rows(*bwd["power"](idx), *bwd["bb"])
    yf_r, yf_i = cmul_rows(*fwd["power"](idx + 1.0), *fwd["c"])
    yb_r, yb_i = cmul_rows(*bwd["power"](el - idx), *bwd["c"])
    kf_r, kf_i = cmul_rows(*fwd["power"](idx), *fwd["c"])
    kb_r, kb_i = cmul_rows(*bwd["power"](idx), *bwd["c"])

    def pair_slot(parts, gl):
        return jnp.concatenate([jnp.where(in_group[gl], p, 0.0) for p in parts], axis=1).astype(_BF)

    for gl in range(2):
        at_ref[gl] = group_rows(ut_ref, gl).T
    z = (_dot(at_ref[0], pair_slot([zf_r, zf_i, zb_r, zb_i], 0))
         + _dot(at_ref[1], pair_slot([zf_r, zf_i, zb_r, zb_i], 1)))
    for comp in range(4):
        z_ref[comp] = z[:, comp * sw:(comp + 1) * sw]

    a_fr, a_fi = fwd["power"](float(el))
    a_br, a_bi = bwd["power"](float(el))

    def step(c, carry):
        fr, fi, br, bi = carry
        rows_f = pl.ds(pl.multiple_of(c * bsz, bsz), bsz)
        rows_b = pl.ds(pl.multiple_of((n_c - 1 - c) * bsz, bsz), bsz)
        xp_ref[0, rows_f, :] = fr
        xp_ref[1, rows_f, :] = fi
        xp_ref[2, rows_b, :] = br
        xp_ref[3, rows_b, :] = bi
        zfr = z_ref[0, rows_f, :]
        zfi = z_ref[1, rows_f, :]
        zbr = z_ref[2, rows_b, :]
        zbi = z_ref[3, rows_b, :]
        return (a_fr * fr - a_fi * fi + zfr, a_fr * fi + a_fi * fr + zfi,
                a_br * br - a_bi * bi + zbr, a_br * bi + a_bi * br + zbi)

    zero = jnp.zeros((bsz, sw), _F32)
    lax.fori_loop(0, n_c, step, (zero, zero, zero, zero), unroll=4)
    xp = jnp.concatenate([xp_ref[comp] for comp in range(4)], axis=1).astype(_BF)

    spread = (lax.broadcasted_iota(jnp.int32, (SSM_GROUP, cw), 1) % SSM_GROUP
              == lax.broadcasted_iota(jnp.int32, (SSM_GROUP, cw), 0)).astype(_BF)
    per_tile = LANES // SSM_GROUP
    t_in_tile = lax.broadcasted_iota(jnp.int32, (1, LANES), 1) // SSM_GROUP

    def lag_kernels(k_r, k_i, bb, gl):
        lhs = jnp.concatenate([jnp.where(in_group[gl], k_r, 0.0),
                               jnp.where(in_group[gl], -k_i, 0.0)], axis=1).astype(_BF)
        rhs = jnp.concatenate([bb[0], bb[1]], axis=1).astype(_BF)
        return _nt_dot(lhs, rhs)

    for gl in range(2):
        ktf = _dot(lag_kernels(kf_r, kf_i, fwd["bb"], gl).astype(_BF), spread)
        ktb = _dot(lag_kernels(kb_r, kb_i, bwd["bb"], gl).astype(_BF), spread)

        def column_block(tile, t):
            lanes = slice(tile * LANES, (tile + 1) * LANES)

            def blk(kt, k):
                return kt[k * SSM_GROUP:(k + 1) * SSM_GROUP, lanes]

            rows = [blk(ktb, t - s_) if s_ < t else blk(ktf, s_ - t) if s_ > t
                    else blk(ktf, 0) + blk(ktb, 0) for s_ in range(el)]
            return jnp.concatenate(rows, axis=0)

        tiles = []
        for tile in range(cw // LANES):
            acc = column_block(tile, tile * per_tile)
            for j in range(1, per_tile):
                acc = jnp.where(t_in_tile == j, column_block(tile, tile * per_tile + j), acc)
            tiles.append(acc)
        tm = jnp.concatenate(tiles, axis=1)
        u_g = group_rows(ut_ref, gl)
        yt = _dot(tm.astype(_BF), u_g) + _nt_dot(pair_slot([yf_r, -yf_i, yb_r, -yb_i], gl), xp)
        d_rows = jnp.broadcast_to(dsk_ref[0, gl][None], (el, SSM_GROUP, 1)).reshape(cw, 1)
        yt = yt + d_rows * u_g.astype(_F32)
        yt_ref[:, gl * SSM_GROUP:(gl + 1) * SSM_GROUP, :] = yt.reshape(
            el, SSM_GROUP, n_cols).astype(_BF)


def _ssm(ut, lam, bt, ct, dsk, *, bsz):
    _, u_w, n_cols = ut.shape
    n_c = n_cols // bsz
    n_pairs = u_w // (2 * SSM_GROUP)
    cw = SSM_CHUNK * SSM_GROUP
    sw = 2 * SSM_STATE
    kern = functools.partial(_ssm_kernel, bsz=bsz, n_c=n_c)
    blk = pl.BlockSpec((SSM_CHUNK, 2 * SSM_GROUP, n_cols), lambda g: (0, g, 0))
    par = pl.BlockSpec((1, 2, 2, SSM_GROUP, sw), lambda g: (g, 0, 0, 0, 0))
    return pl.pallas_call(
        kern,
        grid=(n_pairs,),
        in_specs=[blk,
                  pl.BlockSpec((1, 2, 3, sw), lambda g: (g, 0, 0, 0)),
                  par, par,
                  pl.BlockSpec((1, 2, SSM_GROUP, 1), lambda g: (g, 0, 0, 0))],
        out_specs=blk,
        out_shape=jax.ShapeDtypeStruct(ut.shape, _BF),
        scratch_shapes=[pltpu.VMEM((2, n_cols, cw), _BF),
                        pltpu.VMEM((4, n_cols, sw), _F32),
                        pltpu.VMEM((4, n_cols, sw), _F32)],
        compiler_params=pltpu.CompilerParams(
            dimension_semantics=("arbitrary",), vmem_limit_bytes=VMEM_LIMIT_BYTES),
        name="s5_chunked",
    )(ut, lam, bt, ct, dsk)


def _post_kernel(o_ref, gates_ref, y_ref, x_ref, mod_ref, bglu_ref, gfin_ref,
                 wua_ref, wglu_ref, wus_ref, wout_ref, out_ref, *, d, aw, sw):
    y = jax.nn.gelu(y_ref[0].astype(_F32))
    y = y * _sigmoid(_dot(y.astype(_BF), wglu_ref[...]) + bglu_ref[...])
    s_br = _dot((y * gates_ref[0, :, aw:aw + sw].astype(_F32)).astype(_BF), wus_ref[...])
    a_br = _dot((o_ref[0].astype(_F32) * gates_ref[0, :, 0:aw].astype(_F32)).astype(_BF),
                wua_ref[...])
    g_off = aw + sw
    merged = (gates_ref[0, :, g_off:g_off + d].astype(_F32) * a_br
              + gates_ref[0, :, g_off + d:g_off + 2 * d].astype(_F32) * s_br)
    r = _dot(merged.astype(_BF), wout_ref[...])
    xo = x_ref[0] + mod_ref[0, :, 2 * d:3 * d] * r
    inv = lax.rsqrt(jnp.mean(xo * xo, axis=-1, keepdims=True) + EPS)
    out_ref[0] = (xo * inv) * gfin_ref[...]


def _post(o, gates, y, x, mod3, b_glu, g_final, wua, wglu, wus, wout, *, rows):
    bsz, s, d = x.shape
    aw = o.shape[2]
    sw = y.shape[2]
    kern = functools.partial(_post_kernel, d=d, aw=aw, sw=sw)
    full = lambda a: pl.BlockSpec(a.shape, lambda b, t: (0,) * a.ndim)
    seq_spec = lambda w: pl.BlockSpec((1, rows, w), lambda b, t: (b, t, 0))
    bgl = b_glu.reshape(1, sw)
    gfi = g_final.reshape(1, d)
    return pl.pallas_call(
        kern,
        grid=(bsz, s // rows),
        in_specs=[seq_spec(aw), seq_spec(gates.shape[2]), seq_spec(sw), seq_spec(d),
                  pl.BlockSpec((1, 1, 3 * d), lambda b, t: (b, 0, 0)),
                  full(bgl), full(gfi), full(wua), full(wglu), full(wus), full(wout)],
        out_specs=seq_spec(d),
        out_shape=jax.ShapeDtypeStruct(x.shape, _F32),
        compiler_params=pltpu.CompilerParams(
            dimension_semantics=("arbitrary", "arbitrary"), vmem_limit_bytes=VMEM_LIMIT_BYTES),
        name="merge_out",
    )(o, gates, y, x, mod3, bgl, gfi, wua, wglu, wus, wout)


def _rope_constants():
    half = ROT_DIM // 2
    freq = (ROPE_THETA ** (-jnp.arange(half, dtype=_F32) * 2.0 / ROT_DIM)).reshape(half, 1)
    in_head = jnp.arange(LANES) % HEAD_DIM
    j = jnp.arange(half)[:, None]
    cos_sel = ((in_head == j) | (in_head == half + j)).astype(_F32)
    sin_sel = (in_head == half + j).astype(_F32) - (in_head == j).astype(_F32)
    zero = jnp.zeros_like(cos_sel)
    return freq, jnp.concatenate([jnp.concatenate([cos_sel, zero], axis=1),
                                  jnp.concatenate([zero, sin_sel], axis=1)], axis=0)


def _ssm_params(lam_re, lam_im, log_dt, b_re, b_im, c_re, c_im):
    n_g, n_p = lam_re.shape[1], lam_re.shape[2]
    pairs = n_g // 2

    def lanes(m):
        rows = m.shape[2]
        return m.reshape(2, pairs, 2, rows, n_p).transpose(1, 0, 3, 2, 4).reshape(pairs, 2, rows, 2 * n_p)

    lam = lanes(jnp.stack([lam_re, lam_im, jnp.broadcast_to(log_dt[..., None], lam_re.shape)],
                          axis=2).astype(_F32))
    bt = jnp.stack([lanes(jnp.swapaxes(b_re, -1, -2)), lanes(jnp.swapaxes(b_im, -1, -2))], axis=2)
    ct = jnp.stack([lanes(c_re), lanes(c_im)], axis=2)
    return lam, bt.astype(_F32), ct.astype(_F32)


def _layer(x, c, positions, layer_idx, w_ada, b_ada, g_pre, w_in, lam_qk, g_subln,
           lam_re, lam_im, log_dt, b_re, b_im, c_re, c_im, d_skip,
           w_glu, b_glu, w_up_attn, w_up_ssm, w_out, g_final):
    bsz, s, d = x.shape
    n_c = s // SSM_CHUNK
    qk_w = ATTN_HEADS * 2 * HEAD_DIM
    v_w = ATTN_HEADS * V_HEAD_DIM
    u_w = lam_re.shape[1] * SSM_GROUP
    u_off = 2 * qk_w + 2 * v_w
    lam_init = 0.8 - 0.6 * math.exp(-0.3 * layer_idx)
    mod, lam, w_in_bf, wvt_bf = _prepare(c, w_ada, b_ada, lam_qk, w_in, lam_init=lam_init,
                                         v_off=2 * qk_w, v_w=v_w)
    mod3 = mod.reshape(bsz, 1, 3 * d)
    rows = min(ROW_BLOCK, s)
    pos_rows = positions.astype(_F32).reshape(bsz * (s // rows), 1, rows)
    qt, k, vt, gates, u = _inproj(x, mod3, g_pre, w_in_bf, wvt_bf, pos_rows, rows=rows,
                                  qk_w=qk_w, v_w=v_w, u_off=u_off, u_w=u_w)
    o = _attention(lam, g_subln, qt, k, vt, tq=min(Q_ROWS, s), lam_init=lam_init)
    ut = u.reshape(bsz, n_c, SSM_CHUNK, u_w).transpose(2, 3, 1, 0).reshape(SSM_CHUNK, u_w, n_c * bsz)
    dsk = d_skip.astype(_F32).reshape(u_w // (2 * SSM_GROUP), 2, SSM_GROUP, 1)
    yt = _ssm(ut, *_ssm_params(lam_re, lam_im, log_dt, b_re, b_im, c_re, c_im), dsk, bsz=bsz)
    y = yt.reshape(SSM_CHUNK, u_w, n_c, bsz).transpose(3, 2, 0, 1).reshape(bsz, s, u_w)
    return _post(o, gates, y, x, mod3, b_glu, g_final,
                 w_up_attn.astype(_BF), w_glu.astype(_BF), w_up_ssm.astype(_BF), w_out.astype(_BF),
                 rows=rows)


def kernel(x, c, positions, w_ada, b_ada, g_pre, w_in, lam_qk, g_subln, ssm_lam_re, ssm_lam_im,
           ssm_log_dt, ssm_b_re, ssm_b_im, ssm_c_re, ssm_c_im, ssm_d, w_glu, b_glu, w_up_attn,
           w_up_ssm, w_out, g_final):
    depth = w_ada.shape[0]
    assert depth == 1, "the final RMSNorm is fused into the single layer's epilogue"
    assert x.shape[1] % (SSM_CHUNK * 8) == 0
    return _layer(x, c, positions, 0, w_ada[0], b_ada[0], g_pre[0], w_in[0],
                  lam_qk[0], g_subln[0], ssm_lam_re[0], ssm_lam_im[0], ssm_log_dt[0], ssm_b_re[0],
                  ssm_b_im[0], ssm_c_re[0], ssm_c_im[0], ssm_d[0], w_glu[0], b_glu[0],
                  w_up_attn[0], w_up_ssm[0], w_out[0], g_final)
```

```python
import functools
import math

import jax
import jax.numpy as jnp
import numpy as np
from jax import lax
from jax.experimental import pallas as pl
from jax.experimental.pallas import tpu as pltpu

ATTN_HEADS = 4
HEAD_DIM = 64
V_HEAD_DIM = 2 * HEAD_DIM
ROT_DIM = HEAD_DIM // 4
ROPE_THETA = 500000.0
SSM_GROUP = 16
SSM_STATE = 64
SSM_CHUNK = 16
EPS = 1e-6
LOG2_E = math.log2(math.e)
LANES = 128
VMEM_LIMIT_BYTES = 56 * 1024 * 1024
ROW_BLOCK = 1024
Q_ROWS = 1024
KEY_BLOCK = 128
VT_PAD = 16

_HI = lax.Precision.HIGHEST
_BF = jnp.bfloat16
_F32 = jnp.float32


def _nt_dot(a, b):
    return lax.dot_general(a, b, (((1,), (1,)), ((), ())), preferred_element_type=_F32)


def _dot(a, b):
    return jnp.dot(a, b, preferred_element_type=_F32)


def _sigmoid(x):
    return 0.5 * jnp.tanh(0.5 * x) + 0.5


def _prep_kernel(c_ref, w_ref, b_ref, lamqk_ref, win_ref, o_ref, lam_ref, winb_ref, wvt_ref,
                 *, lam_init, n_mod, v_tile):
    j = pl.program_id(0)

    @pl.when(j < n_mod)
    def _():
        o_ref[...] = jnp.dot(jax.nn.silu(c_ref[...]), w_ref[...], precision=_HI,
                             preferred_element_type=_F32) + b_ref[...]

    @pl.when(j == 0)
    def _():
        lf = lamqk_ref[...]
        lam = (jnp.exp(jnp.sum(lf[0:1, :] * lf[1:2, :], axis=-1, keepdims=True))
               - jnp.exp(jnp.sum(lf[2:3, :] * lf[3:4, :], axis=-1, keepdims=True)) + lam_init)
        lam_ref[...] = jnp.broadcast_to(lam, lam_ref.shape)

    w = win_ref[...]
    winb_ref[...] = w.astype(_BF)

    @pl.when(j == v_tile)
    def _():
        wvt_ref[...] = w.T.astype(_BF)


def _prepare(c, w_ada, b_ada, lam_qk, w_in, *, lam_init, v_off, v_w):
    bsz, d = c.shape
    n = w_ada.shape[1]
    n_in = w_in.shape[1]
    tn = d
    n_mod = n // tn
    assert v_off % v_w == 0 and n_in % v_w == 0 and n_in // v_w >= n_mod
    last = n_mod - 1
    return pl.pallas_call(
        functools.partial(_prep_kernel, lam_init=lam_init, n_mod=n_mod, v_tile=v_off // v_w),
        grid=(n_in // v_w,),
        in_specs=[pl.BlockSpec((bsz, d), lambda j: (0, 0)),
                  pl.BlockSpec((d, tn), lambda j: (0, jnp.minimum(j, last))),
                  pl.BlockSpec((1, tn), lambda j: (0, jnp.minimum(j, last))),
                  pl.BlockSpec(lam_qk.shape, lambda j: (0, 0)),
                  pl.BlockSpec((d, v_w), lambda j: (0, j))],
        out_specs=[pl.BlockSpec((bsz, tn), lambda j: (0, jnp.minimum(j, last))),
                   pl.BlockSpec((1, LANES), lambda j: (0, 0)),
                   pl.BlockSpec((d, v_w), lambda j: (0, j)),
                   pl.BlockSpec((v_w, d), lambda j: (0, 0))],
        out_shape=[jax.ShapeDtypeStruct((bsz, n), _F32), jax.ShapeDtypeStruct((1, LANES), _F32),
                   jax.ShapeDtypeStruct((d, n_in), _BF), jax.ShapeDtypeStruct((v_w, d), _BF)],
        compiler_params=pltpu.CompilerParams(
            dimension_semantics=("arbitrary",), vmem_limit_bytes=VMEM_LIMIT_BYTES),
        name="adaln_mod",
    )(c, w_ada, b_ada.reshape(1, n), lam_qk, w_in)


def _inproj_kernel(x_ref, mod_ref, gpre_ref, w_ref, wvt_ref, pos_ref, freq_ref, spread_ref,
                   qt_ref, k_ref, vt_ref, gates_ref, u_ref, *, d, qk_w, v_w, u_off, u_w):
    shift = mod_ref[0, :, 0:d]
    scale = mod_ref[0, :, d:2 * d]
    x = x_ref[0]
    inv = lax.rsqrt(jnp.mean(x * x, axis=-1, keepdims=True) + EPS)
    h = ((x * inv) * (gpre_ref[...] * (1.0 + scale)) + shift).astype(_BF)

    rows = x.shape[0]
    half = ROT_DIM // 2
    ang = freq_ref[...] * pos_ref[0]
    cs = jnp.concatenate([jnp.cos(ang), jnp.sin(ang)], axis=0)
    pieces = []
    rest = cs
    for _ in range(3):
        piece = rest.astype(_BF).astype(_F32)
        rest = rest - piece
        pieces.append(piece)
    sel = spread_ref[...].astype(_BF)
    tab = _dot(jnp.concatenate(pieces, axis=0).T.astype(_BF), jnp.concatenate([sel] * 3, axis=0))
    lane = lax.broadcasted_iota(jnp.int32, (1, LANES), 1)
    reps = qk_w // LANES
    rot_c = jnp.concatenate([jnp.where(lane % HEAD_DIM < ROT_DIM, tab[:, 0:LANES], 1.0)] * reps,
                            axis=1)
    rot_s = jnp.concatenate([tab[:, LANES:2 * LANES]] * reps, axis=1)
    low = lax.broadcasted_iota(jnp.int32, (1, qk_w), 1) % HEAD_DIM < half

    def rope(t):
        up = pltpu.roll(t, qk_w - half, 1)
        dn = pltpu.roll(t, half, 1)
        return t * rot_c + jnp.where(low, up, dn) * rot_s

    v_off = 2 * qk_w
    vt = _nt_dot(wvt_ref[...], h).astype(_BF)
    pad = (lax.broadcasted_iota(jnp.int32, (VT_PAD, rows), 0) == 0).astype(_BF)
    for hd in range(v_w // V_HEAD_DIM):
        vt_ref[0, hd, 0:V_HEAD_DIM, :] = vt[hd * V_HEAD_DIM:(hd + 1) * V_HEAD_DIM, :]
        vt_ref[0, hd, V_HEAD_DIM:V_HEAD_DIM + VT_PAD, :] = pad
    u_ref[0] = _dot(h, w_ref[:, u_off:u_off + u_w]).astype(_BF)
    qt = rope(_dot(h, w_ref[:, 0:qk_w]) * (HEAD_DIM ** -0.5 * LOG2_E)).T.astype(_BF)
    hw = 2 * HEAD_DIM
    for hd in range(qk_w // hw):
        qt_ref[0, hd] = qt[hd * hw:(hd + 1) * hw, :]
    kk = rope(_dot(h, w_ref[:, qk_w:2 * qk_w])).astype(_BF)
    for hd in range(qk_w // hw):
        k_ref[0, hd] = kk[:, hd * hw:(hd + 1) * hw]
    za_off = v_off + v_w
    z_a = _dot(h, w_ref[:, za_off:za_off + v_w])
    gates_ref[0, :, 0:v_w] = (z_a * _sigmoid(z_a)).astype(_BF)
    zs_off = u_off + u_w
    z_s = _dot(h, w_ref[:, zs_off:zs_off + u_w])
    gates_ref[0, :, v_w:v_w + u_w] = (z_s * _sigmoid(z_s)).astype(_BF)
    g_off = zs_off + u_w
    rest = w_ref.shape[1] - g_off
    piece = 512
    for p in range(rest // piece):
        gates_ref[0, :, v_w + u_w + p * piece:v_w + u_w + (p + 1) * piece] = _sigmoid(_dot(
            h, w_ref[:, g_off + p * piece:g_off + (p + 1) * piece])).astype(_BF)


def _inproj(x, mod3, g_pre, w_in_bf, wvt_bf, pos_rows, *, rows, qk_w, v_w, u_off, u_w):
    bsz, s, d = x.shape
    n_in = w_in_bf.shape[1]
    gates_w = n_in - 2 * qk_w - v_w - u_w
    n_heads = v_w // V_HEAD_DIM
    vt_rows = V_HEAD_DIM + VT_PAD
    kern = functools.partial(_inproj_kernel, d=d, qk_w=qk_w, v_w=v_w, u_off=u_off, u_w=u_w)
    freq, spread = _rope_constants()
    seq_spec = lambda w: pl.BlockSpec((1, rows, w), lambda b, t: (b, t, 0))
    return pl.pallas_call(
        kern,
        grid=(bsz, s // rows),
        in_specs=[seq_spec(d),
                  pl.BlockSpec((1, 1, 3 * d), lambda b, t: (b, 0, 0)),
                  pl.BlockSpec((1, d), lambda b, t: (0, 0)),
                  pl.BlockSpec((d, n_in), lambda b, t: (0, 0), pipeline_mode=pl.Buffered(1)),
                  pl.BlockSpec((v_w, d), lambda b, t: (0, 0), pipeline_mode=pl.Buffered(1)),
                  pl.BlockSpec((1, 1, rows), lambda b, t: (b * (s // rows) + t, 0, 0)),
                  pl.BlockSpec(freq.shape, lambda b, t: (0, 0)),
                  pl.BlockSpec(spread.shape, lambda b, t: (0, 0))],
        out_specs=[pl.BlockSpec((1, n_heads, 2 * HEAD_DIM, rows), lambda b, t: (b, 0, 0, t)),
                   pl.BlockSpec((1, n_heads, rows, 2 * HEAD_DIM), lambda b, t: (b, 0, t, 0)),
                   pl.BlockSpec((1, n_heads, vt_rows, rows), lambda b, t: (b, 0, 0, t)),
                   seq_spec(gates_w), seq_spec(u_w)],
        out_shape=[jax.ShapeDtypeStruct((bsz, n_heads, 2 * HEAD_DIM, s), _BF),
                   jax.ShapeDtypeStruct((bsz, n_heads, s, 2 * HEAD_DIM), _BF),
                   jax.ShapeDtypeStruct((bsz, n_heads, vt_rows, s), _BF),
                   jax.ShapeDtypeStruct((bsz, s, gates_w), _BF),
                   jax.ShapeDtypeStruct((bsz, s, u_w), _BF)],
        compiler_params=pltpu.CompilerParams(
            dimension_semantics=("arbitrary", "arbitrary"), vmem_limit_bytes=VMEM_LIMIT_BYTES),
        name="norm_inproj",
    )(x, mod3, g_pre.reshape(1, d), w_in_bf, wvt_bf, pos_rows, freq, spread)


def _attn_kernel(tab_ref, lam_ref, gsub_ref, qt_ref, k_ref, vt_ref, o_ref, *scratch, lam_init):
    s_ref = (scratch[0:2], scratch[2:4])
    e_ref = (scratch[4:6], scratch[6:8])
    m_ref = (scratch[8:10], scratch[10:12])
    ev_ref = (scratch[12:14], scratch[14:16])
    n_keys, tq = s_ref[0][0].shape
    tk = KEY_BLOCK
    n_kb = n_keys // tk
    step = pl.program_id(0)

    @pl.when(step == 0)
    def _():
        for ref in scratch:
            ref[...] = jnp.zeros_like(ref)

    lam = lam_ref[0:1, 0:1]

    def tick(new, old):
        ot = None
        for mp in range(2):
            ev = ev_ref[old][mp][...]
            nrm = ev[0:V_HEAD_DIM, :] * (1.0 / ev[V_HEAD_DIM:V_HEAD_DIM + 1, :])
            ot = nrm if ot is None else ot - lam * nrm
        inv = lax.rsqrt(jnp.mean(ot * ot, axis=0, keepdims=True) + EPS)
        o_ref[0, 0] = (((ot * inv) * gsub_ref[...]) * (1.0 - lam_init)).T.astype(_BF)
        qt = qt_ref[0, 0]
        first = lax.broadcasted_iota(jnp.int32, (2 * HEAD_DIM, 1), 0) < HEAD_DIM
        zero = jnp.zeros_like(qt)
        qmt = (jnp.where(first, qt, zero), jnp.where(first, zero, qt))
        m_rows = [jnp.max(m_ref[old][mp][...], axis=0, keepdims=True) for mp in range(2)]
        n_part = 4 if n_kb % 4 == 0 else 1
        for mp in range(2):
            ev = None
            m = jnp.full((8, tq), -jnp.inf, _F32)
            for kb in range(n_kb):
                keys = slice(kb * tk, (kb + 1) * tk)
                if kb % (n_kb // n_part) == 0:
                    part_keys = slice(kb * tk, (kb + n_kb // n_part) * tk)
                    part = _dot(vt_ref[0, 0, :, part_keys], e_ref[new][mp][part_keys, :])
                    ev = part if ev is None else ev + part
                e_ref[old][mp][keys, :] = jnp.exp2(s_ref[old][mp][keys, :] - m_rows[mp]).astype(_BF)
                st = _dot(k_ref[0, 0, keys, :], qmt[mp])
                s_ref[new][mp][keys, :] = st
                m = jnp.maximum(m, jnp.max(st.reshape(tk // 8, 8, tq), axis=0))
            m_ref[new][mp][...] = m
            ev_ref[new][mp][...] = ev

    @pl.when(step % 2 == 0)
    def _():
        tick(0, 1)

    @pl.when(step % 2 == 1)
    def _():
        tick(1, 0)


def _attention(lam, g_subln, qt, k, vt, *, tq, lam_init):
    bsz, n_heads, hw, s = qt.shape
    _, _, vt_rows, _ = vt.shape
    n_q = s // tq
    n_items = bsz * n_heads * n_q
    kern = functools.partial(_attn_kernel, lam_init=lam_init)
    n_steps = n_items + 3

    def item(j):
        j = np.clip(j, 0, n_items - 1)
        return j // (n_heads * n_q), (j // n_q) % n_heads, j % n_q

    g = np.arange(n_steps)
    tab = jnp.asarray(np.stack([*item(g), *item(g - 2), *item(g - 3)]).astype(np.int32))

    return pl.pallas_call(
        kern,
        grid_spec=pltpu.PrefetchScalarGridSpec(
            num_scalar_prefetch=1,
            grid=(n_steps,),
            in_specs=[pl.BlockSpec(lam.shape, lambda g, t: (0, 0)),
                      pl.BlockSpec((V_HEAD_DIM, 1), lambda g, t: (0, 0)),
                      pl.BlockSpec((1, 1, hw, tq), lambda g, t: (t[0, g], t[1, g], 0, t[2, g])),
                      pl.BlockSpec((1, 1, s, hw), lambda g, t: (t[0, g], t[1, g], 0, 0)),
                      pl.BlockSpec((1, 1, vt_rows, s), lambda g, t: (t[3, g], t[4, g], 0, 0))],
            out_specs=pl.BlockSpec((1, 1, tq, V_HEAD_DIM), lambda g, t: (t[6, g], t[7, g], t[8, g], 0)),
            scratch_shapes=([pltpu.VMEM((s, tq), _F32)] * 4 + [pltpu.VMEM((s, tq), _BF)] * 4
                            + [pltpu.VMEM((8, tq), _F32)] * 4 + [pltpu.VMEM((vt_rows, tq), _F32)] * 4)),
        out_shape=jax.ShapeDtypeStruct((bsz, n_heads, s, V_HEAD_DIM), _BF),
        compiler_params=pltpu.CompilerParams(
            dimension_semantics=("arbitrary",), vmem_limit_bytes=VMEM_LIMIT_BYTES),
        name="diff_attention",
    )(tab, lam, g_subln.reshape(V_HEAD_DIM, 1), qt, k, vt)


def _ssm_kernel(ut_ref, lam_ref, bt_ref, ct_ref, dsk_ref, yt_ref, at_ref, z_ref, xp_ref,
                *, bsz, n_c):
    el = SSM_CHUNK
    cw = el * SSM_GROUP
    sw = 2 * SSM_STATE
    n_cols = n_c * bsz
    in_group = [lax.broadcasted_iota(jnp.int32, (1, sw), 1) // SSM_STATE == gl for gl in range(2)]

    def group_rows(ref, gl):
        return ref[:, gl * SSM_GROUP:(gl + 1) * SSM_GROUP, :].reshape(cw, n_cols)

    def per_row(rows, cols):
        return (jnp.broadcast_to(rows[:, None, :], (el, SSM_GROUP, sw)),
                jnp.broadcast_to(cols[None, :, :], (el, SSM_GROUP, sw)))

    def cmul_rows(p_r, p_i, m_r, m_i):
        pr3, mr3 = per_row(p_r, m_r)
        pi3, mi3 = per_row(p_i, m_i)
        return (pr3 * mr3 - pi3 * mi3).reshape(cw, sw), (pr3 * mi3 + pi3 * mr3).reshape(cw, sw)

    idx = lax.broadcasted_iota(jnp.int32, (el, 1), 0).astype(_F32)
    ops = []
    for e in range(2):
        lr = jnp.minimum(lam_ref[0, e, 0:1, :], -1e-4)
        li = lam_ref[0, e, 1:2, :]
        dt = jnp.exp(lam_ref[0, e, 2:3, :])

        def power(k, lr=lr, li=li, dt=dt):
            mag = jnp.exp(k * (lr * dt))
            return mag * jnp.cos(k * (li * dt)), mag * jnp.sin(k * (li * dt))

        one_r, one_i = power(1.0)
        den = lr * lr + li * li
        nr, ni = one_r - 1.0, one_i
        coef_r = (nr * lr + ni * li) / den
        coef_i = (ni * lr - nr * li) / den
        b_r, b_i = bt_ref[0, e, 0], bt_ref[0, e, 1]
        bb_r = coef_r * b_r - coef_i * b_i
        bb_i = coef_r * b_i + coef_i * b_r
        c_r, c_i = ct_ref[0, e, 0], ct_ref[0, e, 1]
        ops.append(dict(power=power, bb=(bb_r, bb_i), c=(c_r, c_i)))
    fwd, bwd = ops

    zf_r, zf_i = cmul_rows(*fwd["power"](el - 1.0 - idx), *fwd["bb"])
    zb_r, zb_i = cmul_rows(*bwd["power"](idx), *bwd["bb"])
    yf_r, yf_i = cmul_rows(*fwd["power"](idx + 1.0), *fwd["c"])
    yb_r, yb_i = cmul_rows(*bwd["power"](el - idx), *bwd["c"])
    kf_r, kf_i = cmul_rows(*fwd["power"](idx), *fwd["c"])
    kb_r, kb_i = cmul_rows(*bwd["power"](idx), *bwd["c"])

    def pair_slot(parts, gl):
        return jnp.concatenate([jnp.where(in_group[gl], p, 0.0) for p in parts], axis=1).astype(_BF)

    for gl in range(2):
        at_ref[gl] = group_rows(ut_ref, gl).T
    z = (_dot(at_ref[0], pair_slot([zf_r, zf_i, zb_r, zb_i], 0))
         + _dot(at_ref[1], pair_slot([zf_r, zf_i, zb_r, zb_i], 1)))
    for comp in range(4):
        z_ref[comp] = z[:, comp * sw:(comp + 1) * sw]

    a_fr, a_fi = fwd["power"](float(el))
    a_br, a_bi = bwd["power"](float(el))

    def step(c, carry):
        fr, fi, br, bi = carry
        rows_f = pl.ds(pl.multiple_of(c * bsz, bsz), bsz)
        rows_b = pl.ds(pl.multiple_of((n_c - 1 - c) * bsz, bsz), bsz)
        xp_ref[0, rows_f, :] = fr
        xp_ref[1, rows_f, :] = fi
        xp_ref[2, rows_b, :] = br
        xp_ref[3, rows_b, :] = bi
        zfr = z_ref[0, rows_f, :]
        zfi = z_ref[1, rows_f, :]
        zbr = z_ref[2, rows_b, :]
        zbi = z_ref[3, rows_b, :]
        return (a_fr * fr - a_fi * fi + zfr, a_fr * fi + a_fi * fr + zfi,
                a_br * br - a_bi * bi + zbr, a_br * bi + a_bi * br + zbi)

    zero = jnp.zeros((bsz, sw), _F32)
    lax.fori_loop(0, n_c, step, (zero, zero, zero, zero), unroll=4)
    xp = jnp.concatenate([xp_ref[comp] for comp in range(4)], axis=1).astype(_BF)

    spread = (lax.broadcasted_iota(jnp.int32, (SSM_GROUP, cw), 1) % SSM_GROUP
              == lax.broadcasted_iota(jnp.int32, (SSM_GROUP, cw), 0)).astype(_BF)
    per_tile = LANES // SSM_GROUP
    t_in_tile = lax.broadcasted_iota(jnp.int32, (1, LANES), 1) // SSM_GROUP

    def lag_kernels(k_r, k_i, bb, gl):
        lhs = jnp.concatenate([jnp.where(in_group[gl], k_r, 0.0),
                               jnp.where(in_group[gl], -k_i, 0.0)], axis=1).astype(_BF)
        rhs = jnp.concatenate([bb[0], bb[1]], axis=1).astype(_BF)
        return _nt_dot(lhs, rhs)

    for gl in range(2):
        ktf = _dot(lag_kernels(kf_r, kf_i, fwd["bb"], gl).astype(_BF), spread)
        ktb = _dot(lag_kernels(kb_r, kb_i, bwd["bb"], gl).astype(_BF), spread)

        def column_block(tile, t):
            lanes = slice(tile * LANES, (tile + 1) * LANES)

            def blk(kt, k):
                return kt[k * SSM_GROUP:(k + 1) * SSM_GROUP, lanes]

            rows = [blk(ktb, t - s_) if s_ < t else blk(ktf, s_ - t) if s_ > t
                    else blk(ktf, 0) + blk(ktb, 0) for s_ in range(el)]
            return jnp.concatenate(rows, axis=0)

        tiles = []
        for tile in range(cw // LANES):
            acc = column_block(tile, tile * per_tile)
            for j in range(1, per_tile):
                acc = jnp.where(t_in_tile == j, column_block(tile, tile * per_tile + j), acc)
            tiles.append(acc)
        tm = jnp.concatenate(tiles, axis=1)
        u_g = group_rows(ut_ref, gl)
        yt = _dot(tm.astype(_BF), u_g) + _nt_dot(pair_slot([yf_r, -yf_i, yb_r, -yb_i], gl), xp)
        d_rows = jnp.broadcast_to(dsk_ref[0, gl][None], (el, SSM_GROUP, 1)).reshape(cw, 1)
        yt = yt + d_rows * u_g.astype(_F32)
        yt_ref[:, gl * SSM_GROUP:(gl + 1) * SSM_GROUP, :] = yt.reshape(
            el, SSM_GROUP, n_cols).astype(_BF)


def _ssm(ut, lam, bt, ct, dsk, *, bsz):
    _, u_w, n_cols = ut.shape
    n_c = n_cols // bsz
    n_pairs = u_w // (2 * SSM_GROUP)
    cw = SSM_CHUNK * SSM_GROUP
    sw = 2 * SSM_STATE
    kern = functools.partial(_ssm_kernel, bsz=bsz, n_c=n_c)
    blk = pl.BlockSpec((SSM_CHUNK, 2 * SSM_GROUP, n_cols), lambda g: (0, g, 0))
    par = pl.BlockSpec((1, 2, 2, SSM_GROUP, sw), lambda g: (g, 0, 0, 0, 0))
    return pl.pallas_call(
        kern,
        grid=(n_pairs,),
        in_specs=[blk,
                  pl.BlockSpec((1, 2, 3, sw), lambda g: (g, 0, 0, 0)),
                  par, par,
                  pl.BlockSpec((1, 2, SSM_GROUP, 1), lambda g: (g, 0, 0, 0))],
        out_specs=blk,
        out_shape=jax.ShapeDtypeStruct(ut.shape, _BF),
        scratch_shapes=[pltpu.VMEM((2, n_cols, cw), _BF),
                        pltpu.VMEM((4, n_cols, sw), _F32),
                        pltpu.VMEM((4, n_cols, sw), _F32)],
        compiler_params=pltpu.CompilerParams(
            dimension_semantics=("arbitrary",), vmem_limit_bytes=VMEM_LIMIT_BYTES),
        name="s5_chunked",
    )(ut, lam, bt, ct, dsk)


def _post_kernel(o_ref, gates_ref, y_ref, x_ref, mod_ref, bglu_ref, gfin_ref,
                 wua_ref, wglu_ref, wus_ref, wout_ref, out_ref, *, d, aw, sw):
    y = jax.nn.gelu(y_ref[0].astype(_F32))
    y = y * _sigmoid(_dot(y.astype(_BF), wglu_ref[...]) + bglu_ref[...])
    s_br = _dot((y * gates_ref[0, :, aw:aw + sw].astype(_F32)).astype(_BF), wus_ref[...])
    o = jnp.concatenate([o_ref[0, hd] for hd in range(o_ref.shape[1])], axis=1)
    a_br = _dot((o.astype(_F32) * gates_ref[0, :, 0:aw].astype(_F32)).astype(_BF), wua_ref[...])
    g_off = aw + sw
    merged = (gates_ref[0, :, g_off:g_off + d].astype(_F32) * a_br
              + gates_ref[0, :, g_off + d:g_off + 2 * d].astype(_F32) * s_br)
    r = _dot(merged.astype(_BF), wout_ref[...])
    xo = x_ref[0] + mod_ref[0, :, 2 * d:3 * d] * r
    inv = lax.rsqrt(jnp.mean(xo * xo, axis=-1, keepdims=True) + EPS)
    out_ref[0] = (xo * inv) * gfin_ref[...]


def _post(o, gates, y, x, mod3, b_glu, g_final, wua, wglu, wus, wout, *, rows):
    bsz, s, d = x.shape
    _, n_heads, _, vd = o.shape
    aw = n_heads * vd
    sw = y.shape[2]
    kern = functools.partial(_post_kernel, d=d, aw=aw, sw=sw)
    full = lambda a: pl.BlockSpec(a.shape, lambda b, t: (0,) * a.ndim)
    seq_spec = lambda w: pl.BlockSpec((1, rows, w), lambda b, t: (b, t, 0))
    bgl = b_glu.reshape(1, sw)
    gfi = g_final.reshape(1, d)
    return pl.pallas_call(
        kern,
        grid=(bsz, s // rows),
        in_specs=[pl.BlockSpec((1, n_heads, rows, vd), lambda b, t: (b, 0, t, 0)),
                  seq_spec(gates.shape[2]), seq_spec(sw), seq_spec(d),
                  pl.BlockSpec((1, 1, 3 * d), lambda b, t: (b, 0, 0)),
                  full(bgl), full(gfi), full(wua), full(wglu), full(wus), full(wout)],
        out_specs=seq_spec(d),
        out_shape=jax.ShapeDtypeStruct(x.shape, _F32),
        compiler_params=pltpu.CompilerParams(
            dimension_semantics=("arbitrary", "arbitrary"), vmem_limit_bytes=VMEM_LIMIT_BYTES),
        name="merge_out",
    )(o, gates, y, x, mod3, bgl, gfi, wua, wglu, wus, wout)


def _rope_constants():
    half = ROT_DIM // 2
    freq = (ROPE_THETA ** (-jnp.arange(half, dtype=_F32) * 2.0 / ROT_DIM)).reshape(half, 1)
    in_head = jnp.arange(LANES) % HEAD_DIM
    j = jnp.arange(half)[:, None]
    cos_sel = ((in_head == j) | (in_head == half + j)).astype(_F32)
    sin_sel = (in_head == half + j).astype(_F32) - (in_head == j).astype(_F32)
    zero = jnp.zeros_like(cos_sel)
    return freq, jnp.concatenate([jnp.concatenate([cos_sel, zero], axis=1),
                                  jnp.concatenate([zero, sin_sel], axis=1)], axis=0)


def _ssm_params(lam_re, lam_im, log_dt, b_re, b_im, c_re, c_im):
    n_g, n_p = lam_re.shape[1], lam_re.shape[2]
    pairs = n_g // 2

    def lanes(m):
        rows = m.shape[2]
        return m.reshape(2, pairs, 2, rows, n_p).transpose(1, 0, 3, 2, 4).reshape(pairs, 2, rows, 2 * n_p)

    lam = lanes(jnp.stack([lam_re, lam_im, jnp.broadcast_to(log_dt[..., None], lam_re.shape)],
                          axis=2).astype(_F32))
    bt = jnp.stack([lanes(jnp.swapaxes(b_re, -1, -2)), lanes(jnp.swapaxes(b_im, -1, -2))], axis=2)
    ct = jnp.stack([lanes(c_re), lanes(c_im)], axis=2)
    return lam, bt.astype(_F32), ct.astype(_F32)


def _layer(x, c, positions, layer_idx, w_ada, b_ada, g_pre, w_in, lam_qk, g_subln,
           lam_re, lam_im, log_dt, b_re, b_im, c_re, c_im, d_skip,
           w_glu, b_glu, w_up_attn, w_up_ssm, w_out, g_final):
    bsz, s, d = x.shape
    n_c = s // SSM_CHUNK
    qk_w = ATTN_HEADS * 2 * HEAD_DIM
    v_w = ATTN_HEADS * V_HEAD_DIM
    u_w = lam_re.shape[1] * SSM_GROUP
    u_off = 2 * qk_w + 2 * v_w
    lam_init = 0.8 - 0.6 * math.exp(-0.3 * layer_idx)
    mod, lam, w_in_bf, wvt_bf = _prepare(c, w_ada, b_ada, lam_qk, w_in, lam_init=lam_init,
                                         v_off=2 * qk_w, v_w=v_w)
    mod3 = mod.reshape(bsz, 1, 3 * d)
    rows = min(ROW_BLOCK, s)
    pos_rows = positions.astype(_F32).reshape(bsz * (s // rows), 1, rows)
    qt, k, vt, gates, u = _inproj(x, mod3, g_pre, w_in_bf, wvt_bf, pos_rows, rows=rows,
                                  qk_w=qk_w, v_w=v_w, u_off=u_off, u_w=u_w)
    o = _attention(lam, g_subln, qt, k, vt, tq=min(Q_ROWS, s), lam_init=lam_init)
    ut = u.reshape(bsz, n_c, SSM_CHUNK, u_w).transpose(2, 3, 1, 0).reshape(SSM_CHUNK, u_w, n_c * bsz)
    dsk = d_skip.astype(_F32).reshape(u_w // (2 * SSM_GROUP), 2, SSM_GROUP, 1)
    yt = _ssm(ut, *_ssm_params(lam_re, lam_im, log_dt, b_re, b_im, c_re, c_im), dsk, bsz=bsz)
    y = yt.reshape(SSM_CHUNK, u_w, n_c, bsz).transpose(3, 2, 0, 1).reshape(bsz, s, u_w)
    return _post(o, gates, y, x, mod3, b_glu, g_final,
                 w_up_attn.astype(_BF), w_glu.astype(_BF), w_up_ssm.astype(_BF), w_out.astype(_BF),
                 rows=rows)


def kernel(x, c, positions, w_ada, b_ada, g_pre, w_in, lam_qk, g_subln, ssm_lam_re, ssm_lam_im,
           ssm_log_dt, ssm_b_re, ssm_b_im, ssm_c_re, ssm_c_im, ssm_d, w_glu, b_glu, w_up_attn,
           w_up_ssm, w_out, g_final):
    depth = w_ada.shape[0]
    assert depth == 1, "the final RMSNorm is fused into the single layer's epilogue"
    assert x.shape[1] % (SSM_CHUNK * 8) == 0
    return _layer(x, c, positions, 0, w_ada[0], b_ada[0], g_pre[0], w_in[0],
                  lam_qk[0], g_subln[0], ssm_lam_re[0], ssm_lam_im[0], ssm_log_dt[0], ssm_b_re[0],
                  ssm_b_im[0], ssm_c_re[0], ssm_c_im[0], ssm_d[0], w_glu[0], b_glu[0],
                  w_up_attn[0], w_up_ssm[0], w_out[0], g_final)
```

```python
import functools
import math

import jax
import jax.numpy as jnp
import numpy as np
from jax import lax
from jax.experimental import pallas as pl
from jax.experimental.pallas import tpu as pltpu

ATTN_HEADS = 4
HEAD_DIM = 64
V_HEAD_DIM = 2 * HEAD_DIM
ROT_DIM = HEAD_DIM // 4
ROPE_THETA = 500000.0
SSM_GROUP = 16
SSM_STATE = 64
SSM_CHUNK = 16
EPS = 1e-6
LOG2_E = math.log2(math.e)
LANES = 128
VMEM_LIMIT_BYTES = 56 * 1024 * 1024
ROW_BLOCK = 1024
Q_ROWS = 1024
KEY_BLOCK = 256
Q_BLOCK = 512
VT_PAD = 16

_HI = lax.Precision.HIGHEST
_BF = jnp.bfloat16
_F32 = jnp.float32


def _nt_dot(a, b):
    return lax.dot_general(a, b, (((1,), (1,)), ((), ())), preferred_element_type=_F32)


def _dot(a, b):
    return jnp.dot(a, b, preferred_element_type=_F32)


def _sigmoid(x):
    return 0.5 * jnp.tanh(0.5 * x) + 0.5


def _prep_kernel(c_ref, w_ref, b_ref, lamqk_ref, win_ref, o_ref, lam_ref, winb_ref, wvt_ref,
                 *, lam_init, n_mod, v_tile):
    j = pl.program_id(0)

    @pl.when(j < n_mod)
    def _():
        o_ref[...] = jnp.dot(jax.nn.silu(c_ref[...]), w_ref[...], precision=_HI,
                             preferred_element_type=_F32) + b_ref[...]

    @pl.when(j == 0)
    def _():
        lf = lamqk_ref[...]
        lam = (jnp.exp(jnp.sum(lf[0:1, :] * lf[1:2, :], axis=-1, keepdims=True))
               - jnp.exp(jnp.sum(lf[2:3, :] * lf[3:4, :], axis=-1, keepdims=True)) + lam_init)
        lam_ref[...] = jnp.broadcast_to(lam, lam_ref.shape)

    w = win_ref[...]
    winb_ref[...] = w.astype(_BF)

    @pl.when(j == v_tile)
    def _():
        wvt_ref[...] = w.T.astype(_BF)


def _prepare(c, w_ada, b_ada, lam_qk, w_in, *, lam_init, v_off, v_w):
    bsz, d = c.shape
    n = w_ada.shape[1]
    n_in = w_in.shape[1]
    tn = d
    n_mod = n // tn
    assert v_off % v_w == 0 and n_in % v_w == 0 and n_in // v_w >= n_mod
    last = n_mod - 1
    return pl.pallas_call(
        functools.partial(_prep_kernel, lam_init=lam_init, n_mod=n_mod, v_tile=v_off // v_w),
        grid=(n_in // v_w,),
        in_specs=[pl.BlockSpec((bsz, d), lambda j: (0, 0)),
                  pl.BlockSpec((d, tn), lambda j: (0, jnp.minimum(j, last))),
                  pl.BlockSpec((1, tn), lambda j: (0, jnp.minimum(j, last))),
                  pl.BlockSpec(lam_qk.shape, lambda j: (0, 0)),
                  pl.BlockSpec((d, v_w), lambda j: (0, j))],
        out_specs=[pl.BlockSpec((bsz, tn), lambda j: (0, jnp.minimum(j, last))),
                   pl.BlockSpec((1, LANES), lambda j: (0, 0)),
                   pl.BlockSpec((d, v_w), lambda j: (0, j)),
                   pl.BlockSpec((v_w, d), lambda j: (0, 0))],
        out_shape=[jax.ShapeDtypeStruct((bsz, n), _F32), jax.ShapeDtypeStruct((1, LANES), _F32),
                   jax.ShapeDtypeStruct((d, n_in), _BF), jax.ShapeDtypeStruct((v_w, d), _BF)],
        compiler_params=pltpu.CompilerParams(
            dimension_semantics=("arbitrary",), vmem_limit_bytes=VMEM_LIMIT_BYTES),
        name="adaln_mod",
    )(c, w_ada, b_ada.reshape(1, n), lam_qk, w_in)


def _inproj_kernel(x_ref, mod_ref, gpre_ref, w_ref, wvt_ref, pos_ref, freq_ref, spread_ref,
                   qt_ref, k_ref, vt_ref, gates_ref, u_ref, *, d, qk_w, v_w, u_off, u_w):
    shift = mod_ref[0, :, 0:d]
    scale = mod_ref[0, :, d:2 * d]
    x = x_ref[0]
    inv = lax.rsqrt(jnp.mean(x * x, axis=-1, keepdims=True) + EPS)
    h = ((x * inv) * (gpre_ref[...] * (1.0 + scale)) + shift).astype(_BF)

    rows = x.shape[0]
    half = ROT_DIM // 2
    ang = freq_ref[...] * pos_ref[0]
    cs = jnp.concatenate([jnp.cos(ang), jnp.sin(ang)], axis=0)
    pieces = []
    rest = cs
    for _ in range(3):
        piece = rest.astype(_BF).astype(_F32)
        rest = rest - piece
        pieces.append(piece)
    sel = spread_ref[...].astype(_BF)
    tab = _dot(jnp.concatenate(pieces, axis=0).T.astype(_BF), jnp.concatenate([sel] * 3, axis=0))
    lane = lax.broadcasted_iota(jnp.int32, (1, LANES), 1)
    reps = qk_w // LANES
    rot_c = jnp.concatenate([jnp.where(lane % HEAD_DIM < ROT_DIM, tab[:, 0:LANES], 1.0)] * reps,
                            axis=1)
    rot_s = jnp.concatenate([tab[:, LANES:2 * LANES]] * reps, axis=1)
    low = lax.broadcasted_iota(jnp.int32, (1, qk_w), 1) % HEAD_DIM < half

    def rope(t):
        up = pltpu.roll(t, qk_w - half, 1)
        dn = pltpu.roll(t, half, 1)
        return t * rot_c + jnp.where(low, up, dn) * rot_s

    v_off = 2 * qk_w
    vt = _nt_dot(wvt_ref[...], h).astype(_BF)
    pad = (lax.broadcasted_iota(jnp.int32, (VT_PAD, rows), 0) == 0).astype(_BF)
    for hd in range(v_w // V_HEAD_DIM):
        vt_ref[0, hd, 0:V_HEAD_DIM, :] = vt[hd * V_HEAD_DIM:(hd + 1) * V_HEAD_DIM, :]
        vt_ref[0, hd, V_HEAD_DIM:V_HEAD_DIM + VT_PAD, :] = pad
    u_ref[0] = _dot(h, w_ref[:, u_off:u_off + u_w]).astype(_BF)
    qt = rope(_dot(h, w_ref[:, 0:qk_w]) * (HEAD_DIM ** -0.5 * LOG2_E)).T.astype(_BF)
    hw = 2 * HEAD_DIM
    for hd in range(qk_w // hw):
        qt_ref[0, hd] = qt[hd * hw:(hd + 1) * hw, :]
    kk = rope(_dot(h, w_ref[:, qk_w:2 * qk_w])).astype(_BF)
    for hd in range(qk_w // hw):
        k_ref[0, hd] = kk[:, hd * hw:(hd + 1) * hw]
    za_off = v_off + v_w
    z_a = _dot(h, w_ref[:, za_off:za_off + v_w])
    gates_ref[0, :, 0:v_w] = (z_a * _sigmoid(z_a)).astype(_BF)
    zs_off = u_off + u_w
    z_s = _dot(h, w_ref[:, zs_off:zs_off + u_w])
    gates_ref[0, :, v_w:v_w + u_w] = (z_s * _sigmoid(z_s)).astype(_BF)
    g_off = zs_off + u_w
    rest = w_ref.shape[1] - g_off
    piece = 512
    for p in range(rest // piece):
        gates_ref[0, :, v_w + u_w + p * piece:v_w + u_w + (p + 1) * piece] = _sigmoid(_dot(
            h, w_ref[:, g_off + p * piece:g_off + (p + 1) * piece])).astype(_BF)


def _inproj(x, mod3, g_pre, w_in_bf, wvt_bf, pos_rows, *, rows, qk_w, v_w, u_off, u_w):
    bsz, s, d = x.shape
    n_in = w_in_bf.shape[1]
    gates_w = n_in - 2 * qk_w - v_w - u_w
    n_heads = v_w // V_HEAD_DIM
    vt_rows = V_HEAD_DIM + VT_PAD
    kern = functools.partial(_inproj_kernel, d=d, qk_w=qk_w, v_w=v_w, u_off=u_off, u_w=u_w)
    freq, spread = _rope_constants()
    seq_spec = lambda w: pl.BlockSpec((1, rows, w), lambda b, t: (b, t, 0))
    return pl.pallas_call(
        kern,
        grid=(bsz, s // rows),
        in_specs=[seq_spec(d),
                  pl.BlockSpec((1, 1, 3 * d), lambda b, t: (b, 0, 0)),
                  pl.BlockSpec((1, d), lambda b, t: (0, 0)),
                  pl.BlockSpec((d, n_in), lambda b, t: (0, 0), pipeline_mode=pl.Buffered(1)),
                  pl.BlockSpec((v_w, d), lambda b, t: (0, 0), pipeline_mode=pl.Buffered(1)),
                  pl.BlockSpec((1, 1, rows), lambda b, t: (b * (s // rows) + t, 0, 0)),
                  pl.BlockSpec(freq.shape, lambda b, t: (0, 0)),
                  pl.BlockSpec(spread.shape, lambda b, t: (0, 0))],
        out_specs=[pl.BlockSpec((1, n_heads, 2 * HEAD_DIM, rows), lambda b, t: (b, 0, 0, t)),
                   pl.BlockSpec((1, n_heads, rows, 2 * HEAD_DIM), lambda b, t: (b, 0, t, 0)),
                   pl.BlockSpec((1, n_heads, vt_rows, rows), lambda b, t: (b, 0, 0, t)),
                   seq_spec(gates_w), seq_spec(u_w)],
        out_shape=[jax.ShapeDtypeStruct((bsz, n_heads, 2 * HEAD_DIM, s), _BF),
                   jax.ShapeDtypeStruct((bsz, n_heads, s, 2 * HEAD_DIM), _BF),
                   jax.ShapeDtypeStruct((bsz, n_heads, vt_rows, s), _BF),
                   jax.ShapeDtypeStruct((bsz, s, gates_w), _BF),
                   jax.ShapeDtypeStruct((bsz, s, u_w), _BF)],
        compiler_params=pltpu.CompilerParams(
            dimension_semantics=("arbitrary", "arbitrary"), vmem_limit_bytes=VMEM_LIMIT_BYTES),
        name="norm_inproj",
    )(x, mod3, g_pre.reshape(1, d), w_in_bf, wvt_bf, pos_rows, freq, spread)


def _attn_kernel(tab_ref, lam_ref, gsub_ref, qt_ref, k_ref, vt_ref, o_ref, *scratch, lam_init):
    s_ref = (scratch[0:2], scratch[2:4])
    e_ref = (scratch[4:6], scratch[6:8])
    m_ref = (scratch[8:10], scratch[10:12])
    ev_ref = (scratch[12:14], scratch[14:16])
    n_keys, tq = s_ref[0][0].shape
    tk = min(KEY_BLOCK, n_keys)
    tqs = min(Q_BLOCK, tq)
    n_kb = n_keys // tk
    step = pl.program_id(0)

    @pl.when(step == 0)
    def _():
        for ref in scratch:
            ref[...] = jnp.zeros_like(ref)

    lam = lam_ref[0:1, 0:1]

    def tick(new, old):
        ot = None
        for mp in range(2):
            ev = ev_ref[old][mp][...]
            nrm = ev[0:V_HEAD_DIM, :] * (1.0 / ev[V_HEAD_DIM:V_HEAD_DIM + 1, :])
            ot = nrm if ot is None else ot - lam * nrm
        inv = lax.rsqrt(jnp.mean(ot * ot, axis=0, keepdims=True) + EPS)
        o_ref[0, 0] = (((ot * inv) * gsub_ref[...]) * (1.0 - lam_init)).T.astype(_BF)
        qt = qt_ref[0, 0]
        first = lax.broadcasted_iota(jnp.int32, (2 * HEAD_DIM, 1), 0) < HEAD_DIM
        zero = jnp.zeros_like(qt)
        qmt = (jnp.where(first, qt, zero), jnp.where(first, zero, qt))
        m_rows = [jnp.max(m_ref[old][mp][...], axis=0, keepdims=True) for mp in range(2)]
        n_part = 4 if n_kb % 4 == 0 else 1
        for mp in range(2):
            ev = None
            m = [jnp.full((8, tqs), -jnp.inf, _F32) for _ in range(tq // tqs)]
            for kb in range(n_kb):
                keys = slice(kb * tk, (kb + 1) * tk)
                if kb % (n_kb // n_part) == 0:
                    part_keys = slice(kb * tk, (kb + n_kb // n_part) * tk)
                    part = _dot(vt_ref[0, 0, :, part_keys], e_ref[new][mp][part_keys, :])
                    ev = part if ev is None else ev + part
                for qb in range(tq // tqs):
                    qs = slice(qb * tqs, (qb + 1) * tqs)
                    e_ref[old][mp][keys, qs] = jnp.exp2(
                        s_ref[old][mp][keys, qs] - m_rows[mp][:, qs]).astype(_BF)
                    st = _dot(k_ref[0, 0, keys, :], qmt[mp][:, qs])
                    s_ref[new][mp][keys, qs] = st
                    m[qb] = jnp.maximum(m[qb], jnp.max(st.reshape(tk // 8, 8, tqs), axis=0))
            m_ref[new][mp][...] = jnp.concatenate(m, axis=1)
            ev_ref[new][mp][...] = ev

    @pl.when(step % 2 == 0)
    def _():
        tick(0, 1)

    @pl.when(step % 2 == 1)
    def _():
        tick(1, 0)


def _attention(lam, g_subln, qt, k, vt, *, tq, lam_init):
    bsz, n_heads, hw, s = qt.shape
    _, _, vt_rows, _ = vt.shape
    n_q = s // tq
    n_items = bsz * n_heads * n_q
    kern = functools.partial(_attn_kernel, lam_init=lam_init)
    n_steps = n_items + 3

    def item(j):
        j = np.clip(j, 0, n_items - 1)
        return j // (n_heads * n_q), (j // n_q) % n_heads, j % n_q

    g = np.arange(n_steps)
    tab = jnp.asarray(np.stack([*item(g), *item(g - 2), *item(g - 3)]).astype(np.int32))

    return pl.pallas_call(
        kern,
        grid_spec=pltpu.PrefetchScalarGridSpec(
            num_scalar_prefetch=1,
            grid=(n_steps,),
            in_specs=[pl.BlockSpec(lam.shape, lambda g, t: (0, 0)),
                      pl.BlockSpec((V_HEAD_DIM, 1), lambda g, t: (0, 0)),
                      pl.BlockSpec((1, 1, hw, tq), lambda g, t: (t[0, g], t[1, g], 0, t[2, g])),
                      pl.BlockSpec((1, 1, s, hw), lambda g, t: (t[0, g], t[1, g], 0, 0)),
                      pl.BlockSpec((1, 1, vt_rows, s), lambda g, t: (t[3, g], t[4, g], 0, 0))],
            out_specs=pl.BlockSpec((1, 1, tq, V_HEAD_DIM), lambda g, t: (t[6, g], t[7, g], t[8, g], 0)),
            scratch_shapes=([pltpu.VMEM((s, tq), _F32)] * 4 + [pltpu.VMEM((s, tq), _BF)] * 4
                            + [pltpu.VMEM((8, tq), _F32)] * 4 + [pltpu.VMEM((vt_rows, tq), _F32)] * 4)),
        out_shape=jax.ShapeDtypeStruct((bsz, n_heads, s, V_HEAD_DIM), _BF),
        compiler_params=pltpu.CompilerParams(
            dimension_semantics=("arbitrary",), vmem_limit_bytes=VMEM_LIMIT_BYTES),
        name="diff_attention",
    )(tab, lam, g_subln.reshape(V_HEAD_DIM, 1), qt, k, vt)


def _ssm_kernel(ut_ref, lam_ref, bt_ref, ct_ref, dsk_ref, yt_ref, at_ref, z_ref, xp_ref,
                *, bsz, n_c):
    el = SSM_CHUNK
    cw = el * SSM_GROUP
    sw = 2 * SSM_STATE
    n_cols = n_c * bsz
    in_group = [lax.broadcasted_iota(jnp.int32, (1, sw), 1) // SSM_STATE == gl for gl in range(2)]

    def group_rows(ref, gl):
        return ref[:, gl * SSM_GROUP:(gl + 1) * SSM_GROUP, :].reshape(cw, n_cols)

    def per_row(rows, cols):
        return (jnp.broadcast_to(rows[:, None, :], (el, SSM_GROUP, sw)),
                jnp.broadcast_to(cols[None, :, :], (el, SSM_GROUP, sw)))

    def cmul_rows(p_r, p_i, m_r, m_i):
        pr3, mr3 = per_row(p_r, m_r)
        pi3, mi3 = per_row(p_i, m_i)
        return (pr3 * mr3 - pi3 * mi3).reshape(cw, sw), (pr3 * mi3 + pi3 * mr3).reshape(cw, sw)

    idx = lax.broadcasted_iota(jnp.int32, (el, 1), 0).astype(_F32)
    ops = []
    for e in range(2):
        lr = jnp.minimum(lam_ref[0, e, 0:1, :], -1e-4)
        li = lam_ref[0, e, 1:2, :]
        dt = jnp.exp(lam_ref[0, e, 2:3, :])

        def power(k, lr=lr, li=li, dt=dt):
            mag = jnp.exp(k * (lr * dt))
            return mag * jnp.cos(k * (li * dt)), mag * jnp.sin(k * (li * dt))

        one_r, one_i = power(1.0)
        den = lr * lr + li * li
        nr, ni = one_r - 1.0, one_i
        coef_r = (nr * lr + ni * li) / den
        coef_i = (ni * lr - nr * li) / den
        b_r, b_i = bt_ref[0, e, 0], bt_ref[0, e, 1]
        bb_r = coef_r * b_r - coef_i * b_i
        bb_i = coef_r * b_i + coef_i * b_r
        c_r, c_i = ct_ref[0, e, 0], ct_ref[0, e, 1]
        ops.append(dict(power=power, bb=(bb_r, bb_i), c=(c_r, c_i)))
    fwd, bwd = ops

    zf_r, zf_i = cmul_rows(*fwd["power"](el - 1.0 - idx), *fwd["bb"])
    zb_r, zb_i = cmul_rows(*bwd["power"](idx), *bwd["bb"])
    yf_r, yf_i = cmul_rows(*fwd["power"](idx + 1.0), *fwd["c"])
    yb_r, yb_i = cmul_rows(*bwd["power"](el - idx), *bwd["c"])
    kf_r, kf_i = cmul_rows(*fwd["power"](idx), *fwd["c"])
    kb_r, kb_i = cmul_rows(*bwd["power"](idx), *bwd["c"])

    def pair_slot(parts, gl):
        return jnp.concatenate([jnp.where(in_group[gl], p, 0.0) for p in parts], axis=1).astype(_BF)

    for gl in range(2):
        at_ref[gl] = group_rows(ut_ref, gl).T
    z = (_dot(at_ref[0], pair_slot([zf_r, zf_i, zb_r, zb_i], 0))
         + _dot(at_ref[1], pair_slot([zf_r, zf_i, zb_r, zb_i], 1)))
    for comp in range(4):
        z_ref[comp] = z[:, comp * sw:(comp + 1) * sw]

    a_fr, a_fi = fwd["power"](float(el))
    a_br, a_bi = bwd["power"](float(el))

    def step(c, carry):
        fr, fi, br, bi = carry
        rows_f = pl.ds(pl.multiple_of(c * bsz, bsz), bsz)
        rows_b = pl.ds(pl.multiple_of((n_c - 1 - c) * bsz, bsz), bsz)
        xp_ref[0, rows_f, :] = fr
        xp_ref[1, rows_f, :] = fi
        xp_ref[2, rows_b, :] = br
        xp_ref[3, rows_b, :] = bi
        zfr = z_ref[0, rows_f, :]
        zfi = z_ref[1, rows_f, :]
        zbr = z_ref[2, rows_b, :]
        zbi = z_ref[3, rows_b, :]
        return (a_fr * fr - a_fi * fi + zfr, a_fr * fi + a_fi * fr + zfi,
                a_br * br - a_bi * bi + zbr, a_br * bi + a_bi * br + zbi)

    zero = jnp.zeros((bsz, sw), _F32)
    lax.fori_loop(0, n_c, step, (zero, zero, zero, zero), unroll=4)
    xp = jnp.concatenate([xp_ref[comp] for comp in range(4)], axis=1).astype(_BF)

    spread = (lax.broadcasted_iota(jnp.int32, (SSM_GROUP, cw), 1) % SSM_GROUP
              == lax.broadcasted_iota(jnp.int32, (SSM_GROUP, cw), 0)).astype(_BF)
    per_tile = LANES // SSM_GROUP
    t_in_tile = lax.broadcasted_iota(jnp.int32, (1, LANES), 1) // SSM_GROUP

    def lag_kernels(k_r, k_i, bb, gl):
        lhs = jnp.concatenate([jnp.where(in_group[gl], k_r, 0.0),
                               jnp.where(in_group[gl], -k_i, 0.0)], axis=1).astype(_BF)
        rhs = jnp.concatenate([bb[0], bb[1]], axis=1).astype(_BF)
        return _nt_dot(lhs, rhs)

    for gl in range(2):
        ktf = _dot(lag_kernels(kf_r, kf_i, fwd["bb"], gl).astype(_BF), spread)
        ktb = _dot(lag_kernels(kb_r, kb_i, bwd["bb"], gl).astype(_BF), spread)

        def column_block(tile, t):
            lanes = slice(tile * LANES, (tile + 1) * LANES)

            def blk(kt, k):
                return kt[k * SSM_GROUP:(k + 1) * SSM_GROUP, lanes]

            rows = [blk(ktb, t - s_) if s_ < t else blk(ktf, s_ - t) if s_ > t
                    else blk(ktf, 0) + blk(ktb, 0) for s_ in range(el)]
            return jnp.concatenate(rows, axis=0)

        tiles = []
        for tile in range(cw // LANES):
            acc = column_block(tile, tile * per_tile)
            for j in range(1, per_tile):
                acc = jnp.where(t_in_tile == j, column_block(tile, tile * per_tile + j), acc)
            tiles.append(acc)
        tm = jnp.concatenate(tiles, axis=1)
        u_g = group_rows(ut_ref, gl)
        yt = _dot(tm.astype(_BF), u_g) + _nt_dot(pair_slot([yf_r, -yf_i, yb_r, -yb_i], gl), xp)
        d_rows = jnp.broadcast_to(dsk_ref[0, gl][None], (el, SSM_GROUP, 1)).reshape(cw, 1)
        yt = yt + d_rows * u_g.astype(_F32)
        yt_ref[:, gl * SSM_GROUP:(gl + 1) * SSM_GROUP, :] = yt.reshape(
            el, SSM_GROUP, n_cols).astype(_BF)


def _ssm(ut, lam, bt, ct, dsk, *, bsz):
    _, u_w, n_cols = ut.shape
    n_c = n_cols // bsz
    n_pairs = u_w // (2 * SSM_GROUP)
    cw = SSM_CHUNK * SSM_GROUP
    sw = 2 * SSM_STATE
    kern = functools.partial(_ssm_kernel, bsz=bsz, n_c=n_c)
    blk = pl.BlockSpec((SSM_CHUNK, 2 * SSM_GROUP, n_cols), lambda g: (0, g, 0))
    par = pl.BlockSpec((1, 2, 2, SSM_GROUP, sw), lambda g: (g, 0, 0, 0, 0))
    return pl.pallas_call(
        kern,
        grid=(n_pairs,),
        in_specs=[blk,
                  pl.BlockSpec((1, 2, 3, sw), lambda g: (g, 0, 0, 0)),
                  par, par,
                  pl.BlockSpec((1, 2, SSM_GROUP, 1), lambda g: (g, 0, 0, 0))],
        out_specs=blk,
        out_shape=jax.ShapeDtypeStruct(ut.shape, _BF),
        scratch_shapes=[pltpu.VMEM((2, n_cols, cw), _BF),
                        pltpu.VMEM((4, n_cols, sw), _F32),
                        pltpu.VMEM((4, n_cols, sw), _F32)],
        compiler_params=pltpu.CompilerParams(
            dimension_semantics=("arbitrary",), vmem_limit_bytes=VMEM_LIMIT_BYTES),
        name="s5_chunked",
    )(ut, lam, bt, ct, dsk)


def _post_kernel(o_ref, gates_ref, y_ref, x_ref, mod_ref, bglu_ref, gfin_ref,
                 wua_ref, wglu_ref, wus_ref, wout_ref, out_ref, *, d, aw, sw):
    y = jax.nn.gelu(y_ref[0].astype(_F32))
    y = y * _sigmoid(_dot(y.astype(_BF), wglu_ref[...]) + bglu_ref[...])
    s_br = _dot((y * gates_ref[0, :, aw:aw + sw].astype(_F32)).astype(_BF), wus_ref[...])
    o = jnp.concatenate([o_ref[0, hd] for hd in range(o_ref.shape[1])], axis=1)
    a_br = _dot((o.astype(_F32) * gates_ref[0, :, 0:aw].astype(_F32)).astype(_BF), wua_ref[...])
    g_off = aw + sw
    merged = (gates_ref[0, :, g_off:g_off + d].astype(_F32) * a_br
              + gates_ref[0, :, g_off + d:g_off + 2 * d].astype(_F32) * s_br)
    r = _dot(merged.astype(_BF), wout_ref[...])
    xo = x_ref[0] + mod_ref[0, :, 2 * d:3 * d] * r
    inv = lax.rsqrt(jnp.mean(xo * xo, axis=-1, keepdims=True) + EPS)
    out_ref[0] = (xo * inv) * gfin_ref[...]


def _post(o, gates, y, x, mod3, b_glu, g_final, wua, wglu, wus, wout, *, rows):
    bsz, s, d = x.shape
    _, n_heads, _, vd = o.shape
    aw = n_heads * vd
    sw = y.shape[2]
    kern = functools.partial(_post_kernel, d=d, aw=aw, sw=sw)
    full = lambda a: pl.BlockSpec(a.shape, lambda b, t: (0,) * a.ndim)
    seq_spec = lambda w: pl.BlockSpec((1, rows, w), lambda b, t: (b, t, 0))
    bgl = b_glu.reshape(1, sw)
    gfi = g_final.reshape(1, d)
    return pl.pallas_call(
        kern,
        grid=(bsz, s // rows),
        in_specs=[pl.BlockSpec((1, n_heads, rows, vd), lambda b, t: (b, 0, t, 0)),
                  seq_spec(gates.shape[2]), seq_spec(sw), seq_spec(d),
                  pl.BlockSpec((1, 1, 3 * d), lambda b, t: (b, 0, 0)),
                  full(bgl), full(gfi), full(wua), full(wglu), full(wus), full(wout)],
        out_specs=seq_spec(d),
        out_shape=jax.ShapeDtypeStruct(x.shape, _F32),
        compiler_params=pltpu.CompilerParams(
            dimension_semantics=("arbitrary", "arbitrary"), vmem_limit_bytes=VMEM_LIMIT_BYTES),
        name="merge_out",
    )(o, gates, y, x, mod3, bgl, gfi, wua, wglu, wus, wout)


def _rope_constants():
    half = ROT_DIM // 2
    freq = (ROPE_THETA ** (-jnp.arange(half, dtype=_F32) * 2.0 / ROT_DIM)).reshape(half, 1)
    in_head = jnp.arange(LANES) % HEAD_DIM
    j = jnp.arange(half)[:, None]
    cos_sel = ((in_head == j) | (in_head == half + j)).astype(_F32)
    sin_sel = (in_head == half + j).astype(_F32) - (in_head == j).astype(_F32)
    zero = jnp.zeros_like(cos_sel)
    return freq, jnp.concatenate([jnp.concatenate([cos_sel, zero], axis=1),
                                  jnp.concatenate([zero, sin_sel], axis=1)], axis=0)


def _ssm_params(lam_re, lam_im, log_dt, b_re, b_im, c_re, c_im):
    n_g, n_p = lam_re.shape[1], lam_re.shape[2]
    pairs = n_g // 2

    def lanes(m):
        rows = m.shape[2]
        return m.reshape(2, pairs, 2, rows, n_p).transpose(1, 0, 3, 2, 4).reshape(pairs, 2, rows, 2 * n_p)

    lam = lanes(jnp.stack([lam_re, lam_im, jnp.broadcast_to(log_dt[..., None], lam_re.shape)],
                          axis=2).astype(_F32))
    bt = jnp.stack([lanes(jnp.swapaxes(b_re, -1, -2)), lanes(jnp.swapaxes(b_im, -1, -2))], axis=2)
    ct = jnp.stack([lanes(c_re), lanes(c_im)], axis=2)
    return lam, bt.astype(_F32), ct.astype(_F32)


def _layer(x, c, positions, layer_idx, w_ada, b_ada, g_pre, w_in, lam_qk, g_subln,
           lam_re, lam_im, log_dt, b_re, b_im, c_re, c_im, d_skip,
           w_glu, b_glu, w_up_attn, w_up_ssm, w_out, g_final):
    bsz, s, d = x.shape
    n_c = s // SSM_CHUNK
    qk_w = ATTN_HEADS * 2 * HEAD_DIM
    v_w = ATTN_HEADS * V_HEAD_DIM
    u_w = lam_re.shape[1] * SSM_GROUP
    u_off = 2 * qk_w + 2 * v_w
    lam_init = 0.8 - 0.6 * math.exp(-0.3 * layer_idx)
    mod, lam, w_in_bf, wvt_bf = _prepare(c, w_ada, b_ada, lam_qk, w_in, lam_init=lam_init,
                                         v_off=2 * qk_w, v_w=v_w)
    mod3 = mod.reshape(bsz, 1, 3 * d)
    rows = min(ROW_BLOCK, s)
    pos_rows = positions.astype(_F32).reshape(bsz * (s // rows), 1, rows)
    qt, k, vt, gates, u = _inproj(x, mod3, g_pre, w_in_bf, wvt_bf, pos_rows, rows=rows,
                                  qk_w=qk_w, v_w=v_w, u_off=u_off, u_w=u_w)
    o = _attention(lam, g_subln, qt, k, vt, tq=min(Q_ROWS, s), lam_init=lam_init)
    ut = u.reshape(bsz, n_c, SSM_CHUNK, u_w).transpose(2, 3, 1, 0).reshape(SSM_CHUNK, u_w, n_c * bsz)
    dsk = d_skip.astype(_F32).reshape(u_w // (2 * SSM_GROUP), 2, SSM_GROUP, 1)
    yt = _ssm(ut, *_ssm_params(lam_re, lam_im, log_dt, b_re, b_im, c_re, c_im), dsk, bsz=bsz)
    y = yt.reshape(SSM_CHUNK, u_w, n_c, bsz).transpose(3, 2, 0, 1).reshape(bsz, s, u_w)
    return _post(o, gates, y, x, mod3, b_glu, g_final,
                 w_up_attn.astype(_BF), w_glu.astype(_BF), w_up_ssm.astype(_BF), w_out.astype(_BF),
                 rows=rows)


def kernel(x, c, positions, w_ada, b_ada, g_pre, w_in, lam_qk, g_subln, ssm_lam_re, ssm_lam_im,
           ssm_log_dt, ssm_b_re, ssm_b_im, ssm_c_re, ssm_c_im, ssm_d, w_glu, b_glu, w_up_attn,
           w_up_ssm, w_out, g_final):
    depth = w_ada.shape[0]
    assert depth == 1, "the final RMSNorm is fused into the single layer's epilogue"
    assert x.shape[1] % (SSM_CHUNK * 8) == 0
    return _layer(x, c, positions, 0, w_ada[0], b_ada[0], g_pre[0], w_in[0],
                  lam_qk[0], g_subln[0], ssm_lam_re[0], ssm_lam_im[0], ssm_log_dt[0], ssm_b_re[0],
                  ssm_b_im[0], ssm_c_re[0], ssm_c_im[0], ssm_d[0], w_glu[0], b_glu[0],
                  w_up_attn[0], w_up_ssm[0], w_out[0], g_final)
```

```python
import functools
import math

import jax
import jax.numpy as jnp
import numpy as np
from jax import lax
from jax.experimental import pallas as pl
from jax.experimental.pallas import tpu as pltpu

ATTN_HEADS = 4
HEAD_DIM = 64
V_HEAD_DIM = 2 * HEAD_DIM
ROT_DIM = HEAD_DIM // 4
ROPE_THETA = 500000.0
SSM_GROUP = 16
SSM_STATE = 64
SSM_CHUNK = 16
EPS = 1e-6
LOG2_E = math.log2(math.e)
LANES = 128
VMEM_LIMIT_BYTES = 56 * 1024 * 1024
ROW_BLOCK = 1024
Q_ROWS = 1024
KEY_BLOCK = 128
Q_BLOCK = 512
VT_PAD = 16

_HI = lax.Precision.HIGHEST
_BF = jnp.bfloat16
_F32 = jnp.float32


def _nt_dot(a, b):
    return lax.dot_general(a, b, (((1,), (1,)), ((), ())), preferred_element_type=_F32)


def _dot(a, b):
    return jnp.dot(a, b, preferred_element_type=_F32)


def _sigmoid(x):
    return 0.5 * jnp.tanh(0.5 * x) + 0.5


def _prep_kernel(c_ref, w_ref, b_ref, lamqk_ref, win_ref, o_ref, lam_ref, winb_ref, wvt_ref,
                 *, lam_init, n_mod, v_tile):
    j = pl.program_id(0)

    @pl.when(j < n_mod)
    def _():
        o_ref[...] = jnp.dot(jax.nn.silu(c_ref[...]), w_ref[...], precision=_HI,
                             preferred_element_type=_F32) + b_ref[...]

    @pl.when(j == 0)
    def _():
        lf = lamqk_ref[...]
        lam = (jnp.exp(jnp.sum(lf[0:1, :] * lf[1:2, :], axis=-1, keepdims=True))
               - jnp.exp(jnp.sum(lf[2:3, :] * lf[3:4, :], axis=-1, keepdims=True)) + lam_init)
        lam_ref[...] = jnp.broadcast_to(lam, lam_ref.shape)

    w = win_ref[...]
    winb_ref[...] = w.astype(_BF)

    @pl.when(j == v_tile)
    def _():
        wvt_ref[...] = w.T.astype(_BF)


def _prepare(c, w_ada, b_ada, lam_qk, w_in, *, lam_init, v_off, v_w):
    bsz, d = c.shape
    n = w_ada.shape[1]
    n_in = w_in.shape[1]
    tn = d
    n_mod = n // tn
    assert v_off % v_w == 0 and n_in % v_w == 0 and n_in // v_w >= n_mod
    last = n_mod - 1
    return pl.pallas_call(
        functools.partial(_prep_kernel, lam_init=lam_init, n_mod=n_mod, v_tile=v_off // v_w),
        grid=(n_in // v_w,),
        in_specs=[pl.BlockSpec((bsz, d), lambda j: (0, 0)),
                  pl.BlockSpec((d, tn), lambda j: (0, jnp.minimum(j, last))),
                  pl.BlockSpec((1, tn), lambda j: (0, jnp.minimum(j, last))),
                  pl.BlockSpec(lam_qk.shape, lambda j: (0, 0)),
                  pl.BlockSpec((d, v_w), lambda j: (0, j))],
        out_specs=[pl.BlockSpec((bsz, tn), lambda j: (0, jnp.minimum(j, last))),
                   pl.BlockSpec((1, LANES), lambda j: (0, 0)),
                   pl.BlockSpec((d, v_w), lambda j: (0, j)),
                   pl.BlockSpec((v_w, d), lambda j: (0, 0))],
        out_shape=[jax.ShapeDtypeStruct((bsz, n), _F32), jax.ShapeDtypeStruct((1, LANES), _F32),
                   jax.ShapeDtypeStruct((d, n_in), _BF), jax.ShapeDtypeStruct((v_w, d), _BF)],
        compiler_params=pltpu.CompilerParams(
            dimension_semantics=("arbitrary",), vmem_limit_bytes=VMEM_LIMIT_BYTES),
        name="adaln_mod",
    )(c, w_ada, b_ada.reshape(1, n), lam_qk, w_in)


def _inproj_kernel(x_ref, mod_ref, gpre_ref, w_ref, wvt_ref, pos_ref, freq_ref, spread_ref,
                   qt_ref, k_ref, vt_ref, gates_ref, u_ref, *, d, qk_w, v_w, u_off, u_w):
    shift = mod_ref[0, :, 0:d]
    scale = mod_ref[0, :, d:2 * d]
    x = x_ref[0]
    inv = lax.rsqrt(jnp.mean(x * x, axis=-1, keepdims=True) + EPS)
    h = ((x * inv) * (gpre_ref[...] * (1.0 + scale)) + shift).astype(_BF)

    rows = x.shape[0]
    half = ROT_DIM // 2
    ang = freq_ref[...] * pos_ref[0]
    cs = jnp.concatenate([jnp.cos(ang), jnp.sin(ang)], axis=0)
    pieces = []
    rest = cs
    for _ in range(3):
        piece = rest.astype(_BF).astype(_F32)
        rest = rest - piece
        pieces.append(piece)
    sel = spread_ref[...].astype(_BF)
    tab = _dot(jnp.concatenate(pieces, axis=0).T.astype(_BF), jnp.concatenate([sel] * 3, axis=0))
    lane = lax.broadcasted_iota(jnp.int32, (1, LANES), 1)
    reps = qk_w // LANES
    rot_c = jnp.concatenate([jnp.where(lane % HEAD_DIM < ROT_DIM, tab[:, 0:LANES], 1.0)] * reps,
                            axis=1)
    rot_s = jnp.concatenate([tab[:, LANES:2 * LANES]] * reps, axis=1)
    low = lax.broadcasted_iota(jnp.int32, (1, qk_w), 1) % HEAD_DIM < half

    def rope(t):
        up = pltpu.roll(t, qk_w - half, 1)
        dn = pltpu.roll(t, half, 1)
        return t * rot_c + jnp.where(low, up, dn) * rot_s

    v_off = 2 * qk_w
    vt = _nt_dot(wvt_ref[...], h).astype(_BF)
    pad = (lax.broadcasted_iota(jnp.int32, (VT_PAD, rows), 0) == 0).astype(_BF)
    for hd in range(v_w // V_HEAD_DIM):
        vt_ref[0, hd, 0:V_HEAD_DIM, :] = vt[hd * V_HEAD_DIM:(hd + 1) * V_HEAD_DIM, :]
        vt_ref[0, hd, V_HEAD_DIM:V_HEAD_DIM + VT_PAD, :] = pad
    u_ref[0] = _dot(h, w_ref[:, u_off:u_off + u_w]).astype(_BF)
    qt = rope(_dot(h, w_ref[:, 0:qk_w]) * (HEAD_DIM ** -0.5 * LOG2_E)).T.astype(_BF)
    hw = 2 * HEAD_DIM
    for hd in range(qk_w // hw):
        qt_ref[0, hd] = qt[hd * hw:(hd + 1) * hw, :]
    kk = rope(_dot(h, w_ref[:, qk_w:2 * qk_w])).astype(_BF)
    for hd in range(qk_w // hw):
        k_ref[0, hd] = kk[:, hd * hw:(hd + 1) * hw]
    za_off = v_off + v_w
    z_a = _dot(h, w_ref[:, za_off:za_off + v_w])
    gates_ref[0, :, 0:v_w] = (z_a * _sigmoid(z_a)).astype(_BF)
    zs_off = u_off + u_w
    z_s = _dot(h, w_ref[:, zs_off:zs_off + u_w])
    gates_ref[0, :, v_w:v_w + u_w] = (z_s * _sigmoid(z_s)).astype(_BF)
    g_off = zs_off + u_w
    rest = w_ref.shape[1] - g_off
    piece = 512
    for p in range(rest // piece):
        gates_ref[0, :, v_w + u_w + p * piece:v_w + u_w + (p + 1) * piece] = _sigmoid(_dot(
            h, w_ref[:, g_off + p * piece:g_off + (p + 1) * piece])).astype(_BF)


def _inproj(x, mod3, g_pre, w_in_bf, wvt_bf, pos_rows, *, rows, qk_w, v_w, u_off, u_w):
    bsz, s, d = x.shape
    n_in = w_in_bf.shape[1]
    gates_w = n_in - 2 * qk_w - v_w - u_w
    n_heads = v_w // V_HEAD_DIM
    vt_rows = V_HEAD_DIM + VT_PAD
    kern = functools.partial(_inproj_kernel, d=d, qk_w=qk_w, v_w=v_w, u_off=u_off, u_w=u_w)
    freq, spread = _rope_constants()
    seq_spec = lambda w: pl.BlockSpec((1, rows, w), lambda b, t: (b, t, 0))
    return pl.pallas_call(
        kern,
        grid=(bsz, s // rows),
        in_specs=[seq_spec(d),
                  pl.BlockSpec((1, 1, 3 * d), lambda b, t: (b, 0, 0)),
                  pl.BlockSpec((1, d), lambda b, t: (0, 0)),
                  pl.BlockSpec((d, n_in), lambda b, t: (0, 0), pipeline_mode=pl.Buffered(1)),
                  pl.BlockSpec((v_w, d), lambda b, t: (0, 0), pipeline_mode=pl.Buffered(1)),
                  pl.BlockSpec((1, 1, rows), lambda b, t: (b * (s // rows) + t, 0, 0)),
                  pl.BlockSpec(freq.shape, lambda b, t: (0, 0)),
                  pl.BlockSpec(spread.shape, lambda b, t: (0, 0))],
        out_specs=[pl.BlockSpec((1, n_heads, 2 * HEAD_DIM, rows), lambda b, t: (b, 0, 0, t)),
                   pl.BlockSpec((1, n_heads, rows, 2 * HEAD_DIM), lambda b, t: (b, 0, t, 0)),
                   pl.BlockSpec((1, n_heads, vt_rows, rows), lambda b, t: (b, 0, 0, t)),
                   seq_spec(gates_w), seq_spec(u_w)],
        out_shape=[jax.ShapeDtypeStruct((bsz, n_heads, 2 * HEAD_DIM, s), _BF),
                   jax.ShapeDtypeStruct((bsz, n_heads, s, 2 * HEAD_DIM), _BF),
                   jax.ShapeDtypeStruct((bsz, n_heads, vt_rows, s), _BF),
                   jax.ShapeDtypeStruct((bsz, s, gates_w), _BF),
                   jax.ShapeDtypeStruct((bsz, s, u_w), _BF)],
        compiler_params=pltpu.CompilerParams(
            dimension_semantics=("arbitrary", "arbitrary"), vmem_limit_bytes=VMEM_LIMIT_BYTES),
        name="norm_inproj",
    )(x, mod3, g_pre.reshape(1, d), w_in_bf, wvt_bf, pos_rows, freq, spread)


def _attn_kernel(tab_ref, lam_ref, gsub_ref, qt_ref, k_ref, vt_ref, o_ref, *scratch, lam_init):
    s_ref = (scratch[0:2], scratch[2:4])
    e_ref = (scratch[4:6], scratch[6:8])
    m_ref = (scratch[8:10], scratch[10:12])
    ev_ref = (scratch[12:14], scratch[14:16])
    n_keys, tq = s_ref[0][0].shape
    tk = KEY_BLOCK
    tqs = min(Q_BLOCK, tq)
    n_kb = n_keys // tk
    step = pl.program_id(0)

    @pl.when(step == 0)
    def _():
        for ref in scratch:
            ref[...] = jnp.zeros_like(ref)

    lam = lam_ref[0:1, 0:1]

    def tick(new, old):
        ot = None
        for mp in range(2):
            ev = ev_ref[old][mp][...]
            nrm = ev[0:V_HEAD_DIM, :] * (1.0 / ev[V_HEAD_DIM:V_HEAD_DIM + 1, :])
            ot = nrm if ot is None else ot - lam * nrm
        inv = lax.rsqrt(jnp.mean(ot * ot, axis=0, keepdims=True) + EPS)
        o_ref[0, 0] = (((ot * inv) * gsub_ref[...]) * (1.0 - lam_init)).T.astype(_BF)
        qt = qt_ref[0, 0]
        first = lax.broadcasted_iota(jnp.int32, (2 * HEAD_DIM, 1), 0) < HEAD_DIM
        zero = jnp.zeros_like(qt)
        qmt = (jnp.where(first, qt, zero), jnp.where(first, zero, qt))
        m_rows = [jnp.max(m_ref[old][mp][...], axis=0, keepdims=True) for mp in range(2)]
        n_part = 4 if n_kb % 4 == 0 else 1
        for mp in range(2):
            ev = None
            m = [jnp.full((8, tqs), -jnp.inf, _F32) for _ in range(tq // tqs)]
            for kb in range(n_kb):
                keys = slice(kb * tk, (kb + 1) * tk)
                if kb % (n_kb // n_part) == 0:
                    part_keys = slice(kb * tk, (kb + n_kb // n_part) * tk)
                    part = _dot(vt_ref[0, 0, :, part_keys], e_ref[new][mp][part_keys, :])
                    ev = part if ev is None else ev + part
                for qb in range(tq // tqs):
                    qs = slice(qb * tqs, (qb + 1) * tqs)
                    e_ref[old][mp][keys, qs] = jnp.exp2(
                        (s_ref[old][mp][keys, qs] - m_rows[mp][:, qs]).astype(_BF))
                    st = _dot(k_ref[0, 0, keys, :], qmt[mp][:, qs])
                    s_ref[new][mp][keys, qs] = st
                    m[qb] = jnp.maximum(m[qb], jnp.max(st.reshape(tk // 8, 8, tqs), axis=0))
            m_ref[new][mp][...] = jnp.concatenate(m, axis=1)
            ev_ref[new][mp][...] = ev

    @pl.when(step % 2 == 0)
    def _():
        tick(0, 1)

    @pl.when(step % 2 == 1)
    def _():
        tick(1, 0)


def _attention(lam, g_subln, qt, k, vt, *, tq, lam_init):
    bsz, n_heads, hw, s = qt.shape
    _, _, vt_rows, _ = vt.shape
    n_q = s // tq
    n_items = bsz * n_heads * n_q
    kern = functools.partial(_attn_kernel, lam_init=lam_init)
    n_steps = n_items + 3

    def item(j):
        j = np.clip(j, 0, n_items - 1)
        return j // (n_heads * n_q), (j // n_q) % n_heads, j % n_q

    g = np.arange(n_steps)
    tab = jnp.asarray(np.stack([*item(g), *item(g - 2), *item(g - 3)]).astype(np.int32))

    return pl.pallas_call(
        kern,
        grid_spec=pltpu.PrefetchScalarGridSpec(
            num_scalar_prefetch=1,
            grid=(n_steps,),
            in_specs=[pl.BlockSpec(lam.shape, lambda g, t: (0, 0)),
                      pl.BlockSpec((V_HEAD_DIM, 1), lambda g, t: (0, 0)),
                      pl.BlockSpec((1, 1, hw, tq), lambda g, t: (t[0, g], t[1, g], 0, t[2, g])),
                      pl.BlockSpec((1, 1, s, hw), lambda g, t: (t[0, g], t[1, g], 0, 0)),
                      pl.BlockSpec((1, 1, vt_rows, s), lambda g, t: (t[3, g], t[4, g], 0, 0))],
            out_specs=pl.BlockSpec((1, 1, tq, V_HEAD_DIM), lambda g, t: (t[6, g], t[7, g], t[8, g], 0)),
            scratch_shapes=([pltpu.VMEM((s, tq), _F32)] * 4 + [pltpu.VMEM((s, tq), _BF)] * 4
                            + [pltpu.VMEM((8, tq), _F32)] * 4 + [pltpu.VMEM((vt_rows, tq), _F32)] * 4)),
        out_shape=jax.ShapeDtypeStruct((bsz, n_heads, s, V_HEAD_DIM), _BF),
        compiler_params=pltpu.CompilerParams(
            dimension_semantics=("arbitrary",), vmem_limit_bytes=VMEM_LIMIT_BYTES),
        name="diff_attention",
    )(tab, lam, g_subln.reshape(V_HEAD_DIM, 1), qt, k, vt)


def _ssm_kernel(ut_ref, lam_ref, bt_ref, ct_ref, dsk_ref, yt_ref, at_ref, z_ref, xp_ref,
                *, bsz, n_c):
    el = SSM_CHUNK
    cw = el * SSM_GROUP
    sw = 2 * SSM_STATE
    n_cols = n_c * bsz
    in_group = [lax.broadcasted_iota(jnp.int32, (1, sw), 1) // SSM_STATE == gl for gl in range(2)]

    def group_rows(ref, gl):
        return ref[:, gl * SSM_GROUP:(gl + 1) * SSM_GROUP, :].reshape(cw, n_cols)

    def per_row(rows, cols):
        return (jnp.broadcast_to(rows[:, None, :], (el, SSM_GROUP, sw)),
                jnp.broadcast_to(cols[None, :, :], (el, SSM_GROUP, sw)))

    def cmul_rows(p_r, p_i, m_r, m_i):
        pr3, mr3 = per_row(p_r, m_r)
        pi3, mi3 = per_row(p_i, m_i)
        return (pr3 * mr3 - pi3 * mi3).reshape(cw, sw), (pr3 * mi3 + pi3 * mr3).reshape(cw, sw)

    idx = lax.broadcasted_iota(jnp.int32, (el, 1), 0).astype(_F32)
    ops = []
    for e in range(2):
        lr = jnp.minimum(lam_ref[0, e, 0:1, :], -1e-4)
        li = lam_ref[0, e, 1:2, :]
        dt = jnp.exp(lam_ref[0, e, 2:3, :])

        def power(k, lr=lr, li=li, dt=dt):
            mag = jnp.exp(k * (lr * dt))
            return mag * jnp.cos(k * (li * dt)), mag * jnp.sin(k * (li * dt))

        one_r, one_i = power(1.0)
        den = lr * lr + li * li
        nr, ni = one_r - 1.0, one_i
        coef_r = (nr * lr + ni * li) / den
        coef_i = (ni * lr - nr * li) / den
        b_r, b_i = bt_ref[0, e, 0], bt_ref[0, e, 1]
        bb_r = coef_r * b_r - coef_i * b_i
        bb_i = coef_r * b_i + coef_i * b_r
        c_r, c_i = ct_ref[0, e, 0], ct_ref[0, e, 1]
        ops.append(dict(power=power, bb=(bb_r, bb_i), c=(c_r, c_i)))
    fwd, bwd = ops

    zf_r, zf_i = cmul_rows(*fwd["power"](el - 1.0 - idx), *fwd["bb"])
    zb_r, zb_i = cmul_rows(*bwd["power"](idx), *bwd["bb"])
    yf_r, yf_i = cmul_rows(*fwd["power"](idx + 1.0), *fwd["c"])
    yb_r, yb_i = cmul_rows(*bwd["power"](el - idx), *bwd["c"])
    kf_r, kf_i = cmul_rows(*fwd["power"](idx), *fwd["c"])
    kb_r, kb_i = cmul_rows(*bwd["power"](idx), *bwd["c"])

    def pair_slot(parts, gl):
        return jnp.concatenate([jnp.where(in_group[gl], p, 0.0) for p in parts], axis=1).astype(_BF)

    for gl in range(2):
        at_ref[gl] = group_rows(ut_ref, gl).T
    z = (_dot(at_ref[0], pair_slot([zf_r, zf_i, zb_r, zb_i], 0))
         + _dot(at_ref[1], pair_slot([zf_r, zf_i, zb_r, zb_i], 1)))
    for comp in range(4):
        z_ref[comp] = z[:, comp * sw:(comp + 1) * sw]

    a_fr, a_fi = fwd["power"](float(el))
    a_br, a_bi = bwd["power"](float(el))

    def step(c, carry):
        fr, fi, br, bi = carry
        rows_f = pl.ds(pl.multiple_of(c * bsz, bsz), bsz)
        rows_b = pl.ds(pl.multiple_of((n_c - 1 - c) * bsz, bsz), bsz)
        xp_ref[0, rows_f, :] = fr
        xp_ref[1, rows_f, :] = fi
        xp_ref[2, rows_b, :] = br
        xp_ref[3, rows_b, :] = bi
        zfr = z_ref[0, rows_f, :]
        zfi = z_ref[1, rows_f, :]
        zbr = z_ref[2, rows_b, :]
        zbi = z_ref[3, rows_b, :]
        return (a_fr * fr - a_fi * fi + zfr, a_fr * fi + a_fi * fr + zfi,
                a_br * br - a_bi * bi + zbr, a_br * bi + a_bi * br + zbi)

    zero = jnp.zeros((bsz, sw), _F32)
    lax.fori_loop(0, n_c, step, (zero, zero, zero, zero), unroll=4)
    xp = jnp.concatenate([xp_ref[comp] for comp in range(4)], axis=1).astype(_BF)

    spread = (lax.broadcasted_iota(jnp.int32, (SSM_GROUP, cw), 1) % SSM_GROUP
              == lax.broadcasted_iota(jnp.int32, (SSM_GROUP, cw), 0)).astype(_BF)
    per_tile = LANES // SSM_GROUP
    t_in_tile = lax.broadcasted_iota(jnp.int32, (1, LANES), 1) // SSM_GROUP

    def lag_kernels(k_r, k_i, bb, gl):
        lhs = jnp.concatenate([jnp.where(in_group[gl], k_r, 0.0),
                               jnp.where(in_group[gl], -k_i, 0.0)], axis=1).astype(_BF)
        rhs = jnp.concatenate([bb[0], bb[1]], axis=1).astype(_BF)
        return _nt_dot(lhs, rhs)

    for gl in range(2):
        ktf = _dot(lag_kernels(kf_r, kf_i, fwd["bb"], gl).astype(_BF), spread)
        ktb = _dot(lag_kernels(kb_r, kb_i, bwd["bb"], gl).astype(_BF), spread)

        def column_block(tile, t):
            lanes = slice(tile * LANES, (tile + 1) * LANES)

            def blk(kt, k):
                return kt[k * SSM_GROUP:(k + 1) * SSM_GROUP, lanes]

            rows = [blk(ktb, t - s_) if s_ < t else blk(ktf, s_ - t) if s_ > t
                    else blk(ktf, 0) + blk(ktb, 0) for s_ in range(el)]
            return jnp.concatenate(rows, axis=0)

        tiles = []
        for tile in range(cw // LANES):
            acc = column_block(tile, tile * per_tile)
            for j in range(1, per_tile):
                acc = jnp.where(t_in_tile == j, column_block(tile, tile * per_tile + j), acc)
            tiles.append(acc)
        tm = jnp.concatenate(tiles, axis=1)
        u_g = group_rows(ut_ref, gl)
        yt = _dot(tm.astype(_BF), u_g) + _nt_dot(pair_slot([yf_r, -yf_i, yb_r, -yb_i], gl), xp)
        d_rows = jnp.broadcast_to(dsk_ref[0, gl][None], (el, SSM_GROUP, 1)).reshape(cw, 1)
        yt = yt + d_rows * u_g.astype(_F32)
        yt_ref[:, gl * SSM_GROUP:(gl + 1) * SSM_GROUP, :] = yt.reshape(
            el, SSM_GROUP, n_cols).astype(_BF)


def _ssm(ut, lam, bt, ct, dsk, *, bsz):
    _, u_w, n_cols = ut.shape
    n_c = n_cols // bsz
    n_pairs = u_w // (2 * SSM_GROUP)
    cw = SSM_CHUNK * SSM_GROUP
    sw = 2 * SSM_STATE
    kern = functools.partial(_ssm_kernel, bsz=bsz, n_c=n_c)
    blk = pl.BlockSpec((SSM_CHUNK, 2 * SSM_GROUP, n_cols), lambda g: (0, g, 0))
    par = pl.BlockSpec((1, 2, 2, SSM_GROUP, sw), lambda g: (g, 0, 0, 0, 0))
    return pl.pallas_call(
        kern,
        grid=(n_pairs,),
        in_specs=[blk,
                  pl.BlockSpec((1, 2, 3, sw), lambda g: (g, 0, 0, 0)),
                  par, par,
                  pl.BlockSpec((1, 2, SSM_GROUP, 1), lambda g: (g, 0, 0, 0))],
        out_specs=blk,
        out_shape=jax.ShapeDtypeStruct(ut.shape, _BF),
        scratch_shapes=[pltpu.VMEM((2, n_cols, cw), _BF),
                        pltpu.VMEM((4, n_cols, sw), _F32),
                        pltpu.VMEM((4, n_cols, sw), _F32)],
        compiler_params=pltpu.CompilerParams(
            dimension_semantics=("arbitrary",), vmem_limit_bytes=VMEM_LIMIT_BYTES),
        name="s5_chunked",
    )(ut, lam, bt, ct, dsk)


def _post_kernel(o_ref, gates_ref, y_ref, x_ref, mod_ref, bglu_ref, gfin_ref,
                 wua_ref, wglu_ref, wus_ref, wout_ref, out_ref, *, d, aw, sw):
    y = jax.nn.gelu(y_ref[0].astype(_F32))
    y = y * _sigmoid(_dot(y.astype(_BF), wglu_ref[...]) + bglu_ref[...])
    s_br = _dot((y * gates_ref[0, :, aw:aw + sw].astype(_F32)).astype(_BF), wus_ref[...])
    o = jnp.concatenate([o_ref[0, hd] for hd in range(o_ref.shape[1])], axis=1)
    a_br = _dot((o.astype(_F32) * gates_ref[0, :, 0:aw].astype(_F32)).astype(_BF), wua_ref[...])
    g_off = aw + sw
    merged = (gates_ref[0, :, g_off:g_off + d].astype(_F32) * a_br
              + gates_ref[0, :, g_off + d:g_off + 2 * d].astype(_F32) * s_br)
    r = _dot(merged.astype(_BF), wout_ref[...])
    xo = x_ref[0] + mod_ref[0, :, 2 * d:3 * d] * r
    inv = lax.rsqrt(jnp.mean(xo * xo, axis=-1, keepdims=True) + EPS)
    out_ref[0] = (xo * inv) * gfin_ref[...]


def _post(o, gates, y, x, mod3, b_glu, g_final, wua, wglu, wus, wout, *, rows):
    bsz, s, d = x.shape
    _, n_heads, _, vd = o.shape
    aw = n_heads * vd
    sw = y.shape[2]
    kern = functools.partial(_post_kernel, d=d, aw=aw, sw=sw)
    full = lambda a: pl.BlockSpec(a.shape, lambda b, t: (0,) * a.ndim)
    seq_spec = lambda w: pl.BlockSpec((1, rows, w), lambda b, t: (b, t, 0))
    bgl = b_glu.reshape(1, sw)
    gfi = g_final.reshape(1, d)
    return pl.pallas_call(
        kern,
        grid=(bsz, s // rows),
        in_specs=[pl.BlockSpec((1, n_heads, rows, vd), lambda b, t: (b, 0, t, 0)),
                  seq_spec(gates.shape[2]), seq_spec(sw), seq_spec(d),
                  pl.BlockSpec((1, 1, 3 * d), lambda b, t: (b, 0, 0)),
                  full(bgl), full(gfi), full(wua), full(wglu), full(wus), full(wout)],
        out_specs=seq_spec(d),
        out_shape=jax.ShapeDtypeStruct(x.shape, _F32),
        compiler_params=pltpu.CompilerParams(
            dimension_semantics=("arbitrary", "arbitrary"), vmem_limit_bytes=VMEM_LIMIT_BYTES),
        name="merge_out",
    )(o, gates, y, x, mod3, bgl, gfi, wua, wglu, wus, wout)


def _rope_constants():
    half = ROT_DIM // 2
    freq = (ROPE_THETA ** (-jnp.arange(half, dtype=_F32) * 2.0 / ROT_DIM)).reshape(half, 1)
    in_head = jnp.arange(LANES) % HEAD_DIM
    j = jnp.arange(half)[:, None]
    cos_sel = ((in_head == j) | (in_head == half + j)).astype(_F32)
    sin_sel = (in_head == half + j).astype(_F32) - (in_head == j).astype(_F32)
    zero = jnp.zeros_like(cos_sel)
    return freq, jnp.concatenate([jnp.concatenate([cos_sel, zero], axis=1),
                                  jnp.concatenate([zero, sin_sel], axis=1)], axis=0)


def _ssm_params(lam_re, lam_im, log_dt, b_re, b_im, c_re, c_im):
    n_g, n_p = lam_re.shape[1], lam_re.shape[2]
    pairs = n_g // 2

    def lanes(m):
        rows = m.shape[2]
        return m.reshape(2, pairs, 2, rows, n_p).transpose(1, 0, 3, 2, 4).reshape(pairs, 2, rows, 2 * n_p)

    lam = lanes(jnp.stack([lam_re, lam_im, jnp.broadcast_to(log_dt[..., None], lam_re.shape)],
                          axis=2).astype(_F32))
    bt = jnp.stack([lanes(jnp.swapaxes(b_re, -1, -2)), lanes(jnp.swapaxes(b_im, -1, -2))], axis=2)
    ct = jnp.stack([lanes(c_re), lanes(c_im)], axis=2)
    return lam, bt.astype(_F32), ct.astype(_F32)


def _layer(x, c, positions, layer_idx, w_ada, b_ada, g_pre, w_in, lam_qk, g_subln,
           lam_re, lam_im, log_dt, b_re, b_im, c_re, c_im, d_skip,
           w_glu, b_glu, w_up_attn, w_up_ssm, w_out, g_final):
    bsz, s, d = x.shape
    n_c = s // SSM_CHUNK
    qk_w = ATTN_HEADS * 2 * HEAD_DIM
    v_w = ATTN_HEADS * V_HEAD_DIM
    u_w = lam_re.shape[1] * SSM_GROUP
    u_off = 2 * qk_w + 2 * v_w
    lam_init = 0.8 - 0.6 * math.exp(-0.3 * layer_idx)
    mod, lam, w_in_bf, wvt_bf = _prepare(c, w_ada, b_ada, lam_qk, w_in, lam_init=lam_init,
                                         v_off=2 * qk_w, v_w=v_w)
    mod3 = mod.reshape(bsz, 1, 3 * d)
    rows = min(ROW_BLOCK, s)
    pos_rows = positions.astype(_F32).reshape(bsz * (s // rows), 1, rows)
    qt, k, vt, gates, u = _inproj(x, mod3, g_pre, w_in_bf, wvt_bf, pos_rows, rows=rows,
                                  qk_w=qk_w, v_w=v_w, u_off=u_off, u_w=u_w)
    o = _attention(lam, g_subln, qt, k, vt, tq=min(Q_ROWS, s), lam_init=lam_init)
    ut = u.reshape(bsz, n_c, SSM_CHUNK, u_w).transpose(2, 3, 1, 0).reshape(SSM_CHUNK, u_w, n_c * bsz)
    dsk = d_skip.astype(_F32).reshape(u_w // (2 * SSM_GROUP), 2, SSM_GROUP, 1)
    yt = _ssm(ut, *_ssm_params(lam_re, lam_im, log_dt, b_re, b_im, c_re, c_im), dsk, bsz=bsz)
    y = yt.reshape(SSM_CHUNK, u_w, n_c, bsz).transpose(3, 2, 0, 1).reshape(bsz, s, u_w)
    return _post(o, gates, y, x, mod3, b_glu, g_final,
                 w_up_attn.astype(_BF), w_glu.astype(_BF), w_up_ssm.astype(_BF), w_out.astype(_BF),
                 rows=rows)


def kernel(x, c, positions, w_ada, b_ada, g_pre, w_in, lam_qk, g_subln, ssm_lam_re, ssm_lam_im,
           ssm_log_dt, ssm_b_re, ssm_b_im, ssm_c_re, ssm_c_im, ssm_d, w_glu, b_glu, w_up_attn,
           w_up_ssm, w_out, g_final):
    depth = w_ada.shape[0]
    assert depth == 1, "the final RMSNorm is fused into the single layer's epilogue"
    assert x.shape[1] % (SSM_CHUNK * 8) == 0
    return _layer(x, c, positions, 0, w_ada[0], b_ada[0], g_pre[0], w_in[0],
                  lam_qk[0], g_subln[0], ssm_lam_re[0], ssm_lam_im[0], ssm_log_dt[0], ssm_b_re[0],
                  ssm_b_im[0], ssm_c_re[0], ssm_c_im[0], ssm_d[0], w_glu[0], b_glu[0],
                  w_up_attn[0], w_up_ssm[0], w_out[0], g_final)
```

```python
import functools
import math

import jax
import jax.numpy as jnp
import numpy as np
from jax import lax
from jax.experimental import pallas as pl
from jax.experimental.pallas import tpu as pltpu

ATTN_HEADS = 4
HEAD_DIM = 64
V_HEAD_DIM = 2 * HEAD_DIM
ROT_DIM = HEAD_DIM // 4
ROPE_THETA = 500000.0
SSM_GROUP = 16
SSM_STATE = 64
SSM_CHUNK = 16
EPS = 1e-6
LOG2_E = math.log2(math.e)
LANES = 128
VMEM_LIMIT_BYTES = 56 * 1024 * 1024
ROW_BLOCK = 1024
Q_ROWS = 1024
KEY_BLOCK = 128
VT_PAD = 16

_HI = lax.Precision.HIGHEST
_BF = jnp.bfloat16
_F32 = jnp.float32


def _nt_dot(a, b):
    return lax.dot_general(a, b, (((1,), (1,)), ((), ())), preferred_element_type=_F32)


def _dot(a, b):
    return jnp.dot(a, b, preferred_element_type=_F32)


def _sigmoid(x):
    return 0.5 * jnp.tanh(0.5 * x) + 0.5


def _prep_kernel(c_ref, w_ref, b_ref, lamqk_ref, win_ref, o_ref, lam_ref, winb_ref, wvt_ref,
                 *, lam_init, n_mod, v_tile):
    j = pl.program_id(0)

    @pl.when(j < n_mod)
    def _():
        o_ref[...] = jnp.dot(jax.nn.silu(c_ref[...]), w_ref[...], precision=_HI,
                             preferred_element_type=_F32) + b_ref[...]

    @pl.when(j == 0)
    def _():
        lf = lamqk_ref[...]
        lam = (jnp.exp(jnp.sum(lf[0:1, :] * lf[1:2, :], axis=-1, keepdims=True))
               - jnp.exp(jnp.sum(lf[2:3, :] * lf[3:4, :], axis=-1, keepdims=True)) + lam_init)
        lam_ref[...] = jnp.broadcast_to(lam, lam_ref.shape)

    w = win_ref[...]
    winb_ref[...] = w.astype(_BF)

    @pl.when(j == v_tile)
    def _():
        wvt_ref[...] = w.T.astype(_BF)


def _prepare(c, w_ada, b_ada, lam_qk, w_in, *, lam_init, v_off, v_w):
    bsz, d = c.shape
    n = w_ada.shape[1]
    n_in = w_in.shape[1]
    tn = d
    n_mod = n // tn
    assert v_off % v_w == 0 and n_in % v_w == 0 and n_in // v_w >= n_mod
    last = n_mod - 1
    return pl.pallas_call(
        functools.partial(_prep_kernel, lam_init=lam_init, n_mod=n_mod, v_tile=v_off // v_w),
        grid=(n_in // v_w,),
        in_specs=[pl.BlockSpec((bsz, d), lambda j: (0, 0)),
                  pl.BlockSpec((d, tn), lambda j: (0, jnp.minimum(j, last))),
                  pl.BlockSpec((1, tn), lambda j: (0, jnp.minimum(j, last))),
                  pl.BlockSpec(lam_qk.shape, lambda j: (0, 0)),
                  pl.BlockSpec((d, v_w), lambda j: (0, j))],
        out_specs=[pl.BlockSpec((bsz, tn), lambda j: (0, jnp.minimum(j, last))),
                   pl.BlockSpec((1, LANES), lambda j: (0, 0)),
                   pl.BlockSpec((d, v_w), lambda j: (0, j)),
                   pl.BlockSpec((v_w, d), lambda j: (0, 0))],
        out_shape=[jax.ShapeDtypeStruct((bsz, n), _F32), jax.ShapeDtypeStruct((1, LANES), _F32),
                   jax.ShapeDtypeStruct((d, n_in), _BF), jax.ShapeDtypeStruct((v_w, d), _BF)],
        compiler_params=pltpu.CompilerParams(
            dimension_semantics=("arbitrary",), vmem_limit_bytes=VMEM_LIMIT_BYTES),
        name="adaln_mod",
    )(c, w_ada, b_ada.reshape(1, n), lam_qk, w_in)


def _inproj_kernel(x_ref, mod_ref, gpre_ref, w_ref, wvt_ref, pos_ref, freq_ref, spread_ref,
                   qt_ref, k_ref, vt_ref, gates_ref, u_ref, *, d, qk_w, v_w, u_off, u_w):
    shift = mod_ref[0, :, 0:d]
    scale = mod_ref[0, :, d:2 * d]
    x = x_ref[0]
    inv = lax.rsqrt(jnp.mean(x * x, axis=-1, keepdims=True) + EPS)
    h = ((x * inv) * (gpre_ref[...] * (1.0 + scale)) + shift).astype(_BF)

    rows = x.shape[0]
    half = ROT_DIM // 2
    ang = freq_ref[...] * pos_ref[0]
    cs = jnp.concatenate([jnp.cos(ang), jnp.sin(ang)], axis=0)
    pieces = []
    rest = cs
    for _ in range(3):
        piece = rest.astype(_BF).astype(_F32)
        rest = rest - piece
        pieces.append(piece)
    sel = spread_ref[...].astype(_BF)
    tab = _dot(jnp.concatenate(pieces, axis=0).T.astype(_BF), jnp.concatenate([sel] * 3, axis=0))
    lane = lax.broadcasted_iota(jnp.int32, (1, LANES), 1)
    rot_c = jnp.where(lane % HEAD_DIM < ROT_DIM, tab[:, 0:LANES], 1.0)
    rot_s = tab[:, LANES:2 * LANES]
    low = lane % HEAD_DIM < half

    def rope(t):
        out = []
        for j in range(t.shape[1] // LANES):
            tj = t[:, j * LANES:(j + 1) * LANES]
            up = pltpu.roll(tj, LANES - half, 1)
            dn = pltpu.roll(tj, half, 1)
            out.append(tj * rot_c + jnp.where(low, up, dn) * rot_s)
        return jnp.concatenate(out, axis=1)

    v_off = 2 * qk_w
    vt = _nt_dot(wvt_ref[...], h).astype(_BF)
    pad = (lax.broadcasted_iota(jnp.int32, (VT_PAD, rows), 0) == 0).astype(_BF)
    for hd in range(v_w // V_HEAD_DIM):
        vt_ref[0, hd, 0:V_HEAD_DIM, :] = vt[hd * V_HEAD_DIM:(hd + 1) * V_HEAD_DIM, :]
        vt_ref[0, hd, V_HEAD_DIM:V_HEAD_DIM + VT_PAD, :] = pad
    u_ref[0] = _dot(h, w_ref[:, u_off:u_off + u_w]).astype(_BF)
    qt = rope(_dot(h, w_ref[:, 0:qk_w]) * (HEAD_DIM ** -0.5 * LOG2_E)).T.astype(_BF)
    hw = 2 * HEAD_DIM
    for hd in range(qk_w // hw):
        qt_ref[0, hd] = qt[hd * hw:(hd + 1) * hw, :]
    kk = rope(_dot(h, w_ref[:, qk_w:2 * qk_w])).astype(_BF)
    for hd in range(qk_w // hw):
        k_ref[0, hd] = kk[:, hd * hw:(hd + 1) * hw]
    za_off = v_off + v_w
    z_a = _dot(h, w_ref[:, za_off:za_off + v_w])
    gates_ref[0, :, 0:v_w] = (z_a * _sigmoid(z_a)).astype(_BF)
    zs_off = u_off + u_w
    z_s = _dot(h, w_ref[:, zs_off:zs_off + u_w])
    gates_ref[0, :, v_w:v_w + u_w] = (z_s * _sigmoid(z_s)).astype(_BF)
    g_off = zs_off + u_w
    rest = w_ref.shape[1] - g_off
    piece = 512
    for p in range(rest // piece):
        gates_ref[0, :, v_w + u_w + p * piece:v_w + u_w + (p + 1) * piece] = _sigmoid(_dot(
            h, w_ref[:, g_off + p * piece:g_off + (p + 1) * piece])).astype(_BF)


def _inproj(x, mod3, g_pre, w_in_bf, wvt_bf, pos_rows, *, rows, qk_w, v_w, u_off, u_w):
    bsz, s, d = x.shape
    n_in = w_in_bf.shape[1]
    gates_w = n_in - 2 * qk_w - v_w - u_w
    n_heads = v_w // V_HEAD_DIM
    vt_rows = V_HEAD_DIM + VT_PAD
    kern = functools.partial(_inproj_kernel, d=d, qk_w=qk_w, v_w=v_w, u_off=u_off, u_w=u_w)
    freq, spread = _rope_constants()
    seq_spec = lambda w: pl.BlockSpec((1, rows, w), lambda b, t: (b, t, 0))
    return pl.pallas_call(
        kern,
        grid=(bsz, s // rows),
        in_specs=[seq_spec(d),
                  pl.BlockSpec((1, 1, 3 * d), lambda b, t: (b, 0, 0)),
                  pl.BlockSpec((1, d), lambda b, t: (0, 0)),
                  pl.BlockSpec((d, n_in), lambda b, t: (0, 0), pipeline_mode=pl.Buffered(1)),
                  pl.BlockSpec((v_w, d), lambda b, t: (0, 0), pipeline_mode=pl.Buffered(1)),
                  pl.BlockSpec((1, 1, rows), lambda b, t: (b * (s // rows) + t, 0, 0)),
                  pl.BlockSpec(freq.shape, lambda b, t: (0, 0)),
                  pl.BlockSpec(spread.shape, lambda b, t: (0, 0))],
        out_specs=[pl.BlockSpec((1, n_heads, 2 * HEAD_DIM, rows), lambda b, t: (b, 0, 0, t)),
                   pl.BlockSpec((1, n_heads, rows, 2 * HEAD_DIM), lambda b, t: (b, 0, t, 0)),
                   pl.BlockSpec((1, n_heads, vt_rows, rows), lambda b, t: (b, 0, 0, t)),
                   seq_spec(gates_w), seq_spec(u_w)],
        out_shape=[jax.ShapeDtypeStruct((bsz, n_heads, 2 * HEAD_DIM, s), _BF),
                   jax.ShapeDtypeStruct((bsz, n_heads, s, 2 * HEAD_DIM), _BF),
                   jax.ShapeDtypeStruct((bsz, n_heads, vt_rows, s), _BF),
                   jax.ShapeDtypeStruct((bsz, s, gates_w), _BF),
                   jax.ShapeDtypeStruct((bsz, s, u_w), _BF)],
        compiler_params=pltpu.CompilerParams(
            dimension_semantics=("arbitrary", "arbitrary"), vmem_limit_bytes=VMEM_LIMIT_BYTES),
        name="norm_inproj",
    )(x, mod3, g_pre.reshape(1, d), w_in_bf, wvt_bf, pos_rows, freq, spread)


def _attn_kernel(tab_ref, lam_ref, gsub_ref, qt_ref, k_ref, vt_ref, o_ref, *scratch, lam_init):
    s_ref = (scratch[0:2], scratch[2:4])
    e_ref = (scratch[4:6], scratch[6:8])
    m_ref = (scratch[8:10], scratch[10:12])
    ev_ref = (scratch[12:14], scratch[14:16])
    n_keys, tq = s_ref[0][0].shape
    tk = KEY_BLOCK
    n_kb = n_keys // tk
    step = pl.program_id(0)

    @pl.when(step == 0)
    def _():
        for ref in scratch:
            ref[...] = jnp.zeros_like(ref)

    lam = lam_ref[0:1, 0:1]

    def tick(new, old):
        ot = None
        for mp in range(2):
            ev = ev_ref[old][mp][...]
            nrm = ev[0:V_HEAD_DIM, :] * (1.0 / ev[V_HEAD_DIM:V_HEAD_DIM + 1, :])
            ot = nrm if ot is None else ot - lam * nrm
        inv = lax.rsqrt(jnp.mean(ot * ot, axis=0, keepdims=True) + EPS)
        o_ref[0, 0] = (((ot * inv) * gsub_ref[...]) * (1.0 - lam_init)).T.astype(_BF)
        qt = qt_ref[0, 0]
        first = lax.broadcasted_iota(jnp.int32, (2 * HEAD_DIM, 1), 0) < HEAD_DIM
        zero = jnp.zeros_like(qt)
        qmt = (jnp.where(first, qt, zero), jnp.where(first, zero, qt))
        m_rows = [jnp.max(m_ref[old][mp][...], axis=0, keepdims=True) for mp in range(2)]
        n_part = 4 if n_kb % 4 == 0 else 1
        for mp in range(2):
            ev = None
            m = jnp.full((8, tq), -jnp.inf, _F32)
            for kb in range(n_kb):
                keys = slice(kb * tk, (kb + 1) * tk)
                if kb % (n_kb // n_part) == 0:
                    part_keys = slice(kb * tk, (kb + n_kb // n_part) * tk)
                    part = _dot(vt_ref[0, 0, :, part_keys], e_ref[new][mp][part_keys, :])
                    ev = part if ev is None else ev + part
                e_ref[old][mp][keys, :] = jnp.exp2(s_ref[old][mp][keys, :] - m_rows[mp]).astype(_BF)
                st = _dot(k_ref[0, 0, keys, :], qmt[mp])
                s_ref[new][mp][keys, :] = st
                m = jnp.maximum(m, jnp.max(st.reshape(tk // 8, 8, tq), axis=0))
            m_ref[new][mp][...] = m
            ev_ref[new][mp][...] = ev

    @pl.when(step % 2 == 0)
    def _():
        tick(0, 1)

    @pl.when(step % 2 == 1)
    def _():
        tick(1, 0)


def _attention(lam, g_subln, qt, k, vt, *, tq, lam_init):
    bsz, n_heads, hw, s = qt.shape
    _, _, vt_rows, _ = vt.shape
    n_q = s // tq
    n_items = bsz * n_heads * n_q
    kern = functools.partial(_attn_kernel, lam_init=lam_init)
    n_steps = n_items + 3

    def item(j):
        j = np.clip(j, 0, n_items - 1)
        return j // (n_heads * n_q), (j // n_q) % n_heads, j % n_q

    g = np.arange(n_steps)
    tab = jnp.asarray(np.stack([*item(g), *item(g - 2), *item(g - 3)]).astype(np.int32))

    return pl.pallas_call(
        kern,
        grid_spec=pltpu.PrefetchScalarGridSpec(
            num_scalar_prefetch=1,
            grid=(n_steps,),
            in_specs=[pl.BlockSpec(lam.shape, lambda g, t: (0, 0)),
                      pl.BlockSpec((V_HEAD_DIM, 1), lambda g, t: (0, 0)),
                      pl.BlockSpec((1, 1, hw, tq), lambda g, t: (t[0, g], t[1, g], 0, t[2, g])),
                      pl.BlockSpec((1, 1, s, hw), lambda g, t: (t[0, g], t[1, g], 0, 0)),
                      pl.BlockSpec((1, 1, vt_rows, s), lambda g, t: (t[3, g], t[4, g], 0, 0))],
            out_specs=pl.BlockSpec((1, 1, tq, V_HEAD_DIM), lambda g, t: (t[6, g], t[7, g], t[8, g], 0)),
            scratch_shapes=([pltpu.VMEM((s, tq), _F32)] * 4 + [pltpu.VMEM((s, tq), _BF)] * 4
                            + [pltpu.VMEM((8, tq), _F32)] * 4 + [pltpu.VMEM((vt_rows, tq), _F32)] * 4)),
        out_shape=jax.ShapeDtypeStruct((bsz, n_heads, s, V_HEAD_DIM), _BF),
        compiler_params=pltpu.CompilerParams(
            dimension_semantics=("arbitrary",), vmem_limit_bytes=VMEM_LIMIT_BYTES),
        name="diff_attention",
    )(tab, lam, g_subln.reshape(V_HEAD_DIM, 1), qt, k, vt)


def _ssm_kernel(ut_ref, lam_ref, bt_ref, ct_ref, dsk_ref, yt_ref, at_ref, z_ref, xp_ref,
                *, bsz, n_c):
    el = SSM_CHUNK
    cw = el * SSM_GROUP
    sw = 2 * SSM_STATE
    n_cols = n_c * bsz
    in_group = [lax.broadcasted_iota(jnp.int32, (1, sw), 1) // SSM_STATE == gl for gl in range(2)]

    def group_rows(ref, gl):
        return ref[:, gl * SSM_GROUP:(gl + 1) * SSM_GROUP, :].reshape(cw, n_cols)

    def per_row(rows, cols):
        return (jnp.broadcast_to(rows[:, None, :], (el, SSM_GROUP, sw)),
                jnp.broadcast_to(cols[None, :, :], (el, SSM_GROUP, sw)))

    def cmul_rows(p_r, p_i, m_r, m_i):
        pr3, mr3 = per_row(p_r, m_r)
        pi3, mi3 = per_row(p_i, m_i)
        return (pr3 * mr3 - pi3 * mi3).reshape(cw, sw), (pr3 * mi3 + pi3 * mr3).reshape(cw, sw)

    idx = lax.broadcasted_iota(jnp.int32, (el, 1), 0).astype(_F32)
    ops = []
    for e in range(2):
        lr = jnp.minimum(lam_ref[0, e, 0:1, :], -1e-4)
        li = lam_ref[0, e, 1:2, :]
        dt = jnp.exp(lam_ref[0, e, 2:3, :])

        def power(k, lr=lr, li=li, dt=dt):
            mag = jnp.exp(k * (lr * dt))
            return mag * jnp.cos(k * (li * dt)), mag * jnp.sin(k * (li * dt))

        one_r, one_i = power(1.0)
        den = lr * lr + li * li
        nr, ni = one_r - 1.0, one_i
        coef_r = (nr * lr + ni * li) / den
        coef_i = (ni * lr - nr * li) / den
        b_r, b_i = bt_ref[0, e, 0], bt_ref[0, e, 1]
        bb_r = coef_r * b_r - coef_i * b_i
        bb_i = coef_r * b_i + coef_i * b_r
        c_r, c_i = ct_ref[0, e, 0], ct_ref[0, e, 1]
        ops.append(dict(power=power, bb=(bb_r, bb_i), c=(c_r, c_i)))
    fwd, bwd = ops

    zf_r, zf_i = cmul_rows(*fwd["power"](el - 1.0 - idx), *fwd["bb"])
    zb_r, zb_i = cmul_rows(*bwd["power"](idx), *bwd["bb"])
    yf_r, yf_i = cmul_rows(*fwd["power"](idx + 1.0), *fwd["c"])
    yb_r, yb_i = cmul_rows(*bwd["power"](el - idx), *bwd["c"])
    kf_r, kf_i = cmul_rows(*fwd["power"](idx), *fwd["c"])
    kb_r, kb_i = cmul_rows(*bwd["power"](idx), *bwd["c"])

    def pair_slot(parts, gl):
        return jnp.concatenate([jnp.where(in_group[gl], p, 0.0) for p in parts], axis=1).astype(_BF)

    for gl in range(2):
        at_ref[gl] = group_rows(ut_ref, gl).T
    z = (_dot(at_ref[0], pair_slot([zf_r, zf_i, zb_r, zb_i], 0))
         + _dot(at_ref[1], pair_slot([zf_r, zf_i, zb_r, zb_i], 1)))
    for comp in range(4):
        z_ref[comp] = z[:, comp * sw:(comp + 1) * sw]

    a_fr, a_fi = fwd["power"](float(el))
    a_br, a_bi = bwd["power"](float(el))

    def step(c, carry):
        fr, fi, br, bi = carry
        rows_f = pl.ds(pl.multiple_of(c * bsz, bsz), bsz)
        rows_b = pl.ds(pl.multiple_of((n_c - 1 - c) * bsz, bsz), bsz)
        xp_ref[0, rows_f, :] = fr
        xp_ref[1, rows_f, :] = fi
        xp_ref[2, rows_b, :] = br
        xp_ref[3, rows_b, :] = bi
        zfr = z_ref[0, rows_f, :]
        zfi = z_ref[1, rows_f, :]
        zbr = z_ref[2, rows_b, :]
        zbi = z_ref[3, rows_b, :]
        return (a_fr * fr - a_fi * fi + zfr, a_fr * fi + a_fi * fr + zfi,
                a_br * br - a_bi * bi + zbr, a_br * bi + a_bi * br + zbi)

    zero = jnp.zeros((bsz, sw), _F32)
    lax.fori_loop(0, n_c, step, (zero, zero, zero, zero), unroll=4)
    xp = jnp.concatenate([xp_ref[comp] for comp in range(4)], axis=1).astype(_BF)

    spread = (lax.broadcasted_iota(jnp.int32, (SSM_GROUP, cw), 1) % SSM_GROUP
              == lax.broadcasted_iota(jnp.int32, (SSM_GROUP, cw), 0)).astype(_BF)
    per_tile = LANES // SSM_GROUP
    t_in_tile = lax.broadcasted_iota(jnp.int32, (1, LANES), 1) // SSM_GROUP

    def lag_kernels(k_r, k_i, bb, gl):
        lhs = jnp.concatenate([jnp.where(in_group[gl], k_r, 0.0),
                               jnp.where(in_group[gl], -k_i, 0.0)], axis=1).astype(_BF)
        rhs = jnp.concatenate([bb[0], bb[1]], axis=1).astype(_BF)
        return _nt_dot(lhs, rhs)

    for gl in range(2):
        ktf = _dot(lag_kernels(kf_r, kf_i, fwd["bb"], gl).astype(_BF), spread)
        ktb = _dot(lag_kernels(kb_r, kb_i, bwd["bb"], gl).astype(_BF), spread)

        def column_block(tile, t):
            lanes = slice(tile * LANES, (tile + 1) * LANES)

            def blk(kt, k):
                return kt[k * SSM_GROUP:(k + 1) * SSM_GROUP, lanes]

            rows = [blk(ktb, t - s_) if s_ < t else blk(ktf, s_ - t) if s_ > t
                    else blk(ktf, 0) + blk(ktb, 0) for s_ in range(el)]
            return jnp.concatenate(rows, axis=0)

        tiles = []
        for tile in range(cw // LANES):
            acc = column_block(tile, tile * per_tile)
            for j in range(1, per_tile):
                acc = jnp.where(t_in_tile == j, column_block(tile, tile * per_tile + j), acc)
            tiles.append(acc)
        tm = jnp.concatenate(tiles, axis=1)
        u_g = group_rows(ut_ref, gl)
        yt = _dot(tm.astype(_BF), u_g) + _nt_dot(pair_slot([yf_r, -yf_i, yb_r, -yb_i], gl), xp)
        d_rows = jnp.broadcast_to(dsk_ref[0, gl][None], (el, SSM_GROUP, 1)).reshape(cw, 1)
        yt = yt + d_rows * u_g.astype(_F32)
        yt_ref[:, gl * SSM_GROUP:(gl + 1) * SSM_GROUP, :] = yt.reshape(
            el, SSM_GROUP, n_cols).astype(_BF)


def _ssm(ut, lam, bt, ct, dsk, *, bsz):
    _, u_w, n_cols = ut.shape
    n_c = n_cols // bsz
    n_pairs = u_w // (2 * SSM_GROUP)
    cw = SSM_CHUNK * SSM_GROUP
    sw = 2 * SSM_STATE
    kern = functools.partial(_ssm_kernel, bsz=bsz, n_c=n_c)
    blk = pl.BlockSpec((SSM_CHUNK, 2 * SSM_GROUP, n_cols), lambda g: (0, g, 0))
    par = pl.BlockSpec((1, 2, 2, SSM_GROUP, sw), lambda g: (g, 0, 0, 0, 0))
    return pl.pallas_call(
        kern,
        grid=(n_pairs,),
        in_specs=[blk,
                  pl.BlockSpec((1, 2, 3, sw), lambda g: (g, 0, 0, 0)),
                  par, par,
                  pl.BlockSpec((1, 2, SSM_GROUP, 1), lambda g: (g, 0, 0, 0))],
        out_specs=blk,
        out_shape=jax.ShapeDtypeStruct(ut.shape, _BF),
        scratch_shapes=[pltpu.VMEM((2, n_cols, cw), _BF),
                        pltpu.VMEM((4, n_cols, sw), _F32),
                        pltpu.VMEM((4, n_cols, sw), _F32)],
        compiler_params=pltpu.CompilerParams(
            dimension_semantics=("arbitrary",), vmem_limit_bytes=VMEM_LIMIT_BYTES),
        name="s5_chunked",
    )(ut, lam, bt, ct, dsk)


def _post_kernel(o_ref, gates_ref, y_ref, x_ref, mod_ref, bglu_ref, gfin_ref,
                 wua_ref, wglu_ref, wus_ref, wout_ref, out_ref, *, d, aw, sw):
    y = jax.nn.gelu(y_ref[0].astype(_F32))
    y = y * _sigmoid(_dot(y.astype(_BF), wglu_ref[...]) + bglu_ref[...])
    ys = (y * gates_ref[0, :, aw:aw + sw].astype(_F32)).astype(_BF)
    o = jnp.concatenate([o_ref[0, hd] for hd in range(o_ref.shape[1])], axis=1)
    oa = (o.astype(_F32) * gates_ref[0, :, 0:aw].astype(_F32)).astype(_BF)
    g_off = aw + sw
    parts = []
    for c0 in range(0, d, d // 2):
        cs = slice(c0, c0 + d // 2)
        s_br = _dot(ys, wus_ref[:, cs])
        a_br = _dot(oa, wua_ref[:, cs])
        parts.append((gates_ref[0, :, g_off + c0:g_off + c0 + d // 2].astype(_F32) * a_br
                      + gates_ref[0, :, g_off + d + c0:g_off + d + c0 + d // 2].astype(_F32) * s_br
                      ).astype(_BF))
    r = _dot(jnp.concatenate(parts, axis=1), wout_ref[...])
    xo = x_ref[0] + mod_ref[0, :, 2 * d:3 * d] * r
    inv = lax.rsqrt(jnp.mean(xo * xo, axis=-1, keepdims=True) + EPS)
    out_ref[0] = (xo * inv) * gfin_ref[...]


def _post(o, gates, y, x, mod3, b_glu, g_final, wua, wglu, wus, wout, *, rows):
    bsz, s, d = x.shape
    _, n_heads, _, vd = o.shape
    aw = n_heads * vd
    sw = y.shape[2]
    kern = functools.partial(_post_kernel, d=d, aw=aw, sw=sw)
    full = lambda a: pl.BlockSpec(a.shape, lambda b, t: (0,) * a.ndim)
    seq_spec = lambda w: pl.BlockSpec((1, rows, w), lambda b, t: (b, t, 0))
    bgl = b_glu.reshape(1, sw)
    gfi = g_final.reshape(1, d)
    return pl.pallas_call(
        kern,
        grid=(bsz, s // rows),
        in_specs=[pl.BlockSpec((1, n_heads, rows, vd), lambda b, t: (b, 0, t, 0)),
                  seq_spec(gates.shape[2]), seq_spec(sw), seq_spec(d),
                  pl.BlockSpec((1, 1, 3 * d), lambda b, t: (b, 0, 0)),
                  full(bgl), full(gfi), full(wua), full(wglu), full(wus), full(wout)],
        out_specs=seq_spec(d),
        out_shape=jax.ShapeDtypeStruct(x.shape, _F32),
        compiler_params=pltpu.CompilerParams(
            dimension_semantics=("arbitrary", "arbitrary"), vmem_limit_bytes=VMEM_LIMIT_BYTES),
        name="merge_out",
    )(o, gates, y, x, mod3, bgl, gfi, wua, wglu, wus, wout)


def _rope_constants():
    half = ROT_DIM // 2
    freq = (ROPE_THETA ** (-jnp.arange(half, dtype=_F32) * 2.0 / ROT_DIM)).reshape(half, 1)
    in_head = jnp.arange(LANES) % HEAD_DIM
    j = jnp.arange(half)[:, None]
    cos_sel = ((in_head == j) | (in_head == half + j)).astype(_F32)
    sin_sel = (in_head == half + j).astype(_F32) - (in_head == j).astype(_F32)
    zero = jnp.zeros_like(cos_sel)
    return freq, jnp.concatenate([jnp.concatenate([cos_sel, zero], axis=1),
                                  jnp.concatenate([zero, sin_sel], axis=1)], axis=0)


def _ssm_params(lam_re, lam_im, log_dt, b_re, b_im, c_re, c_im):
    n_g, n_p = lam_re.shape[1], lam_re.shape[2]
    pairs = n_g // 2

    def lanes(m):
        rows = m.shape[2]
        return m.reshape(2, pairs, 2, rows, n_p).transpose(1, 0, 3, 2, 4).reshape(pairs, 2, rows, 2 * n_p)

    lam = lanes(jnp.stack([lam_re, lam_im, jnp.broadcast_to(log_dt[..., None], lam_re.shape)],
                          axis=2).astype(_F32))
    bt = jnp.stack([lanes(jnp.swapaxes(b_re, -1, -2)), lanes(jnp.swapaxes(b_im, -1, -2))], axis=2)
    ct = jnp.stack([lanes(c_re), lanes(c_im)], axis=2)
    return lam, bt.astype(_F32), ct.astype(_F32)


def _layer(x, c, positions, layer_idx, w_ada, b_ada, g_pre, w_in, lam_qk, g_subln,
           lam_re, lam_im, log_dt, b_re, b_im, c_re, c_im, d_skip,
           w_glu, b_glu, w_up_attn, w_up_ssm, w_out, g_final):
    bsz, s, d = x.shape
    n_c = s // SSM_CHUNK
    qk_w = ATTN_HEADS * 2 * HEAD_DIM
    v_w = ATTN_HEADS * V_HEAD_DIM
    u_w = lam_re.shape[1] * SSM_GROUP
    u_off = 2 * qk_w + 2 * v_w
    lam_init = 0.8 - 0.6 * math.exp(-0.3 * layer_idx)
    mod, lam, w_in_bf, wvt_bf = _prepare(c, w_ada, b_ada, lam_qk, w_in, lam_init=lam_init,
                                         v_off=2 * qk_w, v_w=v_w)
    mod3 = mod.reshape(bsz, 1, 3 * d)
    rows = min(ROW_BLOCK, s)
    pos_rows = positions.astype(_F32).reshape(bsz * (s // rows), 1, rows)
    qt, k, vt, gates, u = _inproj(x, mod3, g_pre, w_in_bf, wvt_bf, pos_rows, rows=rows,
                                  qk_w=qk_w, v_w=v_w, u_off=u_off, u_w=u_w)
    o = _attention(lam, g_subln, qt, k, vt, tq=min(Q_ROWS, s), lam_init=lam_init)
    ut = u.reshape(bsz, n_c, SSM_CHUNK, u_w).transpose(2, 3, 1, 0).reshape(SSM_CHUNK, u_w, n_c * bsz)
    dsk = d_skip.astype(_F32).reshape(u_w // (2 * SSM_GROUP), 2, SSM_GROUP, 1)
    yt = _ssm(ut, *_ssm_params(lam_re, lam_im, log_dt, b_re, b_im, c_re, c_im), dsk, bsz=bsz)
    y = yt.reshape(SSM_CHUNK, u_w, n_c, bsz).transpose(3, 2, 0, 1).reshape(bsz, s, u_w)
    return _post(o, gates, y, x, mod3, b_glu, g_final,
                 w_up_attn.astype(_BF), w_glu.astype(_BF), w_up_ssm.astype(_BF), w_out.astype(_BF),
                 rows=rows)


def kernel(x, c, positions, w_ada, b_ada, g_pre, w_in, lam_qk, g_subln, ssm_lam_re, ssm_lam_im,
           ssm_log_dt, ssm_b_re, ssm_b_im, ssm_c_re, ssm_c_im, ssm_d, w_glu, b_glu, w_up_attn,
           w_up_ssm, w_out, g_final):
    depth = w_ada.shape[0]
    assert depth == 1, "the final RMSNorm is fused into the single layer's epilogue"
    assert x.shape[1] % (SSM_CHUNK * 8) == 0
    return _layer(x, c, positions, 0, w_ada[0], b_ada[0], g_pre[0], w_in[0],
                  lam_qk[0], g_subln[0], ssm_lam_re[0], ssm_lam_im[0], ssm_log_dt[0], ssm_b_re[0],
                  ssm_b_im[0], ssm_c_re[0], ssm_c_im[0], ssm_d[0], w_glu[0], b_glu[0],
                  w_up_attn[0], w_up_ssm[0], w_out[0], g_final)
```
